```python
import jax
import jax.numpy as jnp
from jax import lax
import numpy as np

D_MODEL = 1024
BATCH = 4
SEQ = 8192
DEPTH = 2

GRID_W = 64
CTX_LEN = 256
NORM_EPS = 1e-6
N_BRANCH = 3

ML_W = D_MODEL
ML_HEADS = 4
ML_HD = ML_W // ML_HEADS
ML_CHUNK = 64
ML_CONV = 3

RK_W = D_MODEL
RK_HD = 64
RK_HEADS = RK_W // RK_HD
RK_DECAY_LORA = D_MODEL // 16
RK_A_LORA = D_MODEL // 16
RK_V_LORA = D_MODEL // 32
RK_GN_EPS = 64e-5
RK_MU_W = 3 * RK_W + 2 * RK_DECAY_LORA + 2 * RK_A_LORA

HG_W = D_MODEL
HG_EXPAND = 128
HG_HEADS = HG_W // HG_EXPAND
HG_DV = HG_W // HG_HEADS
HG_CHUNK = 16

IN_LAYOUT = (
    ('m_q', ML_W), ('m_k', ML_W), ('m_v', ML_W), ('m_o', ML_W), ('m_z', ML_W), ('m_if', 4 * ML_HEADS),
    ('r_r', RK_W), ('r_k', RK_W), ('r_v', RK_W), ('r_z', RK_W),
    ('r_wd', 2 * RK_DECAY_LORA), ('r_ad', 2 * RK_A_LORA),
    ('h_q', HG_W), ('h_f', 2 * HG_W), ('h_i', HG_W), ('h_z', HG_W),
    ('gate', N_BRANCH * D_MODEL),
)
IN_COLS = 5 * ML_W + 4 * ML_HEADS + 4 * RK_W + 2 * RK_DECAY_LORA + 2 * RK_A_LORA + 5 * HG_W + N_BRANCH * D_MODEL

kernel_name = 'hybrid_mlstm_rwkv7_hgrn2_dit'


def _in_slices():
    out, off = {}, 0
    for name, width in IN_LAYOUT:
        out[name] = (off, off + width)
        off += width
    return out


def _proj(h, w_in, name):
    lo, hi = _in_slices()[name]
    return h @ w_in[:, lo:hi]


def _rms_norm(x, w):
    xf = x.astype(jnp.float32)
    y = xf * lax.rsqrt(jnp.mean(xf * xf, axis=-1, keepdims=True) + NORM_EPS)
    return (y * w).astype(x.dtype)


def _dwconv_centered(u, w):
    k, ch = w.shape
    pad = k // 2
    return lax.conv_general_dilated(u, w[:, None, :].astype(u.dtype), window_strides=(1,),
                                    padding=((pad, pad),), dimension_numbers=('NWC', 'WIO', 'NWC'),
                                    feature_group_count=ch)


def _q_shift_grid(u):
    b, l, ch = u.shape
    rows = l // GRID_W
    g = jnp.pad(u.reshape(b, rows, GRID_W, ch), ((0, 0), (1, 1), (1, 1), (0, 0)))
    q4 = ch // 4
    parts = (g[:, 1:-1, :-2, :q4], g[:, 1:-1, 2:, q4:2 * q4],
             g[:, :-2, 1:-1, 2 * q4:3 * q4], g[:, 2:, 1:-1, 3 * q4:])
    return jnp.concatenate(parts, axis=-1).reshape(b, l, ch)


def _bi_shift_seq(u):
    half = u.shape[-1] // 2
    g = jnp.pad(u, ((0, 0), (1, 1), (0, 0)))
    return jnp.concatenate((g[:, :-2, :half], g[:, 2:, half:]), axis=-1)


def _token_shift(u, on_grid):
    return _q_shift_grid(u) if on_grid else _bi_shift_seq(u)


def _mlstm_chunk_scan(q, k, v, log_i, log_f, state):
    b_, h_, l, _ = q.shape
    t = ML_CHUNK
    nc = l // t

    def chunk(a):
        return jnp.moveaxis(a.reshape((b_, h_, nc, t) + a.shape[3:]), 2, 0)

    tri = jnp.tril(jnp.ones((t, t), dtype=bool))

    def step(carry, xs):
        c_mem, n_mem, m_prev = carry
        qc, kc, vc, ic, fc = xs
        bcum = jnp.cumsum(fc, axis=-1)
        dmat = jnp.where(tri, bcum[..., :, None] - bcum[..., None, :] + ic[..., None, :], -jnp.inf)
        inter = bcum + m_prev[..., None]
        m_t = jnp.maximum(jnp.max(dmat, axis=-1), inter)
        s = jnp.einsum('bhtd,bhsd->bhts', qc, kc) * jnp.exp(dmat - m_t[..., None])
        w_inter = jnp.exp(inter - m_t)
        num = jnp.einsum('bhts,bhse->bhte', s, vc) + w_inter[..., None] * jnp.einsum('bhtd,bhde->bhte', qc, c_mem)
        den = jnp.sum(s, axis=-1) + w_inter * jnp.einsum('bhtd,bhd->bht', qc, n_mem)
        h_out = num / jnp.maximum(jnp.abs(den), jnp.exp(-m_t))[..., None]
        g = bcum[..., -1:] - bcum + ic
        m_new = jnp.maximum(bcum[..., -1] + m_prev, jnp.max(g, axis=-1))
        wk = jnp.exp(g - m_new[..., None])
        dec = jnp.exp(bcum[..., -1] + m_prev - m_new)
        c_mem = dec[..., None, None] * c_mem + jnp.einsum('bhs,bhsd,bhse->bhde', wk, kc, vc)
        n_mem = dec[..., None] * n_mem + jnp.einsum('bhs,bhsd->bhd', wk, kc)
        return (c_mem, n_mem, m_new), h_out

    state, hs = lax.scan(step, state, (chunk(q), chunk(k), chunk(v), chunk(log_i), chunk(log_f)))
    return jnp.moveaxis(hs, 0, 2).reshape(b_, h_, l, -1), state


def _rwkv7_scan(r, w, k, v, a, b, state):
    def step(s_mem, xs):
        r_t, w_t, k_t, v_t, a_t, b_t = xs
        s_mem = (s_mem * w_t[:, :, None, :]
                 + jnp.einsum('bhij,bhj->bhi', s_mem, a_t)[..., None] * b_t[:, :, None, :]
                 + v_t[..., None] * k_t[:, :, None, :])
        return s_mem, jnp.einsum('bhij,bhj->bhi', s_mem, r_t)

    xs = tuple(jnp.moveaxis(a_, 1, 0) for a_ in (r, w, k, v, a, b))
    state, ys = lax.scan(step, state, xs)
    return jnp.moveaxis(ys, 0, 1), state


def _gla_chunk_scan(q, k, v, log_f, state):
    b_, h_, l, _ = q.shape
    t = HG_CHUNK
    nc = l // t

    def chunk(a):
        return jnp.moveaxis(a.reshape(b_, h_, nc, t, a.shape[-1]), 2, 0)

    tri = jnp.tril(jnp.ones((t, t), dtype=bool))

    def step(s_mem, xs):
        qc, kc, vc, fc = xs
        g = jnp.cumsum(fc, axis=-2)
        g_mid = g[..., t // 2 - 1:t // 2, :]
        att = jnp.einsum('bhtd,bhsd->bhts', qc * jnp.exp(g - g_mid), kc * jnp.exp(g_mid - g))
        att = jnp.where(tri, att, 0.0)
        o = jnp.einsum('bhts,bhse->bhte', att, vc) + jnp.einsum('bhtd,bhde->bhte', qc * jnp.exp(g), s_mem)
        g_last = g[..., -1:, :]
        s_mem = (jnp.exp(g_last[..., 0, :])[..., None] * s_mem
                 + jnp.einsum('bhsd,bhse->bhde', kc * jnp.exp(g_last - g), vc))
        return s_mem, o

    state, os_ = lax.scan(step, state, (chunk(q), chunk(k), chunk(v), chunk(log_f)))
    return jnp.moveaxis(os_, 0, 2).reshape(b_, h_, l, -1), state


def _mlstm_branch(h, p, init, need_out):
    b_, l, _ = h.shape
    nh, dh = ML_HEADS, ML_HD
    f32 = jnp.float32

    def heads(a):
        return a.astype(f32).reshape(b_, l, nh, dh).transpose(0, 2, 1, 3)

    q = heads(jax.nn.silu(_dwconv_centered(_proj(h, p['w_in'], 'm_q'), p['ml_conv'][0]))) * (dh ** -0.5)
    k = heads(jax.nn.silu(_dwconv_centered(_proj(h, p['w_in'], 'm_k'), p['ml_conv'][1])))
    v = heads(_proj(h, p['w_in'], 'm_v'))
    gl = (_proj(h, p['w_in'], 'm_if').reshape(b_, l, 2, 2, nh) + p['ml_if_b']).astype(f32)
    log_i = gl[:, :, :, 0].transpose(2, 0, 3, 1)
    log_f = jax.nn.log_sigmoid(gl[:, :, :, 1]).transpose(2, 0, 3, 1)
    h_f, st_f = _mlstm_chunk_scan(q, k, v, log_i[0], log_f[0], init[0])
    h_b, st_b = _mlstm_chunk_scan(jnp.flip(q, 2), jnp.flip(k, 2), jnp.flip(v, 2),
                                  jnp.flip(log_i[1], -1), jnp.flip(log_f[1], -1), init[1])
    if not need_out:
        return None, (st_f, st_b)
    y = h_f + jnp.flip(h_b, 2)
    y = y - jnp.mean(y, axis=-1, keepdims=True)
    y = y * lax.rsqrt(jnp.mean(y * y, axis=-1, keepdims=True) + NORM_EPS)
    y = y.transpose(0, 2, 1, 3).reshape(b_, l, ML_W) * p['ml_norm_w']
    o = jax.nn.sigmoid(_proj(h, p['w_in'], 'm_o').astype(f32))
    z = _proj(h, p['w_in'], 'm_z').astype(f32)
    return o * y * jax.nn.silu(z), (st_f, st_b)


def _rwkv7_branch(h, p, init, on_grid, v_first, need_out):
    b_, l, _ = h.shape
    nh, n, rd, ra = RK_HEADS, RK_HD, RK_DECAY_LORA, RK_A_LORA
    f32 = jnp.float32
    mu = p['rk_mu']

    def lerp_shift(u, m):
        return u + m * (_token_shift(u, on_grid) - u)

    def heads(a):
        return a.astype(f32).reshape(b_, l, nh, n)

    r = lerp_shift(_proj(h, p['w_in'], 'r_r'), mu[0:RK_W])
    k = lerp_shift(_proj(h, p['w_in'], 'r_k'), mu[RK_W:2 * RK_W])
    v = lerp_shift(_proj(h, p['w_in'], 'r_v'), mu[2 * RK_W:3 * RK_W])
    wd = _proj(h, p['w_in'], 'r_wd')
    ad = _proj(h, p['w_in'], 'r_ad')
    ow, oa = 3 * RK_W, 3 * RK_W + 2 * rd
    if p['rk_v0'] is None:
        v_first = v
    else:
        v = v + (v_first - v) * jax.nn.sigmoid(p['rk_v0'] + (h @ p['rk_v1']) @ p['rk_v2'])
    rh, kh, vh = heads(r), heads(k), heads(v)
    kk = heads(k * p['rk_kk'])
    kk = kk / jnp.maximum(jnp.sqrt(jnp.sum(kk * kk, axis=-1, keepdims=True)), 1e-12)
    k_a = p['rk_ka'].astype(f32).reshape(nh, n)

    def direction(d):
        xw = lerp_shift(wd[..., d * rd:(d + 1) * rd], mu[ow + d * rd:ow + (d + 1) * rd])
        xa = lerp_shift(ad[..., d * ra:(d + 1) * ra], mu[oa + d * ra:oa + (d + 1) * ra])
        log_w = -jax.nn.softplus(-(p['rk_w0'][d] + jnp.tanh(xw) @ p['rk_w2'][d])) - 0.5
        decay = jnp.exp(-jnp.exp(heads(log_w)))
        a = jax.nn.sigmoid(heads(p['rk_a0'][d] + xa @ p['rk_a2'][d]))
        kd = kh * (1.0 + (a - 1.0) * k_a)
        seq = (rh, decay, kd, vh, -kk, kk * a)
        if d == 1:
            seq = tuple(jnp.flip(a_, 1) for a_ in seq)
        y, s = _rwkv7_scan(*seq, init[d])
        if d == 1:
            y = jnp.flip(y, 1)
        return y, s, kd

    y_f, s_f, k_f = direction(0)
    y_b, s_b, k_b = direction(1)
    if not need_out:
        return None, (s_f, s_b), v_first
    r_k = p['rk_rk'].astype(f32).reshape(nh, n)
    bonus = jnp.sum(rh * (k_f + k_b) * r_k, axis=-1, keepdims=True) * vh
    y = y_f + y_b
    y = y - jnp.mean(y, axis=-1, keepdims=True)
    y = y * lax.rsqrt(jnp.mean(y * y, axis=-1, keepdims=True) + RK_GN_EPS)
    y = y.reshape(b_, l, RK_W) * p['rk_ln_w'] + p['rk_ln_b'] + bonus.reshape(b_, l, RK_W)
    z = _proj(h, p['w_in'], 'r_z').astype(f32)
    return y * jax.nn.silu(z), (s_f, s_b), v_first


def _hgrn2_branch(h, p, init, need_out):
    b_, l, _ = h.shape
    nh, dk, dv = HG_HEADS, HG_EXPAND, HG_DV
    f32 = jnp.float32
    q = jax.nn.silu(_proj(h, p['w_in'], 'h_q').astype(f32)).reshape(b_, l, nh, dk).transpose(0, 2, 1, 3)
    i_in = _proj(h, p['w_in'], 'h_i').astype(f32).reshape(b_, l, nh, dv).transpose(0, 2, 1, 3)
    fz = (_proj(h, p['w_in'], 'h_f').reshape(b_, l, 2, HG_W) + p['hg_f_b']).astype(f32)
    f = p['hg_lb'] + (1.0 - p['hg_lb']) * jax.nn.sigmoid(fz)
    f = f.reshape(b_, l, 2, nh, dk).transpose(2, 0, 3, 1, 4)
    log_f = jnp.log(f)
    k_in = 1.0 - f
    o_f, s_f = _gla_chunk_scan(q, k_in[0], i_in, log_f[0], init[0])
    o_b, s_b = _gla_chunk_scan(jnp.flip(q, 2), jnp.flip(k_in[1], 2), jnp.flip(i_in, 2),
                               jnp.flip(log_f[1], 2), init[1])
    if not need_out:
        return None, (s_f, s_b)
    o = o_f + jnp.flip(o_b, 2)
    o = o * lax.rsqrt(jnp.mean(o * o, axis=-1, keepdims=True) + NORM_EPS)
    o = o.transpose(0, 2, 1, 3).reshape(b_, l, HG_W) * p['hg_norm_w']
    z = _proj(h, p['w_in'], 'h_z').astype(f32)
    return o * jax.nn.silu(z), (s_f, s_b)


def _mixer(h, p, init, on_grid, v_first, need_out):
    u_m, st_m = _mlstm_branch(h, p, init[0], need_out)
    u_r, st_r, v_first = _rwkv7_branch(h, p, init[1], on_grid, v_first, need_out)
    u_h, st_h = _hgrn2_branch(h, p, init[2], need_out)
    states = (st_m, st_r, st_h)
    if not need_out:
        return None, states, v_first
    b_, l, _ = h.shape
    g = jax.nn.sigmoid((_proj(h, p['w_in'], 'gate').reshape(b_, l, N_BRANCH, D_MODEL) + p['gate_b']).astype(jnp.float32))
    merged = (g[:, :, 0] * (u_m @ p['w_pm']) + g[:, :, 1] * (u_r @ p['w_pr']) + g[:, :, 2] * (u_h @ p['w_ph']))
    return merged @ p['w_out'], states, v_first


def _zero_states(batch):
    f32 = jnp.float32
    ml = (jnp.zeros((batch, ML_HEADS, ML_HD, ML_HD), f32), jnp.zeros((batch, ML_HEADS, ML_HD), f32),
          jnp.zeros((batch, ML_HEADS), f32))
    rk = jnp.zeros((batch, RK_HEADS, RK_HD, RK_HD), f32)
    hg = jnp.zeros((batch, HG_HEADS, HG_EXPAND, HG_DV), f32)
    return ((ml, ml), (rk, rk), (hg, hg))


def setup_inputs(seed: int = 0) -> dict:
    key = jax.random.key(seed)
    ks = jax.random.split(key, 40)
    f32 = jnp.float32

    def nrm(i, shape, scale=1.0):
        return jax.random.normal(ks[i], shape, f32) * scale

    d, nv = DEPTH, DEPTH - 1
    return {
        'x': nrm(0, (BATCH, SEQ, D_MODEL)),
        'c': nrm(1, (BATCH, D_MODEL)),
        'ctx': nrm(2, (BATCH, CTX_LEN, D_MODEL)),
        'c_ctx': nrm(3, (D_MODEL,)),
        'norm_w': 1.0 + nrm(4, (d, D_MODEL), 0.02),
        'ada_w': nrm(5, (d, D_MODEL, 3 * D_MODEL), 0.3 * D_MODEL ** -0.5),
        'ada_b': nrm(6, (d, 3 * D_MODEL), 0.02),
        'w_in': nrm(7, (d, D_MODEL, IN_COLS), D_MODEL ** -0.5),
        'gate_b': nrm(8, (d, N_BRANCH, D_MODEL), 0.1),
        'ml_conv': nrm(9, (d, 2, ML_CONV, ML_W), ML_CONV ** -0.5),
        'ml_if_b': jnp.concatenate([nrm(10, (d, 2, 1, ML_HEADS), 0.1),
                                    jnp.linspace(3.0, 6.0, ML_HEADS, dtype=f32) + nrm(11, (d, 2, 1, ML_HEADS), 0.1)], axis=2),
        'ml_norm_w': 1.0 + nrm(12, (d, ML_W), 0.02),
        'rk_mu': jax.random.uniform(ks[13], (d, RK_MU_W), f32),
        'rk_w0': jnp.linspace(-6.0, 1.0, RK_W, dtype=f32) + nrm(14, (d, 2, RK_W), 0.1),
        'rk_w2': nrm(15, (d, 2, RK_DECAY_LORA, RK_W), 0.1 * RK_DECAY_LORA ** -0.5),
        'rk_a0': nrm(16, (d, 2, RK_W), 0.1),
        'rk_a2': nrm(17, (d, 2, RK_A_LORA, RK_W), 0.1 * RK_A_LORA ** -0.5),
        'rk_kk': 0.85 + nrm(18, (d, RK_W), 0.05),
        'rk_ka': 1.0 + nrm(19, (d, RK_W), 0.05),
        'rk_rk': nrm(20, (d, RK_W), 0.1),
        'rk_v0': 1.0 + nrm(21, (nv, RK_W), 0.1),
        'rk_v1': nrm(22, (nv, D_MODEL, RK_V_LORA), D_MODEL ** -0.5),
        'rk_v2': nrm(23, (nv, RK_V_LORA, RK_W), 0.1 * RK_V_LORA ** -0.5),
        'rk_ln_w': 1.0 + nrm(24, (d, RK_W), 0.02),
        'rk_ln_b': nrm(25, (d, RK_W), 0.02),
        'hg_f_b': 2.0 + nrm(26, (d, 2, HG_W), 0.1),
        'hg_lb': 1.0 + nrm(27, (2, d, HG_W), 0.1),
        'hg_norm_w': 1.0 + nrm(28, (d, HG_W), 0.02),
        'w_pm': nrm(29, (d, ML_W, D_MODEL), ML_W ** -0.5),
        'w_pr': nrm(30, (d, RK_W, D_MODEL), RK_W ** -0.5),
        'w_ph': nrm(31, (d, HG_W, D_MODEL), HG_W ** -0.5),
        'w_out': nrm(32, (d, D_MODEL, D_MODEL), D_MODEL ** -0.5),
        'final_norm_w': 1.0 + nrm(33, (D_MODEL,), 0.02),
    }


def reference(x, c, ctx, c_ctx, norm_w, ada_w, ada_b, w_in, gate_b, ml_conv, ml_if_b, ml_norm_w,
              rk_mu, rk_w0, rk_w2, rk_a0, rk_a2, rk_kk, rk_ka, rk_rk, rk_v0, rk_v1, rk_v2, rk_ln_w, rk_ln_b,
              hg_f_b, hg_lb, hg_norm_w, w_pm, w_pr, w_ph, w_out, final_norm_w):
    batch = x.shape[0]
    lb_p = jax.nn.softmax(hg_lb.astype(jnp.float32), axis=1)
    lower_bounds = jnp.cumsum(lb_p, axis=1) - lb_p[:, :1]
    xs, cs = x, ctx
    vf_x, vf_c = None, None
    for l in range(DEPTH):
        last = l == DEPTH - 1
        p = {'w_in': w_in[l], 'gate_b': gate_b[l], 'ml_conv': ml_conv[l], 'ml_if_b': ml_if_b[l],
             'ml_norm_w': ml_norm_w[l], 'rk_mu': rk_mu[l], 'rk_w0': rk_w0[l], 'rk_w2': rk_w2[l],
             'rk_a0': rk_a0[l], 'rk_a2': rk_a2[l], 'rk_kk': rk_kk[l], 'rk_ka': rk_ka[l], 'rk_rk': rk_rk[l],
             'rk_v0': rk_v0[l - 1] if l > 0 else None, 'rk_v1': rk_v1[l - 1] if l > 0 else None,
             'rk_v2': rk_v2[l - 1] if l > 0 else None, 'rk_ln_w': rk_ln_w[l], 'rk_ln_b': rk_ln_b[l],
             'hg_f_b': hg_f_b[l], 'hg_lb': lower_bounds[:, l], 'hg_norm_w': hg_norm_w[l],
             'w_pm': w_pm[l], 'w_pr': w_pr[l], 'w_ph': w_ph[l], 'w_out': w_out[l]}
        mod_x = jax.nn.silu(c) @ ada_w[l] + ada_b[l]
        mod_c = jax.nn.silu(c_ctx) @ ada_w[l] + ada_b[l]
        shift_x, scale_x, gate_x = jnp.split(mod_x[:, None, :], 3, axis=-1)
        shift_c, scale_c, gate_c = jnp.split(mod_c, 3, axis=-1)
        hc = _rms_norm(cs, norm_w[l]) * (1.0 + scale_c) + shift_c
        out_c, st_c, vf_c = _mixer(hc, p, _zero_states(batch), False, vf_c, not last)
        hx = _rms_norm(xs, norm_w[l]) * (1.0 + scale_x) + shift_x
        out_x, _, vf_x = _mixer(hx, p, st_c, True, vf_x, True)
        xs = (xs + gate_x * out_x).astype(x.dtype)
        if not last:
            cs = (cs + gate_c * out_c).astype(ctx.dtype)
    return _rms_norm(xs, final_norm_w)
```

```python
import functools

import jax
import jax.numpy as jnp
from jax import lax
from jax.experimental import pallas as pl
from jax.experimental.pallas import tpu as pltpu

F32 = jnp.float32
MXU_DT = jnp.bfloat16

NORM_EPS = 1e-6
GRID_W = 64
ML_HEADS = 4
ML_CHUNK = 64
RK_HD = 64
RK_CHUNK = 64
RK_GROUP = 256
RK_GN_EPS = 64e-5
HG_HEADS = 8
HG_SUB = 16
HG_TILE = 128
VMEM_LIMIT = 48 * 1024 * 1024


def _nt(a, b):
    return lax.dot_general(a, b, (((1,), (1,)), ((), ())), preferred_element_type=F32)


def _tn(a, b):
    return lax.dot_general(a, b, (((0,), (0,)), ((), ())), preferred_element_type=F32)


def _nn(a, b):
    return jnp.dot(a, b, preferred_element_type=F32)


def _mx(a):
    return a.astype(MXU_DT)


def _cumsum_mm(tri, x):
    hi = x.astype(MXU_DT)
    r1 = x - hi.astype(F32)
    mid = r1.astype(MXU_DT)
    lo = (r1 - mid.astype(F32)).astype(MXU_DT)
    return _nn(tri, hi) + _nn(tri, mid) + _nn(tri, lo)


def _mm_kernel(x_ref, w_ref, o_ref):
    o_ref[...] = _nn(_mx(x_ref[...]), w_ref[...])


def _mm(x, w):
    m, k = x.shape
    n = w.shape[1]
    mp = -(-m // 8) * 8
    npad = -(-n // 128) * 128
    if mp != m:
        x = jnp.pad(x, ((0, mp - m), (0, 0)))
    wb = w.astype(MXU_DT)
    if npad != n:
        wb = jnp.pad(wb, ((0, 0), (0, npad - n)))
    tm = next(t for t in (1024, 512, 256, 128, 64, 32, 16, 8) if mp % t == 0)
    tn = next(t for t in (1024, 512, 256, 128) if npad % t == 0)
    out = pl.pallas_call(
        _mm_kernel,
        grid=(mp // tm, npad // tn),
        in_specs=[pl.BlockSpec((tm, k), lambda i, j: (i, 0)),
                  pl.BlockSpec((k, tn), lambda i, j: (0, j))],
        out_specs=pl.BlockSpec((tm, tn), lambda i, j: (i, j)),
        out_shape=jax.ShapeDtypeStruct((mp, npad), F32),
        compiler_params=pltpu.CompilerParams(dimension_semantics=("parallel", "parallel"),
                                             vmem_limit_bytes=VMEM_LIMIT),
        name="mm",
    )(x, wb)
    if mp != m or npad != n:
        out = out[:m, :n]
    return out


def _mlstm_kernel(q_ref, k_ref, v_ref, gcol_ref, grow_ref, c0_ref, n0_ref, m0_ref,
                  h_ref, c_ref, n_ref, m_ref, *, rev, heads, hd):
    @pl.when(pl.program_id(1) == 0)
    def _():
        c_ref[...] = c0_ref[...]
        n_ref[...] = n0_ref[...]
        m_ref[...] = m0_ref[...]

    t = q_ref.shape[1]
    row = lax.broadcasted_iota(jnp.int32, (t, t), 0)
    col = lax.broadcasted_iota(jnp.int32, (t, t), 1)
    mask = (col >= row) if rev else (col <= row)
    last = 0 if rev else t - 1
    for h in range(heads):
        sl = slice(h * hd, (h + 1) * hd)
        q32 = q_ref[0, :, sl]
        qc = _mx(q32)
        kc = k_ref[0, :, sl]
        vc = _mx(v_ref[0, :, sl])
        bc_col = gcol_ref[0, :, h:h + 1]
        i_col = gcol_ref[0, :, heads + h:heads + h + 1]
        bc_row = grow_ref[0, 0, h:h + 1, :]
        i_row = grow_ref[0, 0, heads + h:heads + h + 1, :]
        m_prev = m_ref[0, h, 0:1, 0:1]
        c_mem = c_ref[0, h]
        n_mem = n_ref[0, h]

        dmat = jnp.where(mask, bc_col - bc_row + i_row, -jnp.inf)
        inter = bc_col + m_prev
        m_t = jnp.maximum(jnp.max(dmat, axis=-1, keepdims=True), inter)
        s = _nt(qc, _mx(kc)) * jnp.exp(dmat - m_t)
        w_inter = jnp.exp(inter - m_t)
        num = _nn(_mx(s), vc) + w_inter * _nn(qc, _mx(c_mem))
        qn = jnp.sum(q32 * n_mem, axis=-1, keepdims=True)
        den = jnp.sum(s, axis=-1, keepdims=True) + w_inter * qn
        h_ref[0, :, sl] = num / jnp.maximum(jnp.abs(den), jnp.exp(-m_t))

        total = bc_col[last:last + 1, :]
        g_col = total - bc_col + i_col
        m_new = jnp.maximum(total + m_prev, jnp.max(g_col, axis=0, keepdims=True))
        wk = jnp.exp(g_col - m_new)
        dec = jnp.exp(total + m_prev - m_new)
        kw = kc * wk
        c_ref[0, h] = dec * c_mem + _tn(_mx(kw), vc)
        n_ref[0, h] = dec * n_mem + jnp.sum(kw, axis=0, keepdims=True)
        m_ref[0, h] = jnp.broadcast_to(m_new, m_ref.shape[2:])


def _mlstm_scan(q, k, v, log_i, log_f, state, rev):
    b, l, w = q.shape
    heads = ML_HEADS
    hd = w // heads
    t = ML_CHUNK
    nc = l // t
    lf = log_f.reshape(b, nc, t, heads)
    if rev:
        bcum = jnp.flip(jnp.cumsum(jnp.flip(lf, 2), axis=2), 2)
    else:
        bcum = jnp.cumsum(lf, axis=2)
    gcol = jnp.concatenate([bcum.reshape(b, l, heads), log_i], axis=-1)
    grow = gcol.reshape(b, nc, t, 2 * heads).transpose(0, 1, 3, 2)
    cidx = (lambda bi, c: (bi, nc - 1 - c, 0)) if rev else (lambda bi, c: (bi, c, 0))
    ridx = (lambda bi, c: (bi, nc - 1 - c, 0, 0)) if rev else (lambda bi, c: (bi, c, 0, 0))
    sidx = lambda bi, c: (bi, 0, 0, 0)
    c0, n0, m0 = state
    seq = pl.BlockSpec((1, t, w), cidx)
    st_specs = [pl.BlockSpec((1, heads, hd, hd), sidx), pl.BlockSpec((1, heads, 1, hd), sidx),
                pl.BlockSpec((1, heads, 8, 128), sidx)]
    out = pl.pallas_call(
        functools.partial(_mlstm_kernel, rev=rev, heads=heads, hd=hd),
        grid=(b, nc),
        in_specs=[seq, seq, seq, pl.BlockSpec((1, t, 2 * heads), cidx),
                  pl.BlockSpec((1, 1, 2 * heads, t), ridx)] + st_specs,
        out_specs=[seq] + st_specs,
        out_shape=[jax.ShapeDtypeStruct((b, l, w), F32), jax.ShapeDtypeStruct(c0.shape, F32),
                   jax.ShapeDtypeStruct(n0.shape, F32), jax.ShapeDtypeStruct(m0.shape, F32)],
        compiler_params=pltpu.CompilerParams(dimension_semantics=("parallel", "arbitrary"),
                                             vmem_limit_bytes=VMEM_LIMIT),
        name="mlstm_rev" if rev else "mlstm_fwd",
    )(q, k, v, gcol, grow, c0, n0, m0)
    return out[0], (out[1], out[2], out[3])


def _rwkv_kernel(r_ref, lw_ref, kd_ref, v_ref, kk_ref, a_ref, s0_ref, y_ref, s_ref, *, rev):
    @pl.when(pl.program_id(1) == 0)
    def _():
        s_ref[...] = s0_ref[...]

    c = r_ref.shape[1]
    g_w = RK_GROUP
    nh = g_w // RK_HD
    assert c == RK_HD and c & (c - 1) == 0
    sh = c.bit_length() - 1
    rj = lax.broadcasted_iota(jnp.int32, (g_w, g_w), 0)
    cj = lax.broadcasted_iota(jnp.int32, (g_w, g_w), 1)
    blk = (rj >> sh) == (cj >> sh)
    tt, ss = rj & (c - 1), cj & (c - 1)
    incl = blk & ((ss >= tt) if rev else (ss <= tt))
    strict = blk & ((ss > tt) if rev else (ss < tt))
    eye = (rj == cj).astype(F32)
    tri = incl[:c, :c].astype(F32).astype(MXU_DT)
    last = 0 if rev else c - 1
    zero = jnp.zeros((), F32)

    def bd(x):
        return _mx(jnp.where(blk, jnp.concatenate([x] * nh, axis=0), zero))

    def tile(x):
        return _mx(jnp.concatenate([x] * nh, axis=0))

    def rsum(z):
        out = z[0:c]
        for i in range(1, nh):
            out = out + z[i * c:(i + 1) * c]
        return out

    for g in range(r_ref.shape[2] // g_w):
        sl = slice(g * g_w, (g + 1) * g_w)
        lw = lw_ref[0, :, sl]
        kk = kk_ref[0, :, sl]
        kd = kd_ref[0, :, sl]
        v = v_ref[0, :, sl]
        ka = kk * a_ref[0, :, sl]
        gi = _cumsum_mm(tri, lw)
        tot = gi[last:last + 1, :]
        e_neg = jnp.exp(-gi)
        e_tail = jnp.exp(tot - gi)
        r_t = r_ref[0, :, sl] * jnp.exp(gi)
        bk_hat = _mx(jnp.concatenate([ka * e_tail, kd * e_tail], axis=0))

        a_bd, r_bd, v_bd = bd(-kk * jnp.exp(gi - lw)), bd(r_t), bd(v)
        b4, k4 = tile(ka * e_neg), tile(kd * e_neg)
        l_ab = jnp.where(strict, _nt(a_bd, b4), zero)
        a_ak = jnp.where(strict, _nt(a_bd, k4), zero)
        a_rb = jnp.where(incl, _nt(r_bd, b4), zero)
        a_rk = jnp.where(incl, _nt(r_bd, k4), zero)

        x = _mx(l_ab)
        p = eye + l_ab
        steps = max(1, (c - 1).bit_length() - 1)
        for _ in range(steps):
            x2 = _nn(x, x)
            x = _mx(x2)
            p = p + _nn(_mx(p), x)
        t_inv = _mx(p)

        z_ak = _nn(_mx(a_ak), v_bd)
        wu = _nn(t_inv, jnp.concatenate([a_bd, _mx(z_ak)], axis=1))
        w_cat = rsum(wu[:, :g_w])
        u0 = rsum(wu[:, g_w:])
        y_k = _nn(_mx(rsum(a_rk)), v_bd)

        s_mem = s_ref[0, g]
        s_b = _mx(s_mem)
        u = _nt(_mx(w_cat), s_b) + u0
        y_ref[0, :, sl] = _nt(_mx(r_t), s_b) + _nn(_mx(rsum(a_rb)), bd(u)) + y_k
        upd = _tn(_mx(jnp.concatenate([u, v], axis=0)), bk_hat)
        s_ref[0, g] = s_mem * jnp.exp(tot) + jnp.where(blk, upd, zero)


def _rwkv_scan(r, lw, kd, v, kk, a, state, rev):
    b, l, w = r.shape
    c = RK_CHUNK
    nc = l // c
    cidx = (lambda bi, ch: (bi, nc - 1 - ch, 0)) if rev else (lambda bi, ch: (bi, ch, 0))
    seq = pl.BlockSpec((1, c, w), cidx)
    st = pl.BlockSpec((1,) + state.shape[1:], lambda bi, ch: (bi, 0, 0, 0))
    y, s = pl.pallas_call(
        functools.partial(_rwkv_kernel, rev=rev),
        grid=(b, nc),
        in_specs=[seq] * 6 + [st],
        out_specs=[seq, st],
        out_shape=[jax.ShapeDtypeStruct((b, l, w), F32), jax.ShapeDtypeStruct(state.shape, F32)],
        compiler_params=pltpu.CompilerParams(dimension_semantics=("parallel", "arbitrary"),
                                             vmem_limit_bytes=VMEM_LIMIT),
        name="rwkv_rev" if rev else "rwkv_fwd",
    )(r, lw, kd, v, kk, a, state)
    return y, s


def _hgrn_kernel(q_ref, k_ref, v_ref, lf_ref, s0_ref, o_ref, s_ref, *, rev, heads, hd):
    @pl.when(pl.program_id(1) == 0)
    def _():
        s_ref[...] = s0_ref[...]

    tt = q_ref.shape[1]
    sub = HG_SUB
    ri = lax.broadcasted_iota(jnp.int32, (tt, tt), 0)
    ci = lax.broadcasted_iota(jnp.int32, (tt, tt), 1)
    sh = sub.bit_length() - 1
    tri = (((ri >> sh) == (ci >> sh)) & ((ci >= ri) if rev else (ci <= ri))).astype(F32).astype(MXU_DT)
    r16 = lax.broadcasted_iota(jnp.int32, (sub, sub), 0)
    c16 = lax.broadcasted_iota(jnp.int32, (sub, sub), 1)
    mask = (c16 >= r16) if rev else (c16 <= r16)
    g_all = _cumsum_mm(tri, lf_ref[0])
    mid = sub // 2 if rev else sub // 2 - 1
    last = 0 if rev else sub - 1
    nsub = tt // sub
    for step in range(nsub):
        j = nsub - 1 - step if rev else step
        rows = slice(j * sub, (j + 1) * sub)
        gj = g_all[rows]
        g_mid = gj[mid:mid + 1]
        g_last = gj[last:last + 1]
        q = q_ref[0, rows, :]
        k = k_ref[0, rows, :]
        q1 = _mx(q * jnp.exp(gj - g_mid))
        k1 = _mx(k * jnp.exp(g_mid - gj))
        qg = _mx(q * jnp.exp(gj))
        kl = _mx(k * jnp.exp(g_last - gj))
        dec = jnp.exp(g_last)
        v = _mx(v_ref[0, rows, :])
        for h in range(heads):
            sl = slice(h * hd, (h + 1) * hd)
            s_mem = s_ref[0, h]
            att = jnp.where(mask, _nt(q1[:, sl], k1[:, sl]), jnp.zeros((), F32))
            o_ref[0, rows, sl] = _nn(_mx(att), v[:, sl]) + _nt(qg[:, sl], _mx(s_mem))
            s_ref[0, h] = s_mem * dec[:, sl] + _tn(v[:, sl], kl[:, sl])


def _hgrn_scan(q, k, v, log_f, state, rev):
    b, l, w = q.shape
    heads = HG_HEADS
    hd = w // heads
    tt = min(HG_TILE, l)
    nc = l // tt
    cidx = (lambda bi, ch: (bi, nc - 1 - ch, 0)) if rev else (lambda bi, ch: (bi, ch, 0))
    seq = pl.BlockSpec((1, tt, w), cidx)
    st = pl.BlockSpec((1,) + state.shape[1:], lambda bi, ch: (bi, 0, 0, 0))
    o, s = pl.pallas_call(
        functools.partial(_hgrn_kernel, rev=rev, heads=heads, hd=hd),
        grid=(b, nc),
        in_specs=[seq] * 4 + [st],
        out_specs=[seq, st],
        out_shape=[jax.ShapeDtypeStruct((b, l, w), F32), jax.ShapeDtypeStruct(state.shape, F32)],
        compiler_params=pltpu.CompilerParams(dimension_semantics=("parallel", "arbitrary"),
                                             vmem_limit_bytes=VMEM_LIMIT),
        name="hgrn_rev" if rev else "hgrn_fwd",
    )(q, k, v, log_f, state)
    return o, s


def _rms_norm(x, w):
    return x * lax.rsqrt(jnp.mean(x * x, axis=-1, keepdims=True) + NORM_EPS) * w


def _dwconv_centered(u, w):
    g = jnp.pad(u, ((0, 0), (1, 1), (0, 0)))
    return g[:, :-2] * w[0] + g[:, 1:-1] * w[1] + g[:, 2:] * w[2]


def _q_shift_grid(u):
    b, l, ch = u.shape
    rows = l // GRID_W
    g = jnp.pad(u.reshape(b, rows, GRID_W, ch), ((0, 0), (1, 1), (1, 1), (0, 0)))
    q4 = ch // 4
    parts = (g[:, 1:-1, :-2, :q4], g[:, 1:-1, 2:, q4:2 * q4],
             g[:, :-2, 1:-1, 2 * q4:3 * q4], g[:, 2:, 1:-1, 3 * q4:])
    return jnp.concatenate(parts, axis=-1).reshape(b, l, ch)


def _bi_shift_seq(u):
    half = u.shape[-1] // 2
    g = jnp.pad(u, ((0, 0), (1, 1), (0, 0)))
    return jnp.concatenate((g[:, :-2, :half], g[:, 2:, half:]), axis=-1)


def _in_layout(d):
    rd = d // 16
    return (('m_q', d), ('m_k', d), ('m_v', d), ('m_o', d), ('m_z', d), ('m_if', 4 * ML_HEADS),
            ('r_r', d), ('r_k', d), ('r_v', d), ('r_z', d), ('r_wd', 2 * rd), ('r_ad', 2 * rd),
            ('h_q', d), ('h_f', 2 * d), ('h_i', d), ('h_z', d), ('gate', 3 * d))


def _split_w_in(w_in):
    out, off = {}, 0
    for name, width in _in_layout(w_in.shape[0]):
        out[name] = w_in[:, off:off + width].astype(MXU_DT)
        off += width
    return out


def _zero_states(b, d):
    ml_hd = d // ML_HEADS
    ml = (jnp.zeros((b, ML_HEADS, ml_hd, ml_hd), F32), jnp.zeros((b, ML_HEADS, 1, ml_hd), F32),
          jnp.zeros((b, ML_HEADS, 8, 128), F32))
    rk = jnp.zeros((b, d // RK_GROUP, RK_GROUP, RK_GROUP), F32)
    hg_hd = d // HG_HEADS
    hg = jnp.zeros((b, HG_HEADS, hg_hd, hg_hd), F32)
    return ((ml, ml), (rk, rk), (hg, hg))


def _mlstm_branch(proj, p, init, need_out, b, l, d):
    hd = d // ML_HEADS
    q = jax.nn.silu(_dwconv_centered(proj('m_q'), p['ml_conv'][0])) * (hd ** -0.5)
    k = jax.nn.silu(_dwconv_centered(proj('m_k'), p['ml_conv'][1]))
    v = proj('m_v')
    gl = proj('m_if').reshape(b, l, 2, 2, ML_HEADS) + p['ml_if_b']
    hs, states = [], []
    for dr in (0, 1):
        log_i = gl[:, :, dr, 0]
        log_f = jax.nn.log_sigmoid(gl[:, :, dr, 1])
        h_d, st = _mlstm_scan(q, k, v, log_i, log_f, init[dr], rev=bool(dr))
        hs.append(h_d)
        states.append(st)
    if not need_out:
        return None, tuple(states)
    y = (hs[0] + hs[1]).reshape(b, l, ML_HEADS, hd)
    y = y - jnp.mean(y, axis=-1, keepdims=True)
    y = y * lax.rsqrt(jnp.mean(y * y, axis=-1, keepdims=True) + NORM_EPS)
    y = y.reshape(b, l, d) * p['ml_norm_w']
    o = jax.nn.sigmoid(proj('m_o'))
    z = proj('m_z')
    return o * y * jax.nn.silu(z), tuple(states)


def _rwkv7_branch(proj, h2, p, init, on_grid, v_first, need_out, b, l, d):
    nh, n = d // RK_HD, RK_HD
    rd = d // 16
    mu = p['rk_mu']
    shift = _q_shift_grid if on_grid else _bi_shift_seq

    def lerp_shift(u, m):
        return u + m * (shift(u) - u)

    r = lerp_shift(proj('r_r'), mu[0:d])
    k = lerp_shift(proj('r_k'), mu[d:2 * d])
    v = lerp_shift(proj('r_v'), mu[2 * d:3 * d])
    wd = proj('r_wd')
    ad = proj('r_ad')
    ow, oa = 3 * d, 3 * d + 2 * rd
    if p['rk_v0'] is None:
        v_first = v
    else:
        lora = _mm(_mm(h2, p['rk_v1']), p['rk_v2']).reshape(b, l, d)
        v = v + (v_first - v) * jax.nn.sigmoid(p['rk_v0'] + lora)
    kk = (k * p['rk_kk']).reshape(b, l, nh, n)
    kk = kk / jnp.maximum(jnp.sqrt(jnp.sum(kk * kk, axis=-1, keepdims=True)), 1e-12)
    kk = kk.reshape(b, l, d)
    ys, states, kds = [], [], []
    for dr in (0, 1):
        xw = lerp_shift(wd[..., dr * rd:(dr + 1) * rd], mu[ow + dr * rd:ow + (dr + 1) * rd])
        xa = lerp_shift(ad[..., dr * rd:(dr + 1) * rd], mu[oa + dr * rd:oa + (dr + 1) * rd])
        log_w = -jax.nn.softplus(-(p['rk_w0'][dr] + _mm(jnp.tanh(xw).reshape(b * l, rd), p['rk_w2'][dr]).reshape(b, l, d))) - 0.5
        lw = -jnp.exp(log_w)
        a = jax.nn.sigmoid(p['rk_a0'][dr] + _mm(xa.reshape(b * l, rd), p['rk_a2'][dr]).reshape(b, l, d))
        kd = k * (1.0 + (a - 1.0) * p['rk_ka'])
        y, s = _rwkv_scan(r, lw, kd, v, kk, a, init[dr], rev=bool(dr))
        ys.append(y)
        states.append(s)
        kds.append(kd)
    if not need_out:
        return None, tuple(states), v_first
    bonus = jnp.sum((r * (kds[0] + kds[1]) * p['rk_rk']).reshape(b, l, nh, n), axis=-1, keepdims=True) * v.reshape(b, l, nh, n)
    y = (ys[0] + ys[1]).reshape(b, l, nh, n)
    y = y - jnp.mean(y, axis=-1, keepdims=True)
    y = y * lax.rsqrt(jnp.mean(y * y, axis=-1, keepdims=True) + RK_GN_EPS)
    y = y.reshape(b, l, d) * p['rk_ln_w'] + p['rk_ln_b'] + bonus.reshape(b, l, d)
    z = proj('r_z')
    return y * jax.nn.silu(z), tuple(states), v_first


def _hgrn2_branch(proj, p, init, need_out, b, l, d):
    hd = d // HG_HEADS
    q = jax.nn.silu(proj('h_q'))
    i_in = proj('h_i')
    fz = proj('h_f').reshape(b, l, 2, d) + p['hg_f_b']
    f = p['hg_lb'] + (1.0 - p['hg_lb']) * jax.nn.sigmoid(fz)
    log_f = jnp.log(f)
    k_in = 1.0 - f
    os_, states = [], []
    for dr in (0, 1):
        o_d, s = _hgrn_scan(q, k_in[:, :, dr], i_in, log_f[:, :, dr], init[dr], rev=bool(dr))
        os_.append(o_d)
        states.append(s)
    if not need_out:
        return None, tuple(states)
    o = (os_[0] + os_[1]).reshape(b, l, HG_HEADS, hd)
    o = o * lax.rsqrt(jnp.mean(o * o, axis=-1, keepdims=True) + NORM_EPS)
    o = o.reshape(b, l, d) * p['hg_norm_w']
    z = proj('h_z')
    return o * jax.nn.silu(z), tuple(states)


def _mixer(h, p, init, on_grid, v_first, need_out):
    b, l, d = h.shape
    h2 = _mx(h.reshape(b * l, d))
    cache = {}

    def proj(name):
        if name not in cache:
            w = p['w_in'][name]
            cache[name] = _mm(h2, w).reshape(b, l, w.shape[1])
        return cache[name]

    u_m, st_m = _mlstm_branch(proj, p, init[0], need_out, b, l, d)
    u_r, st_r, v_first = _rwkv7_branch(proj, h2, p, init[1], on_grid, v_first, need_out, b, l, d)
    u_h, st_h = _hgrn2_branch(proj, p, init[2], need_out, b, l, d)
    states = (st_m, st_r, st_h)
    if not need_out:
        return None, states, v_first
    g = jax.nn.sigmoid(proj('gate').reshape(b, l, 3, d) + p['gate_b'])
    pm = _mm(u_m.reshape(b * l, d), p['w_pm']).reshape(b, l, d)
    pr = _mm(u_r.reshape(b * l, d), p['w_pr']).reshape(b, l, d)
    ph = _mm(u_h.reshape(b * l, d), p['w_ph']).reshape(b, l, d)
    merged = g[:, :, 0] * pm + g[:, :, 1] * pr + g[:, :, 2] * ph
    return _mm(merged.reshape(b * l, d), p['w_out']).reshape(b, l, d), states, v_first


def kernel(x, c, ctx, c_ctx, norm_w, ada_w, ada_b, w_in, gate_b, ml_conv, ml_if_b, ml_norm_w, rk_mu, rk_w0, rk_w2, rk_a0, rk_a2, rk_kk, rk_ka, rk_rk, rk_v0, rk_v1, rk_v2, rk_ln_w, rk_ln_b, hg_f_b, hg_lb, hg_norm_w, w_pm, w_pr, w_ph, w_out, final_norm_w):
    batch, _, d = x.shape
    depth = w_in.shape[0]
    lb_p = jax.nn.softmax(hg_lb.astype(F32), axis=1)
    lower_bounds = jnp.cumsum(lb_p, axis=1) - lb_p[:, :1]
    xs, cs = x, ctx
    vf_x, vf_c = None, None
    cond = jnp.concatenate([jax.nn.silu(c), jax.nn.silu(c_ctx)[None]], axis=0)
    for l in range(depth):
        last = l == depth - 1
        p = {'w_in': _split_w_in(w_in[l]), 'gate_b': gate_b[l], 'ml_conv': ml_conv[l], 'ml_if_b': ml_if_b[l],
             'ml_norm_w': ml_norm_w[l], 'rk_mu': rk_mu[l], 'rk_w0': rk_w0[l], 'rk_w2': rk_w2[l],
             'rk_a0': rk_a0[l], 'rk_a2': rk_a2[l], 'rk_kk': rk_kk[l], 'rk_ka': rk_ka[l], 'rk_rk': rk_rk[l],
             'rk_v0': rk_v0[l - 1] if l > 0 else None, 'rk_v1': rk_v1[l - 1] if l > 0 else None,
             'rk_v2': rk_v2[l - 1] if l > 0 else None, 'rk_ln_w': rk_ln_w[l], 'rk_ln_b': rk_ln_b[l],
             'hg_f_b': hg_f_b[l], 'hg_lb': lower_bounds[:, l], 'hg_norm_w': hg_norm_w[l],
             'w_pm': w_pm[l], 'w_pr': w_pr[l], 'w_ph': w_ph[l], 'w_out': w_out[l]}
        mod = _mm(cond, ada_w[l]) + ada_b[l]
        shift_x, scale_x, gate_x = jnp.split(mod[:batch, None, :], 3, axis=-1)
        shift_c, scale_c, gate_c = jnp.split(mod[batch], 3, axis=-1)
        hc = _rms_norm(cs, norm_w[l]) * (1.0 + scale_c) + shift_c
        out_c, st_c, vf_c = _mixer(hc, p, _zero_states(batch, d), False, vf_c, not last)
        hx = _rms_norm(xs, norm_w[l]) * (1.0 + scale_x) + shift_x
        out_x, _, vf_x = _mixer(hx, p, st_c, True, vf_x, True)
        xs = xs + gate_x * out_x
        if not last:
            cs = cs + gate_c * out_c
    return _rms_norm(xs, final_norm_w)
```

```python
import functools

import jax
import jax.numpy as jnp
from jax import lax
from jax.experimental import pallas as pl
from jax.experimental.pallas import tpu as pltpu

F32 = jnp.float32
MXU_DT = jnp.bfloat16

NORM_EPS = 1e-6
GRID_W = 64
ML_HEADS = 4
ML_CHUNK = 64
RK_HD = 64
RK_CHUNK = 64
RK_GROUP = 256
RK_GN_EPS = 64e-5
HG_HEADS = 8
HG_SUB = 16
HG_TILE = 128
EP_TILE = 128
SEG_W = 256
VMEM_LIMIT = 56 * 1024 * 1024


def _nt(a, b):
    return lax.dot_general(a, b, (((1,), (1,)), ((), ())), preferred_element_type=F32)


def _tn(a, b):
    return lax.dot_general(a, b, (((0,), (0,)), ((), ())), preferred_element_type=F32)


def _nn(a, b):
    return jnp.dot(a, b, preferred_element_type=F32)


def _mx(a):
    return a.astype(MXU_DT)


def _split2(x):
    hi = x.astype(MXU_DT)
    return hi, (x - hi.astype(F32)).astype(MXU_DT)


def _cumsum_mm(tri, x):
    hi = x.astype(MXU_DT)
    r1 = x - hi.astype(F32)
    mid = r1.astype(MXU_DT)
    lo = (r1 - mid.astype(F32)).astype(MXU_DT)
    return _nn(tri, hi) + _nn(tri, mid) + _nn(tri, lo)


def _seg_ones(seg):
    sh = seg.bit_length() - 1
    ri = lax.broadcasted_iota(jnp.int32, (SEG_W, SEG_W), 0)
    ci = lax.broadcasted_iota(jnp.int32, (SEG_W, SEG_W), 1)
    return ((ri >> sh) == (ci >> sh)).astype(F32).astype(MXU_DT)


def _segsum(x, ones):
    outs = []
    for g in range(x.shape[1] // SEG_W):
        hi, lo = _split2(x[:, g * SEG_W:(g + 1) * SEG_W])
        outs.append(_nn(hi, ones) + _nn(lo, ones))
    return outs[0] if len(outs) == 1 else jnp.concatenate(outs, axis=1)


def _sigmoid(x):
    return jax.nn.sigmoid(x)


def _silu(x):
    return x * jax.nn.sigmoid(x)


def _params(sem):
    return pltpu.CompilerParams(dimension_semantics=sem, vmem_limit_bytes=VMEM_LIMIT)


def _mm_kernel(x_ref, w_ref, o_ref):
    o_ref[...] = _nn(_mx(x_ref[...]), w_ref[...])


def _mm(x, w):
    m, k = x.shape
    n = w.shape[1]
    mp = -(-m // 8) * 8
    npad = -(-n // 128) * 128
    if mp != m:
        x = jnp.pad(x, ((0, mp - m), (0, 0)))
    wb = w.astype(MXU_DT)
    if npad != n:
        wb = jnp.pad(wb, ((0, 0), (0, npad - n)))
    tm = next(t for t in (1024, 512, 256, 128, 64, 32, 16, 8) if mp % t == 0)
    tn = next(t for t in (1024, 512, 256, 128) if npad % t == 0)
    out = pl.pallas_call(
        _mm_kernel,
        grid=(mp // tm, npad // tn),
        in_specs=[pl.BlockSpec((tm, k), lambda i, j: (i, 0)),
                  pl.BlockSpec((k, tn), lambda i, j: (0, j))],
        out_specs=pl.BlockSpec((tm, tn), lambda i, j: (i, j)),
        out_shape=jax.ShapeDtypeStruct((mp, npad), F32),
        compiler_params=_params(("parallel", "parallel")),
        name="mm",
    )(x, wb)
    if mp != m or npad != n:
        out = out[:m, :n]
    return out


def _mlstm_kernel(*refs, rev, heads, hd, nc, has_acc):
    (q_ref, qp_ref, qn_ref, k_ref, kp_ref, kn_ref, v_ref, cw_ref, gcol_ref, grow_ref,
     c0_ref, n0_ref, m0_ref) = refs[:13]
    acc_ref = refs[13] if has_acc else None
    h_ref, c_ref, n_ref, m_ref = refs[13 + has_acc:]
    step = pl.program_id(1)

    @pl.when(step == 0)
    def _():
        c_ref[...] = c0_ref[...]
        n_ref[...] = n0_ref[...]
        m_ref[...] = m0_ref[...]

    t = q_ref.shape[1]
    ci = nc - 1 - step if rev else step
    has_prev = jnp.where(ci > 0, 1.0, 0.0)
    has_next = jnp.where(ci < nc - 1, 1.0, 0.0)
    trow = lax.broadcasted_iota(jnp.int32, (t, 1), 0)

    def conv_silu(u, prev_blk, next_blk, w3):
        up = jnp.where(trow == 0, prev_blk[7:8, :] * has_prev, pltpu.roll(u, 1, 0))
        dn = jnp.where(trow == t - 1, next_blk[0:1, :] * has_next, pltpu.roll(u, t - 1, 0))
        return _silu(up * w3[0:1, :] + u * w3[1:2, :] + dn * w3[2:3, :])

    q_all = conv_silu(q_ref[0], qp_ref[0], qn_ref[0], cw_ref[0:3, :]) * (hd ** -0.5)
    k_all = conv_silu(k_ref[0], kp_ref[0], kn_ref[0], cw_ref[3:6, :])

    row = lax.broadcasted_iota(jnp.int32, (t, t), 0)
    col = lax.broadcasted_iota(jnp.int32, (t, t), 1)
    mask = (col >= row) if rev else (col <= row)
    last = 0 if rev else t - 1
    for h in range(heads):
        sl = slice(h * hd, (h + 1) * hd)
        q32 = q_all[:, sl]
        qc = _mx(q32)
        kc = k_all[:, sl]
        vc = _mx(v_ref[0, :, sl])
        bc_col = gcol_ref[0, :, h:h + 1]
        i_col = gcol_ref[0, :, heads + h:heads + h + 1]
        bc_row = grow_ref[0, 0, h:h + 1, :]
        i_row = grow_ref[0, 0, heads + h:heads + h + 1, :]
        m_prev = m_ref[0, h, 0:1, 0:1]
        c_mem = c_ref[0, h]
        n_mem = n_ref[0, h]

        dmat = jnp.where(mask, bc_col - bc_row + i_row, -jnp.inf)
        inter = bc_col + m_prev
        m_t = jnp.maximum(jnp.max(dmat, axis=-1, keepdims=True), inter)
        s = _nt(qc, _mx(kc)) * jnp.exp(dmat - m_t)
        w_inter = jnp.exp(inter - m_t)
        num = _nn(_mx(s), vc) + w_inter * _nn(qc, _mx(c_mem))
        qn = jnp.sum(q32 * n_mem, axis=-1, keepdims=True)
        den = jnp.sum(s, axis=-1, keepdims=True) + w_inter * qn
        h_out = num / jnp.maximum(jnp.abs(den), jnp.exp(-m_t))
        if has_acc:
            h_out = h_out + acc_ref[0, :, sl]
        h_ref[0, :, sl] = h_out

        total = bc_col[last:last + 1, :]
        g_col = total - bc_col + i_col
        m_new = jnp.maximum(total + m_prev, jnp.max(g_col, axis=0, keepdims=True))
        wk = jnp.exp(g_col - m_new)
        dec = jnp.exp(total + m_prev - m_new)
        kw = kc * wk
        c_ref[0, h] = dec * c_mem + _tn(_mx(kw), vc)
        n_ref[0, h] = dec * n_mem + jnp.sum(kw, axis=0, keepdims=True)
        m_ref[0, h] = jnp.broadcast_to(m_new, m_ref.shape[2:])


def _mlstm_scan(pq, pk, v, conv_w, log_i, log_f, state, rev, acc=None):
    b, l, w = pq.shape
    heads = ML_HEADS
    hd = w // heads
    t = ML_CHUNK
    nc = l // t
    nb8 = l // 8
    lf = log_f.reshape(b, nc, t, heads)
    if rev:
        bcum = jnp.flip(jnp.cumsum(jnp.flip(lf, 2), axis=2), 2)
    else:
        bcum = jnp.cumsum(lf, axis=2)
    gcol = jnp.concatenate([bcum.reshape(b, l, heads), log_i], axis=-1)
    grow = gcol.reshape(b, nc, t, 2 * heads).transpose(0, 1, 3, 2)
    ch = (lambda c: nc - 1 - c) if rev else (lambda c: c)
    cidx = lambda bi, c: (bi, ch(c), 0)
    pidx = lambda bi, c: (bi, jnp.maximum(ch(c) * (t // 8) - 1, 0), 0)
    nidx = lambda bi, c: (bi, jnp.minimum((ch(c) + 1) * (t // 8), nb8 - 1), 0)
    ridx = lambda bi, c: (bi, ch(c), 0, 0)
    sidx = lambda bi, c: (bi, 0, 0, 0)
    c0, n0, m0 = state
    seq = pl.BlockSpec((1, t, w), cidx)
    prv = pl.BlockSpec((1, 8, w), pidx)
    nxt = pl.BlockSpec((1, 8, w), nidx)
    st_specs = [pl.BlockSpec((1, heads, hd, hd), sidx), pl.BlockSpec((1, heads, 1, hd), sidx),
                pl.BlockSpec((1, heads, 8, 128), sidx)]
    cw = conv_w.reshape(6, w)
    in_specs = [seq, prv, nxt, seq, prv, nxt, seq, pl.BlockSpec((6, w), lambda bi, c: (0, 0)),
                pl.BlockSpec((1, t, 2 * heads), cidx), pl.BlockSpec((1, 1, 2 * heads, t), ridx)] + st_specs
    args = [pq, pq, pq, pk, pk, pk, v, cw, gcol, grow, c0, n0, m0]
    if acc is not None:
        in_specs.append(seq)
        args.append(acc)
    out = pl.pallas_call(
        functools.partial(_mlstm_kernel, rev=rev, heads=heads, hd=hd, nc=nc, has_acc=acc is not None),
        grid=(b, nc),
        in_specs=in_specs,
        out_specs=[seq] + st_specs,
        out_shape=[jax.ShapeDtypeStruct((b, l, w), F32), jax.ShapeDtypeStruct(c0.shape, F32),
                   jax.ShapeDtypeStruct(n0.shape, F32), jax.ShapeDtypeStruct(m0.shape, F32)],
        compiler_params=_params(("parallel", "arbitrary")),
        name="mlstm_rev" if rev else "mlstm_fwd",
    )(*args)
    return out[0], (out[1], out[2], out[3])


def _rwkv_core(r, lw, kd, v, kk, a, s_ref, rev):
    c = r.shape[0]
    g_w = RK_GROUP
    nh = g_w // RK_HD
    ng = r.shape[1] // g_w
    assert c == RK_HD and c & (c - 1) == 0
    sh = c.bit_length() - 1
    rj = lax.broadcasted_iota(jnp.int32, (g_w, g_w), 0)
    cj = lax.broadcasted_iota(jnp.int32, (g_w, g_w), 1)
    blk = (rj >> sh) == (cj >> sh)
    tt, ss = rj & (c - 1), cj & (c - 1)
    incl = blk & ((ss >= tt) if rev else (ss <= tt))
    strict = blk & ((ss > tt) if rev else (ss < tt))
    eye = (rj == cj).astype(F32)
    tri = incl[:c, :c].astype(F32).astype(MXU_DT)
    lo_half = lax.broadcasted_iota(jnp.int32, (2 * g_w, 2 * c), 1) < c
    last = 0 if rev else c - 1
    zero = jnp.zeros((), F32)
    groups = range(ng)

    def bd(x):
        return _mx(jnp.where(blk, jnp.concatenate([x] * nh, axis=0), zero))

    def rsum(z):
        out = z[0:c]
        for i in range(1, nh):
            out = out + z[i * c:(i + 1) * c]
        return out

    ka = kk * a
    gi = _cumsum_mm(tri, lw)
    tot = gi[last:last + 1, :]
    e_neg = jnp.exp(-gi)
    e_tail = jnp.exp(tot - gi)
    a_t = -kk * jnp.exp(gi - lw)
    b_t = _mx(ka * e_neg)
    k_t = _mx(kd * e_neg)
    r_t = r * jnp.exp(gi)
    b_hat = ka * e_tail
    k_hat = kd * e_tail
    dec = jnp.exp(tot)
    gs = lambda x, g: x[:, g * g_w:(g + 1) * g_w]

    a_bd = [bd(gs(a_t, g)) for g in groups]
    r_bd = [bd(gs(r_t, g)) for g in groups]
    v_bd = [bd(gs(v, g)) for g in groups]
    l_ab, a_ak, a_rb, a_rk = [], [], [], []
    for g in groups:
        prod = _nt(jnp.concatenate([a_bd[g], r_bd[g]], axis=0),
                   jnp.concatenate([gs(b_t, g), gs(k_t, g)], axis=0))
        swp = pltpu.roll(prod, c, 1)
        xb = jnp.where(lo_half, prod, swp)
        xk = jnp.where(lo_half, swp, prod)
        xb = jnp.concatenate([xb, xb], axis=1)
        xk = jnp.concatenate([xk, xk], axis=1)
        l_ab.append(jnp.where(strict, xb[:g_w], zero))
        a_ak.append(jnp.where(strict, xk[:g_w], zero))
        a_rb.append(rsum(jnp.where(incl, xb[g_w:], zero)))
        a_rk.append(rsum(jnp.where(incl, xk[g_w:], zero)))

    xs = [_mx(l) for l in l_ab]
    ps = [eye + l for l in l_ab]
    for _ in range(max(1, (c - 1).bit_length() - 1)):
        xs = [_mx(_nn(x, x)) for x in xs]
        ps = [p + _nn(_mx(p), x) for p, x in zip(ps, xs)]

    ys = []
    for g in groups:
        z_ak = _nn(_mx(a_ak[g]), v_bd[g])
        wu = _nn(_mx(ps[g]), jnp.concatenate([a_bd[g], _mx(z_ak)], axis=1))
        w_cat = rsum(wu[:, :g_w])
        u0 = rsum(wu[:, g_w:])
        y_k = _nn(_mx(a_rk[g]), v_bd[g])
        s_mem = s_ref[0, g]
        s_b = _mx(s_mem)
        u = _nt(_mx(w_cat), s_b) + u0
        ys.append(_nt(_mx(gs(r_t, g)), s_b) + _nn(_mx(a_rb[g]), bd(u)) + y_k)
        upd = _tn(_mx(jnp.concatenate([u, gs(v, g)], axis=0)),
                  _mx(jnp.concatenate([gs(b_hat, g), gs(k_hat, g)], axis=0)))
        s_ref[0, g] = s_mem * gs(dec, g) + jnp.where(blk, upd, zero)
    return jnp.concatenate(ys, axis=1)


def _rwkv_seq_kernel(*refs, rev, has_acc):
    r_ref, lw_ref, kd_ref, v_ref, kk_ref, a_ref, s0_ref = refs[:7]
    acc_ref = refs[7] if has_acc else None
    y_ref, s_ref = refs[7 + has_acc:]

    @pl.when(pl.program_id(1) == 0)
    def _():
        s_ref[...] = s0_ref[...]

    y = _rwkv_core(r_ref[0], lw_ref[0], kd_ref[0], v_ref[0], kk_ref[0], a_ref[0], s_ref, rev)
    if has_acc:
        y = y + acc_ref[0]
    y_ref[0] = y


def _rwkv_scan_seq(r, lw, kd, v, kk, a, state, rev, acc=None):
    b, l, w = r.shape
    c = RK_CHUNK
    nc = l // c
    cidx = (lambda bi, ch: (bi, nc - 1 - ch, 0)) if rev else (lambda bi, ch: (bi, ch, 0))
    seq = pl.BlockSpec((1, c, w), cidx)
    st = pl.BlockSpec((1,) + state.shape[1:], lambda bi, ch: (bi, 0, 0, 0))
    args = [r, lw, kd, v, kk, a, state] + ([acc] if acc is not None else [])
    y, s = pl.pallas_call(
        functools.partial(_rwkv_seq_kernel, rev=rev, has_acc=acc is not None),
        grid=(b, nc),
        in_specs=[seq] * 6 + [st] + ([seq] if acc is not None else []),
        out_specs=[seq, st],
        out_shape=[jax.ShapeDtypeStruct((b, l, w), F32), jax.ShapeDtypeStruct(state.shape, F32)],
        compiler_params=_params(("parallel", "arbitrary")),
        name="rwkv_seq_rev" if rev else "rwkv_seq_fwd",
    )(*args)
    return y, s


_PV_ROWS = ('mu_r', 'mu_k', 'mu_v', 'w0', 'a0', 'kk', 'ka', 'rk', 'v0')


def _rwkv_grid_kernel(*refs, rev, nc, has_vf, has_acc, emit_vf):
    it = iter(refs)
    pr, pr_u, pr_d, pk, pk_u, pk_d, pv, pv_u, pv_d, wa, wa_u, wa_d = (next(it) for _ in range(12))
    vf_ref, lora_ref = (next(it), next(it)) if has_vf else (None, None)
    pvec, mu_wa, w2_ref, a2_ref, s0_ref = (next(it) for _ in range(5))
    accy_ref, accb_ref = (next(it), next(it)) if has_acc else (None, None)
    y_ref, bon_ref, s_ref = (next(it) for _ in range(3))
    vfo_ref = next(it) if emit_vf else None
    step = pl.program_id(1)

    @pl.when(step == 0)
    def _():
        s_ref[...] = s0_ref[...]

    c = pr.shape[1]
    w = pr.shape[2]
    rd = w // 16
    ci = nc - 1 - step if rev else step
    has_up = jnp.where(ci > 0, 1.0, 0.0)
    has_dn = jnp.where(ci < nc - 1, 1.0, 0.0)
    trow = lax.broadcasted_iota(jnp.int32, (c, 1), 0)
    prow = lambda name: pvec[_PV_ROWS.index(name):_PV_ROWS.index(name) + 1, :]

    def left(x):
        return jnp.where(trow == 0, 0.0, pltpu.roll(x, 1, 0))

    def right(x):
        return jnp.where(trow == c - 1, 0.0, pltpu.roll(x, c - 1, 0))

    def lerp_wide(cur_ref, up_ref, dn_ref, mu):
        cur = cur_ref[0]
        q = w // 4
        sh = jnp.concatenate([left(cur[:, :q]), right(cur[:, q:2 * q]),
                              up_ref[0] * has_up, dn_ref[0] * has_dn], axis=1)
        return cur + mu * (sh - cur)

    r = lerp_wide(pr, pr_u, pr_d, prow('mu_r'))
    k = lerp_wide(pk, pk_u, pk_d, prow('mu_k'))
    v = lerp_wide(pv, pv_u, pv_d, prow('mu_v'))

    cur = wa[0]
    qd = (lax.broadcasted_iota(jnp.int32, cur.shape, 1) & (rd - 1)) >> ((rd // 4).bit_length() - 1)
    sh = jnp.where(qd == 0, left(cur), jnp.where(qd == 1, right(cur),
                                                 jnp.where(qd == 2, wa_u[0] * has_up, wa_d[0] * has_dn)))
    xwa = cur + mu_wa[...] * (sh - cur)
    d = 1 if rev else 0
    xw = xwa[:, d * rd:(d + 1) * rd]
    xa = xwa[:, (2 + d) * rd:(3 + d) * rd]
    z = -(prow('w0') + _nn(_mx(jnp.tanh(xw)), w2_ref[...]))
    softplus = jnp.maximum(z, 0.0) + jnp.log(1.0 + jnp.exp(-jnp.abs(z)))
    lw = -jnp.exp(-softplus - 0.5)
    a = _sigmoid(prow('a0') + _nn(_mx(xa), a2_ref[...]))
    if has_vf:
        v = v + (vf_ref[0] - v) * _sigmoid(prow('v0') + lora_ref[0])
    if emit_vf:
        vfo_ref[0] = v
    ones = _seg_ones(RK_HD)
    kkn = k * prow('kk')
    kk = kkn / jnp.maximum(jnp.sqrt(_segsum(kkn * kkn, ones)), 1e-12)
    kd = k * (1.0 + (a - 1.0) * prow('ka'))
    bon = _segsum(r * kd * prow('rk'), ones) * v
    y = _rwkv_core(r, lw, kd, v, kk, a, s_ref, rev)
    if has_acc:
        y = y + accy_ref[0]
        bon = bon + accb_ref[0]
    y_ref[0] = y
    bon_ref[0] = bon


def _rwkv_scan_grid(pr, pk, pv, wa, pvec, mu_wa, w2, a2, state, rev, vf=None, lora=None,
                    acc=None, emit_vf=False):
    b, l, w = pr.shape
    c = RK_CHUNK
    assert c == GRID_W
    nc = l // c
    wq = w // 4
    ch = (lambda s: nc - 1 - s) if rev else (lambda s: s)
    cur = lambda bi, s: (bi, ch(s), 0)
    upi = lambda lane: (lambda bi, s: (bi, jnp.maximum(ch(s) - 1, 0), lane))
    dni = lambda lane: (lambda bi, s: (bi, jnp.minimum(ch(s) + 1, nc - 1), lane))
    seq = pl.BlockSpec((1, c, w), cur)
    wide = [seq, pl.BlockSpec((1, c, wq), upi(2)), pl.BlockSpec((1, c, wq), dni(3))]
    wa_w = wa.shape[2]
    small = [pl.BlockSpec((1, c, wa_w), cur), pl.BlockSpec((1, c, wa_w), upi(0)), pl.BlockSpec((1, c, wa_w), dni(0))]
    full = lambda x: pl.BlockSpec(x.shape, lambda bi, s: (0,) * x.ndim)
    st = pl.BlockSpec((1,) + state.shape[1:], lambda bi, s: (bi, 0, 0, 0))
    in_specs = wide * 3 + small
    args = [pr, pr, pr, pk, pk, pk, pv, pv, pv, wa, wa, wa]
    if vf is not None:
        in_specs += [seq, seq]
        args += [vf, lora]
    in_specs += [full(pvec), full(mu_wa), full(w2), full(a2), st]
    args += [pvec, mu_wa, w2, a2, state]
    if acc is not None:
        in_specs += [seq, seq]
        args += list(acc)
    out_specs = [seq, seq, st]
    out_shape = [jax.ShapeDtypeStruct((b, l, w), F32), jax.ShapeDtypeStruct((b, l, w), F32),
                 jax.ShapeDtypeStruct(state.shape, F32)]
    if emit_vf:
        out_specs.append(seq)
        out_shape.append(jax.ShapeDtypeStruct((b, l, w), F32))
    return pl.pallas_call(
        functools.partial(_rwkv_grid_kernel, rev=rev, nc=nc, has_vf=vf is not None,
                          has_acc=acc is not None, emit_vf=emit_vf),
        grid=(b, nc),
        in_specs=in_specs,
        out_specs=out_specs,
        out_shape=out_shape,
        compiler_params=_params(("parallel", "arbitrary")),
        name="rwkv_grid_rev" if rev else "rwkv_grid_fwd",
    )(*args)


def _hgrn_kernel(*refs, rev, heads, hd, has_acc):
    q_ref, f_ref, v_ref, fb_ref, lb_ref, s0_ref = refs[:6]
    acc_ref = refs[6] if has_acc else None
    o_ref, s_ref = refs[6 + has_acc:]

    @pl.when(pl.program_id(1) == 0)
    def _():
        s_ref[...] = s0_ref[...]

    tt = q_ref.shape[1]
    sub = HG_SUB
    sh = sub.bit_length() - 1
    ri = lax.broadcasted_iota(jnp.int32, (tt, tt), 0)
    ci = lax.broadcasted_iota(jnp.int32, (tt, tt), 1)
    tri = (((ri >> sh) == (ci >> sh)) & ((ci >= ri) if rev else (ci <= ri))).astype(F32).astype(MXU_DT)
    r16 = lax.broadcasted_iota(jnp.int32, (sub, sub), 0)
    c16 = lax.broadcasted_iota(jnp.int32, (sub, sub), 1)
    mask = (c16 >= r16) if rev else (c16 <= r16)
    lb = lb_ref[...]
    f = lb + (1.0 - lb) * _sigmoid(f_ref[0] + fb_ref[...])
    g_all = _cumsum_mm(tri, jnp.log(f))
    k_all = 1.0 - f
    q_all = _silu(q_ref[0])
    mid = sub // 2 if rev else sub // 2 - 1
    last = 0 if rev else sub - 1
    nsub = tt // sub
    v_all = _mx(v_ref[0])
    acc_all = acc_ref[0] if has_acc else None
    s_mem = [s_ref[0, h] for h in range(heads)]
    for step in range(nsub):
        j = nsub - 1 - step if rev else step
        rows = slice(j * sub, (j + 1) * sub)
        gj = g_all[rows]
        g_mid = gj[mid:mid + 1]
        g_last = gj[last:last + 1]
        q = q_all[rows]
        k = k_all[rows]
        q1 = _mx(q * jnp.exp(gj - g_mid))
        k1 = _mx(k * jnp.exp(g_mid - gj))
        qg = _mx(q * jnp.exp(gj))
        kl = _mx(k * jnp.exp(g_last - gj))
        dec = jnp.exp(g_last)
        v = v_all[rows]
        outs = []
        for h in range(heads):
            sl = slice(h * hd, (h + 1) * hd)
            att = jnp.where(mask, _nt(q1[:, sl], k1[:, sl]), jnp.zeros((), F32))
            outs.append(_nn(_mx(att), v[:, sl]) + _nt(qg[:, sl], _mx(s_mem[h])))
            s_mem[h] = s_mem[h] * dec[:, sl] + _tn(v[:, sl], kl[:, sl])
        o = jnp.concatenate(outs, axis=1)
        if has_acc:
            o = o + acc_all[rows]
        o_ref[0, rows, :] = o
    for h in range(heads):
        s_ref[0, h] = s_mem[h]


def _hgrn_scan(pq, pf, pi, f_b, lb, state, rev, acc=None):
    b, l, w = pq.shape
    heads = HG_HEADS
    hd = w // heads
    tt = min(HG_TILE, l)
    nc = l // tt
    d = 1 if rev else 0
    ch = (lambda s: nc - 1 - s) if rev else (lambda s: s)
    seq = pl.BlockSpec((1, tt, w), lambda bi, s: (bi, ch(s), 0))
    fseq = pl.BlockSpec((1, tt, w), lambda bi, s: (bi, ch(s), d))
    vec = pl.BlockSpec((1, w), lambda bi, s: (0, 0))
    st = pl.BlockSpec((1,) + state.shape[1:], lambda bi, s: (bi, 0, 0, 0))
    args = [pq, pf, pi, f_b, lb, state] + ([acc] if acc is not None else [])
    o, s = pl.pallas_call(
        functools.partial(_hgrn_kernel, rev=rev, heads=heads, hd=hd, has_acc=acc is not None),
        grid=(b, nc),
        in_specs=[seq, fseq, seq, vec, vec, st] + ([seq] if acc is not None else []),
        out_specs=[seq, st],
        out_shape=[jax.ShapeDtypeStruct((b, l, w), F32), jax.ShapeDtypeStruct(state.shape, F32)],
        compiler_params=_params(("parallel", "arbitrary")),
        name="hgrn_rev" if rev else "hgrn_fwd",
    )(*args)
    return o, s


_EV_ROWS = ('ml_norm_w', 'rk_ln_w', 'rk_ln_b', 'hg_norm_w', 'gate_b0', 'gate_b1', 'gate_b2', 'final_w')


def _epilogue_kernel(hm_ref, mo_ref, mz_ref, yr_ref, bon_ref, rz_ref, oh_ref, hz_ref,
                     g0_ref, g1_ref, g2_ref, xs_ref, gx_ref, ev_ref,
                     wpm_ref, wpr_ref, wph_ref, wout_ref, o_ref, *, final):
    ev = lambda name: ev_ref[_EV_ROWS.index(name):_EV_ROWS.index(name) + 1, :]
    d = hm_ref.shape[2]

    def seg_mean(x, seg):
        return _segsum(x, _seg_ones(seg)) * (1.0 / seg)

    seg = d // ML_HEADS
    y = hm_ref[0]
    y = y - seg_mean(y, seg)
    y = y * lax.rsqrt(seg_mean(y * y, seg) + NORM_EPS) * ev('ml_norm_w')
    u_m = _sigmoid(mo_ref[0]) * y * _silu(mz_ref[0])
    y = yr_ref[0]
    y = y - seg_mean(y, RK_HD)
    y = y * lax.rsqrt(seg_mean(y * y, RK_HD) + RK_GN_EPS) * ev('rk_ln_w') + ev('rk_ln_b') + bon_ref[0]
    u_r = y * _silu(rz_ref[0])
    seg = d // HG_HEADS
    o = oh_ref[0]
    o = o * lax.rsqrt(seg_mean(o * o, seg) + NORM_EPS) * ev('hg_norm_w')
    u_h = o * _silu(hz_ref[0])

    merged = (_sigmoid(g0_ref[0] + ev('gate_b0')) * _nn(_mx(u_m), wpm_ref[...])
              + _sigmoid(g1_ref[0] + ev('gate_b1')) * _nn(_mx(u_r), wpr_ref[...])
              + _sigmoid(g2_ref[0] + ev('gate_b2')) * _nn(_mx(u_h), wph_ref[...]))
    xs = xs_ref[0] + gx_ref[0] * _nn(_mx(merged), wout_ref[...])
    if final:
        xs = xs * lax.rsqrt(jnp.mean(xs * xs, axis=-1, keepdims=True) + NORM_EPS) * ev('final_w')
    o_ref[0] = xs


def _epilogue(hm, mo, mz, yr, bon, rz, oh, hz, gate, xs, gx, p, final_w):
    b, l, d = xs.shape
    tt = min(EP_TILE, l)
    seq = pl.BlockSpec((1, tt, d), lambda bi, i: (bi, i, 0))
    gseq = lambda lane: pl.BlockSpec((1, tt, d), lambda bi, i: (bi, i, lane))
    ev = jnp.stack([p['ml_norm_w'], p['rk_ln_w'], p['rk_ln_b'], p['hg_norm_w'],
                    p['gate_b'][0], p['gate_b'][1], p['gate_b'][2],
                    final_w if final_w is not None else jnp.ones((d,), F32)], axis=0)
    wspec = pl.BlockSpec((d, d), lambda bi, i: (0, 0))
    ws = [_mx(p[n]) for n in ('w_pm', 'w_pr', 'w_ph', 'w_out')]
    return pl.pallas_call(
        functools.partial(_epilogue_kernel, final=final_w is not None),
        grid=(b, l // tt),
        in_specs=[seq] * 8 + [gseq(0), gseq(1), gseq(2), seq,
                              pl.BlockSpec((1, 1, d), lambda bi, i: (bi, 0, 0)),
                              pl.BlockSpec(ev.shape, lambda bi, i: (0, 0))] + [wspec] * 4,
        out_specs=seq,
        out_shape=jax.ShapeDtypeStruct((b, l, d), F32),
        compiler_params=_params(("parallel", "parallel")),
        name="epilogue",
    )(hm, mo, mz, yr, bon, rz, oh, hz, gate, gate, gate, xs, gx, ev, *ws)


def _rms_norm(x, w):
    return x * lax.rsqrt(jnp.mean(x * x, axis=-1, keepdims=True) + NORM_EPS) * w


def _bi_shift_seq(u):
    half = u.shape[-1] // 2
    g = jnp.pad(u, ((0, 0), (1, 1), (0, 0)))
    return jnp.concatenate((g[:, :-2, :half], g[:, 2:, half:]), axis=-1)


def _in_layout(d):
    rd = d // 16
    return (('m_q', d), ('m_k', d), ('m_v', d), ('m_o', d), ('m_z', d), ('m_if', 4 * ML_HEADS),
            ('r_r', d), ('r_k', d), ('r_v', d), ('r_z', d), ('r_wa', 4 * rd),
            ('h_q', d), ('h_f', 2 * d), ('h_i', d), ('h_z', d), ('gate', 3 * d))


def _split_w_in(w_in):
    out, off = {}, 0
    for name, width in _in_layout(w_in.shape[0]):
        out[name] = w_in[:, off:off + width].astype(MXU_DT)
        off += width
    return out


def _zero_states(b, d):
    ml_hd = d // ML_HEADS
    ml = (jnp.zeros((b, ML_HEADS, ml_hd, ml_hd), F32), jnp.zeros((b, ML_HEADS, 1, ml_hd), F32),
          jnp.zeros((b, ML_HEADS, 8, 128), F32))
    rk = jnp.zeros((b, d // RK_GROUP, RK_GROUP, RK_GROUP), F32)
    hg_hd = d // HG_HEADS
    hg = jnp.zeros((b, HG_HEADS, hg_hd, hg_hd), F32)
    return ((ml, ml), (rk, rk), (hg, hg))


def _mlstm_branch(proj, p, init, b, l):
    gl = proj('m_if').reshape(b, l, 2, 2, ML_HEADS) + p['ml_if_b']
    acc, states = None, []
    for dr in (0, 1):
        log_i = gl[:, :, dr, 0]
        log_f = jax.nn.log_sigmoid(gl[:, :, dr, 1])
        acc, st = _mlstm_scan(proj('m_q'), proj('m_k'), proj('m_v'), p['ml_conv'], log_i, log_f,
                              init[dr], rev=bool(dr), acc=acc)
        states.append(st)
    return acc, tuple(states)


def _rwkv7_seq_branch(proj, h2, p, init, v_first, b, l, d):
    nh, n = d // RK_HD, RK_HD
    rd = d // 16
    mu = p['rk_mu']

    def lerp_shift(u, m):
        return u + m * (_bi_shift_seq(u) - u)

    r = lerp_shift(proj('r_r'), mu[0:d])
    k = lerp_shift(proj('r_k'), mu[d:2 * d])
    v = lerp_shift(proj('r_v'), mu[2 * d:3 * d])
    pwa = proj('r_wa')
    xwa = jnp.concatenate([lerp_shift(pwa[..., i * rd:(i + 1) * rd], mu[3 * d + i * rd:3 * d + (i + 1) * rd])
                           for i in range(4)], axis=-1)
    if p['rk_v0'] is None:
        v_first = v
    else:
        lora = _mm(_mm(h2, p['rk_v1']), p['rk_v2']).reshape(b, l, d)
        v = v + (v_first - v) * jax.nn.sigmoid(p['rk_v0'] + lora)
    kk = (k * p['rk_kk']).reshape(b, l, nh, n)
    kk = kk / jnp.maximum(jnp.sqrt(jnp.sum(kk * kk, axis=-1, keepdims=True)), 1e-12)
    kk = kk.reshape(b, l, d)
    acc, states, kd_sum = None, [], 0.0
    for dr in (0, 1):
        xw = xwa[..., dr * rd:(dr + 1) * rd]
        xa = xwa[..., (2 + dr) * rd:(3 + dr) * rd]
        log_w = -jax.nn.softplus(-(p['rk_w0'][dr] + _mm(jnp.tanh(xw).reshape(b * l, rd), p['rk_w2'][dr]).reshape(b, l, d))) - 0.5
        lw = -jnp.exp(log_w)
        a = jax.nn.sigmoid(p['rk_a0'][dr] + _mm(xa.reshape(b * l, rd), p['rk_a2'][dr]).reshape(b, l, d))
        kd = k * (1.0 + (a - 1.0) * p['rk_ka'])
        acc, s = _rwkv_scan_seq(r, lw, kd, v, kk, a, init[dr], rev=bool(dr), acc=acc)
        states.append(s)
        kd_sum = kd_sum + kd
    bonus = jnp.sum((r * kd_sum * p['rk_rk']).reshape(b, l, nh, n), axis=-1, keepdims=True) * v.reshape(b, l, nh, n)
    return acc, bonus.reshape(b, l, d), tuple(states), v_first


def _rwkv7_grid_branch(proj, h2, p, init, v_first, b, l, d):
    mu = p['rk_mu']
    zeros = jnp.zeros((d,), F32)
    has_vf = p['rk_v0'] is not None
    lora = _mm(_mm(h2, p['rk_v1']), p['rk_v2']).reshape(b, l, d) if has_vf else None
    acc, states, vf_out = None, [], v_first
    for dr in (0, 1):
        rows = {'mu_r': mu[0:d], 'mu_k': mu[d:2 * d], 'mu_v': mu[2 * d:3 * d], 'w0': p['rk_w0'][dr],
                'a0': p['rk_a0'][dr], 'kk': p['rk_kk'], 'ka': p['rk_ka'], 'rk': p['rk_rk'],
                'v0': p['rk_v0'] if has_vf else zeros}
        pvec = jnp.stack([rows[n] for n in _PV_ROWS], axis=0)
        emit_vf = (not has_vf) and dr == 0
        out = _rwkv_scan_grid(proj('r_r'), proj('r_k'), proj('r_v'), proj('r_wa'), pvec, mu[3 * d:][None],
                              _mx(p['rk_w2'][dr]), _mx(p['rk_a2'][dr]), init[dr], rev=bool(dr),
                              vf=v_first if has_vf else None, lora=lora, acc=acc, emit_vf=emit_vf)
        acc = (out[0], out[1])
        states.append(out[2])
        if emit_vf:
            vf_out = out[3]
    return acc[0], acc[1], tuple(states), vf_out


def _hgrn2_branch(proj, p, init):
    acc, states = None, []
    for dr in (0, 1):
        acc, s = _hgrn_scan(proj('h_q'), proj('h_f'), proj('h_i'), p['hg_f_b'][dr][None], p['hg_lb'][dr][None],
                            init[dr], rev=bool(dr), acc=acc)
        states.append(s)
    return acc, tuple(states)


def _mixer(h, p, init, on_grid, v_first, need_out, xs, gx, final_w):
    b, l, d = h.shape
    h2 = _mx(h.reshape(b * l, d))
    cache = {}

    def proj(name):
        if name not in cache:
            w = p['w_in'][name]
            cache[name] = _mm(h2, w).reshape(b, l, w.shape[1])
        return cache[name]

    hm, st_m = _mlstm_branch(proj, p, init[0], b, l)
    if on_grid:
        yr, bon, st_r, v_first = _rwkv7_grid_branch(proj, h2, p, init[1], v_first, b, l, d)
    else:
        yr, bon, st_r, v_first = _rwkv7_seq_branch(proj, h2, p, init[1], v_first, b, l, d)
    oh, st_h = _hgrn2_branch(proj, p, init[2])
    states = (st_m, st_r, st_h)
    if not need_out:
        return None, states, v_first
    out = _epilogue(hm, proj('m_o'), proj('m_z'), yr, bon, proj('r_z'), oh, proj('h_z'), proj('gate'),
                    xs, gx, p, final_w)
    return out, states, v_first


def kernel(x, c, ctx, c_ctx, norm_w, ada_w, ada_b, w_in, gate_b, ml_conv, ml_if_b, ml_norm_w, rk_mu, rk_w0, rk_w2, rk_a0, rk_a2, rk_kk, rk_ka, rk_rk, rk_v0, rk_v1, rk_v2, rk_ln_w, rk_ln_b, hg_f_b, hg_lb, hg_norm_w, w_pm, w_pr, w_ph, w_out, final_norm_w):
    batch, _, d = x.shape
    depth = w_in.shape[0]
    lb_p = jax.nn.softmax(hg_lb.astype(F32), axis=1)
    lower_bounds = jnp.cumsum(lb_p, axis=1) - lb_p[:, :1]
    xs, cs = x, ctx
    vf_x, vf_c = None, None
    cond = jnp.concatenate([jax.nn.silu(c), jax.nn.silu(c_ctx)[None]], axis=0)
    for l in range(depth):
        last = l == depth - 1
        p = {'w_in': _split_w_in(w_in[l]), 'gate_b': gate_b[l], 'ml_conv': ml_conv[l], 'ml_if_b': ml_if_b[l],
             'ml_norm_w': ml_norm_w[l], 'rk_mu': rk_mu[l], 'rk_w0': rk_w0[l], 'rk_w2': rk_w2[l],
             'rk_a0': rk_a0[l], 'rk_a2': rk_a2[l], 'rk_kk': rk_kk[l], 'rk_ka': rk_ka[l], 'rk_rk': rk_rk[l],
             'rk_v0': rk_v0[l - 1] if l > 0 else None, 'rk_v1': rk_v1[l - 1] if l > 0 else None,
             'rk_v2': rk_v2[l - 1] if l > 0 else None, 'rk_ln_w': rk_ln_w[l], 'rk_ln_b': rk_ln_b[l],
             'hg_f_b': hg_f_b[l], 'hg_lb': lower_bounds[:, l], 'hg_norm_w': hg_norm_w[l],
             'w_pm': w_pm[l], 'w_pr': w_pr[l], 'w_ph': w_ph[l], 'w_out': w_out[l]}
        mod = _mm(cond, ada_w[l]) + ada_b[l]
        shift_x, scale_x, gate_x = jnp.split(mod[:batch, None, :], 3, axis=-1)
        shift_c, scale_c, gate_c = jnp.split(mod[batch], 3, axis=-1)
        hc = _rms_norm(cs, norm_w[l]) * (1.0 + scale_c) + shift_c
        gc = jnp.broadcast_to(gate_c[None, None, :], (batch, 1, d))
        cs_new, st_c, vf_c = _mixer(hc, p, _zero_states(batch, d), False, vf_c, not last, cs, gc, None)
        hx = _rms_norm(xs, norm_w[l]) * (1.0 + scale_x) + shift_x
        xs, _, vf_x = _mixer(hx, p, st_c, True, vf_x, True, xs, gate_x, final_norm_w if last else None)
        if not last:
            cs = cs_new
    return xs
```

```python
import functools

import jax
import jax.numpy as jnp
from jax import lax
from jax.experimental import pallas as pl
from jax.experimental.pallas import tpu as pltpu

F32 = jnp.float32
MXU_DT = jnp.bfloat16

NORM_EPS = 1e-6
GRID_W = 64
ML_HEADS = 4
ML_CHUNK = 256
RK_HD = 64
RK_CHUNK = 64
RK_GROUP = 256
RK_BATCH = 2
RK_GN_EPS = 64e-5
HG_HEADS = 8
HG_SUB = 16
HG_TILE = 128
EP_TILE = 256
SEG_W = 256
VMEM_LIMIT = 56 * 1024 * 1024


def _nt(a, b):
    return lax.dot_general(a, b, (((1,), (1,)), ((), ())), preferred_element_type=F32)


def _tn(a, b):
    return lax.dot_general(a, b, (((0,), (0,)), ((), ())), preferred_element_type=F32)


def _nn(a, b):
    return jnp.dot(a, b, preferred_element_type=F32)


def _mx(a):
    return a.astype(MXU_DT)


def _split2(x):
    hi = x.astype(MXU_DT)
    return hi, (x - hi.astype(F32)).astype(MXU_DT)


def _cumsum_mm(tri, x):
    hi = x.astype(MXU_DT)
    r1 = x - hi.astype(F32)
    mid = r1.astype(MXU_DT)
    lo = (r1 - mid.astype(F32)).astype(MXU_DT)
    one = lambda t: _nn(t, hi) + _nn(t, mid) + _nn(t, lo)
    return [one(t) for t in tri] if isinstance(tri, (list, tuple)) else one(tri)


def _seg_ones(seg):
    sh = seg.bit_length() - 1
    ri = lax.broadcasted_iota(jnp.int32, (SEG_W, SEG_W), 0)
    ci = lax.broadcasted_iota(jnp.int32, (SEG_W, SEG_W), 1)
    return ((ri >> sh) == (ci >> sh)).astype(F32).astype(MXU_DT)


def _segsum(x, ones):
    outs = []
    for g in range(x.shape[1] // SEG_W):
        hi, lo = _split2(x[:, g * SEG_W:(g + 1) * SEG_W])
        outs.append(_nn(hi, ones) + _nn(lo, ones))
    return outs[0] if len(outs) == 1 else jnp.concatenate(outs, axis=1)


def _sigmoid(x):
    return jax.nn.sigmoid(x)


def _silu(x):
    return x * jax.nn.sigmoid(x)


def _params(sem):
    return pltpu.CompilerParams(dimension_semantics=sem, vmem_limit_bytes=VMEM_LIMIT)


def _mm_kernel(x_ref, w_ref, o_ref):
    o_ref[...] = _nn(_mx(x_ref[...]), w_ref[...])


def _mm(x, w):
    m, k = x.shape
    n = w.shape[1]
    mp = -(-m // 8) * 8
    npad = -(-n // 128) * 128
    if mp != m:
        x = jnp.pad(x, ((0, mp - m), (0, 0)))
    wb = w.astype(MXU_DT)
    if npad != n:
        wb = jnp.pad(wb, ((0, 0), (0, npad - n)))
    tm = next(t for t in (1024, 512, 256, 128, 64, 32, 16, 8) if mp % t == 0)
    tn = next(t for t in (1024, 512, 256, 128) if npad % t == 0)
    out = pl.pallas_call(
        _mm_kernel,
        grid=(mp // tm, npad // tn),
        in_specs=[pl.BlockSpec((tm, k), lambda i, j: (i, 0)),
                  pl.BlockSpec((k, tn), lambda i, j: (0, j))],
        out_specs=pl.BlockSpec((tm, tn), lambda i, j: (i, j)),
        out_shape=jax.ShapeDtypeStruct((mp, npad), F32),
        compiler_params=_params(("parallel", "parallel")),
        name="mm",
    )(x, wb)
    if mp != m or npad != n:
        out = out[:m, :n]
    return out


def _mlstm_kernel(*refs, rev, heads, hd, nc, has_acc):
    (q_ref, qp_ref, qn_ref, k_ref, kp_ref, kn_ref, v_ref, cw_ref, gcol_ref, grow_ref,
     c0_ref, n0_ref, m0_ref) = refs[:13]
    acc_ref = refs[13] if has_acc else None
    h_ref, c_ref, n_ref, m_ref = refs[13 + has_acc:]
    step = pl.program_id(1)

    @pl.when(step == 0)
    def _():
        c_ref[...] = c0_ref[...]
        n_ref[...] = n0_ref[...]
        m_ref[...] = m0_ref[...]

    t = q_ref.shape[1]
    ci = nc - 1 - step if rev else step
    has_prev = jnp.where(ci > 0, 1.0, 0.0)
    has_next = jnp.where(ci < nc - 1, 1.0, 0.0)
    trow = lax.broadcasted_iota(jnp.int32, (t, 1), 0)

    def conv_silu(u, prev_blk, next_blk, w3):
        up = jnp.where(trow == 0, prev_blk[7:8, :] * has_prev, pltpu.roll(u, 1, 0))
        dn = jnp.where(trow == t - 1, next_blk[0:1, :] * has_next, pltpu.roll(u, t - 1, 0))
        return _silu(up * w3[0:1, :] + u * w3[1:2, :] + dn * w3[2:3, :])

    q_all = conv_silu(q_ref[0], qp_ref[0], qn_ref[0], cw_ref[0:3, :]) * (hd ** -0.5)
    k_all = conv_silu(k_ref[0], kp_ref[0], kn_ref[0], cw_ref[3:6, :])

    row = lax.broadcasted_iota(jnp.int32, (t, t), 0)
    col = lax.broadcasted_iota(jnp.int32, (t, t), 1)
    mask = (col >= row) if rev else (col <= row)
    last = 0 if rev else t - 1
    for h in range(heads):
        sl = slice(h * hd, (h + 1) * hd)
        q32 = q_all[:, sl]
        qc = _mx(q32)
        kc = k_all[:, sl]
        vc = _mx(v_ref[0, :, sl])
        bc_col = gcol_ref[0, :, h:h + 1]
        i_col = gcol_ref[0, :, heads + h:heads + h + 1]
        bc_row = grow_ref[0, 0, h:h + 1, :]
        i_row = grow_ref[0, 0, heads + h:heads + h + 1, :]
        m_prev = m_ref[0, h, 0:1, 0:1]
        c_mem = c_ref[0, h]
        n_mem = n_ref[0, h]

        dmat = jnp.where(mask, bc_col - bc_row + i_row, -jnp.inf)
        inter = bc_col + m_prev
        m_t = jnp.maximum(jnp.max(dmat, axis=-1, keepdims=True), inter)
        s = _nt(qc, _mx(kc)) * jnp.exp(dmat - m_t)
        w_inter = jnp.exp(inter - m_t)
        num = _nn(_mx(s), vc) + w_inter * _nn(qc, _mx(c_mem))
        qn = jnp.sum(q32 * n_mem, axis=-1, keepdims=True)
        den = jnp.sum(s, axis=-1, keepdims=True) + w_inter * qn
        h_out = num / jnp.maximum(jnp.abs(den), jnp.exp(-m_t))
        if has_acc:
            h_out = h_out + acc_ref[0, :, sl]
        h_ref[0, :, sl] = h_out

        total = bc_col[last:last + 1, :]
        g_col = total - bc_col + i_col
        m_new = jnp.maximum(total + m_prev, jnp.max(g_col, axis=0, keepdims=True))
        wk = jnp.exp(g_col - m_new)
        dec = jnp.exp(total + m_prev - m_new)
        kw = kc * wk
        c_ref[0, h] = dec * c_mem + _tn(_mx(kw), vc)
        n_ref[0, h] = dec * n_mem + jnp.sum(kw, axis=0, keepdims=True)
        m_ref[0, h] = jnp.broadcast_to(m_new, m_ref.shape[2:])


def _mlstm_scan(pq, pk, v, conv_w, log_i, log_f, state, rev, acc=None):
    b, l, w = pq.shape
    heads = ML_HEADS
    hd = w // heads
    t = min(ML_CHUNK, l)
    assert l % t == 0
    nc = l // t
    nb8 = l // 8
    lf = log_f.reshape(b, nc, t, heads)
    if rev:
        bcum = jnp.flip(jnp.cumsum(jnp.flip(lf, 2), axis=2), 2)
    else:
        bcum = jnp.cumsum(lf, axis=2)
    gcol = jnp.concatenate([bcum.reshape(b, l, heads), log_i], axis=-1)
    grow = gcol.reshape(b, nc, t, 2 * heads).transpose(0, 1, 3, 2)
    ch = (lambda c: nc - 1 - c) if rev else (lambda c: c)
    cidx = lambda bi, c: (bi, ch(c), 0)
    pidx = lambda bi, c: (bi, jnp.maximum(ch(c) * (t // 8) - 1, 0), 0)
    nidx = lambda bi, c: (bi, jnp.minimum((ch(c) + 1) * (t // 8), nb8 - 1), 0)
    ridx = lambda bi, c: (bi, ch(c), 0, 0)
    sidx = lambda bi, c: (bi, 0, 0, 0)
    c0, n0, m0 = state
    seq = pl.BlockSpec((1, t, w), cidx)
    prv = pl.BlockSpec((1, 8, w), pidx)
    nxt = pl.BlockSpec((1, 8, w), nidx)
    st_specs = [pl.BlockSpec((1, heads, hd, hd), sidx), pl.BlockSpec((1, heads, 1, hd), sidx),
                pl.BlockSpec((1, heads, 8, 128), sidx)]
    cw = conv_w.reshape(6, w)
    in_specs = [seq, prv, nxt, seq, prv, nxt, seq, pl.BlockSpec((6, w), lambda bi, c: (0, 0)),
                pl.BlockSpec((1, t, 2 * heads), cidx), pl.BlockSpec((1, 1, 2 * heads, t), ridx)] + st_specs
    args = [pq, pq, pq, pk, pk, pk, v, cw, gcol, grow, c0, n0, m0]
    if acc is not None:
        in_specs.append(seq)
        args.append(acc)
    out = pl.pallas_call(
        functools.partial(_mlstm_kernel, rev=rev, heads=heads, hd=hd, nc=nc, has_acc=acc is not None),
        grid=(b, nc),
        in_specs=in_specs,
        out_specs=[seq] + st_specs,
        out_shape=[jax.ShapeDtypeStruct((b, l, w), F32), jax.ShapeDtypeStruct(c0.shape, F32),
                   jax.ShapeDtypeStruct(n0.shape, F32), jax.ShapeDtypeStruct(m0.shape, F32)],
        compiler_params=_params(("parallel", "arbitrary")),
        name="mlstm_rev" if rev else "mlstm_fwd",
    )(*args)
    return out[0], (out[1], out[2], out[3])


def _rwkv_core(rows, s_ref, rev):
    c, w = rows[0][0].shape
    g_w = RK_GROUP
    nh = g_w // RK_HD
    assert c == RK_HD and c & (c - 1) == 0
    sh = c.bit_length() - 1
    rj = lax.broadcasted_iota(jnp.int32, (g_w, g_w), 0)
    cj = lax.broadcasted_iota(jnp.int32, (g_w, g_w), 1)
    blk = (rj >> sh) == (cj >> sh)
    tt, ss = rj & (c - 1), cj & (c - 1)
    incl = blk & ((ss >= tt) if rev else (ss <= tt))
    strict = blk & ((ss > tt) if rev else (ss < tt))
    eye = (rj == cj).astype(F32)
    tri = incl[:c, :c].astype(F32).astype(MXU_DT)
    lo_half = lax.broadcasted_iota(jnp.int32, (2 * g_w, 2 * c), 1) < c
    last = 0 if rev else c - 1
    zero = jnp.zeros((), F32)
    chains = [(bi, g) for bi in range(len(rows)) for g in range(w // g_w)]

    def bd(x):
        return _mx(jnp.where(blk, jnp.concatenate([x] * nh, axis=0), zero))

    def rsum(z):
        out = z[0:c]
        for i in range(1, nh):
            out = out + z[i * c:(i + 1) * c]
        return out

    a_bd, r_bd, v_bd, bk_t, r_t, v_g, bk_hat, dec = {}, {}, {}, {}, {}, {}, {}, {}
    for bi, (r, lw, kd, v, kk, a) in enumerate(rows):
        ka = kk * a
        gi = _cumsum_mm(tri, lw)
        tot = gi[last:last + 1, :]
        e_neg = jnp.exp(-gi)
        e_tail = jnp.exp(tot - gi)
        full = {'a': -kk * jnp.exp(gi - lw), 'b': ka * e_neg, 'k': kd * e_neg, 'r': r * jnp.exp(gi),
                'bh': ka * e_tail, 'kh': kd * e_tail, 'dec': jnp.exp(tot), 'v': v}
        for g in range(w // g_w):
            x = {n: t[:, g * g_w:(g + 1) * g_w] for n, t in full.items()}
            ch = (bi, g)
            a_bd[ch], r_bd[ch], v_bd[ch] = bd(x['a']), bd(x['r']), bd(x['v'])
            bk_t[ch] = _mx(jnp.concatenate([x['b'], x['k']], axis=0))
            bk_hat[ch] = _mx(jnp.concatenate([x['bh'], x['kh']], axis=0))
            r_t[ch], v_g[ch], dec[ch] = _mx(x['r']), x['v'], x['dec']

    l_ab, a_ak, a_rb, a_rk = {}, {}, {}, {}
    for ch in chains:
        prod = _nt(jnp.concatenate([a_bd[ch], r_bd[ch]], axis=0), bk_t[ch])
        swp = pltpu.roll(prod, c, 1)
        xb = jnp.where(lo_half, prod, swp)
        xk = jnp.where(lo_half, swp, prod)
        xb = jnp.concatenate([xb, xb], axis=1)
        xk = jnp.concatenate([xk, xk], axis=1)
        l_ab[ch] = jnp.where(strict, xb[:g_w], zero)
        a_ak[ch] = _mx(jnp.where(strict, xk[:g_w], zero))
        a_rb[ch] = _mx(rsum(jnp.where(incl, xb[g_w:], zero)))
        a_rk[ch] = _mx(rsum(jnp.where(incl, xk[g_w:], zero)))

    xs = {ch: _mx(l_ab[ch]) for ch in chains}
    ps = {ch: eye + l_ab[ch] for ch in chains}
    for _ in range(max(1, (c - 1).bit_length() - 1)):
        xs = {ch: _mx(_nn(xs[ch], xs[ch])) for ch in chains}
        ps = {ch: ps[ch] + _nn(_mx(ps[ch]), xs[ch]) for ch in chains}

    t_cat = {ch: _mx(rsum(ps[ch])) for ch in chains}
    w_cat = {ch: _mx(_nn(t_cat[ch], a_bd[ch])) for ch in chains}
    ta = {ch: _mx(_nn(t_cat[ch], a_ak[ch])) for ch in chains}
    u0 = {ch: _nn(ta[ch], v_bd[ch]) for ch in chains}
    y_k = {ch: _nn(a_rk[ch], v_bd[ch]) for ch in chains}
    s_mem = {ch: s_ref[ch[0], ch[1]] for ch in chains}
    s_b = {ch: _mx(s_mem[ch]) for ch in chains}
    u = {ch: _nt(w_cat[ch], s_b[ch]) + u0[ch] for ch in chains}
    y = {ch: _nt(r_t[ch], s_b[ch]) + _nn(a_rb[ch], bd(u[ch])) + y_k[ch] for ch in chains}
    for ch in chains:
        upd = _tn(_mx(jnp.concatenate([u[ch], v_g[ch]], axis=0)), bk_hat[ch])
        s_ref[ch[0], ch[1]] = s_mem[ch] * dec[ch] + jnp.where(blk, upd, zero)
    return [jnp.concatenate([y[(bi, g)] for g in range(w // g_w)], axis=1) for bi in range(len(rows))]


def _rwkv_seq_kernel(*refs, rev, has_acc):
    r_ref, lw_ref, kd_ref, v_ref, kk_ref, a_ref, s0_ref = refs[:7]
    acc_ref = refs[7] if has_acc else None
    y_ref, s_ref = refs[7 + has_acc:]

    @pl.when(pl.program_id(1) == 0)
    def _():
        s_ref[...] = s0_ref[...]

    y, = _rwkv_core([(r_ref[0], lw_ref[0], kd_ref[0], v_ref[0], kk_ref[0], a_ref[0])], s_ref, rev)
    if has_acc:
        y = y + acc_ref[0]
    y_ref[0] = y


def _rwkv_scan_seq(r, lw, kd, v, kk, a, state, rev, acc=None):
    b, l, w = r.shape
    c = RK_CHUNK
    nc = l // c
    cidx = (lambda bi, ch: (bi, nc - 1 - ch, 0)) if rev else (lambda bi, ch: (bi, ch, 0))
    seq = pl.BlockSpec((1, c, w), cidx)
    st = pl.BlockSpec((1,) + state.shape[1:], lambda bi, ch: (bi, 0, 0, 0))
    args = [r, lw, kd, v, kk, a, state] + ([acc] if acc is not None else [])
    y, s = pl.pallas_call(
        functools.partial(_rwkv_seq_kernel, rev=rev, has_acc=acc is not None),
        grid=(b, nc),
        in_specs=[seq] * 6 + [st] + ([seq] if acc is not None else []),
        out_specs=[seq, st],
        out_shape=[jax.ShapeDtypeStruct((b, l, w), F32), jax.ShapeDtypeStruct(state.shape, F32)],
        compiler_params=_params(("parallel", "arbitrary")),
        name="rwkv_seq_rev" if rev else "rwkv_seq_fwd",
    )(*args)
    return y, s


_PV_ROWS = ('mu_r', 'mu_k', 'mu_v', 'w0', 'a0', 'kk', 'ka', 'rk', 'v0')


def _rwkv_grid_kernel(*refs, rev, nc, has_vf, has_acc, emit_vf):
    it = iter(refs)
    pr, pr_u, pr_d, pk, pk_u, pk_d, pv, pv_u, pv_d, wa, wa_u, wa_d = (next(it) for _ in range(12))
    vf_ref, lora_ref = (next(it), next(it)) if has_vf else (None, None)
    pvec, mu_wa, w2_ref, a2_ref, s0_ref = (next(it) for _ in range(5))
    accy_ref, accb_ref = (next(it), next(it)) if has_acc else (None, None)
    y_ref, bon_ref, s_ref = (next(it) for _ in range(3))
    vfo_ref = next(it) if emit_vf else None
    step = pl.program_id(1)

    @pl.when(step == 0)
    def _():
        s_ref[...] = s0_ref[...]

    c = pr.shape[1]
    w = pr.shape[2]
    rd = w // 16
    ci = nc - 1 - step if rev else step
    has_up = jnp.where(ci > 0, 1.0, 0.0)
    has_dn = jnp.where(ci < nc - 1, 1.0, 0.0)
    trow = lax.broadcasted_iota(jnp.int32, (c, 1), 0)
    prow = lambda name: pvec[_PV_ROWS.index(name):_PV_ROWS.index(name) + 1, :]

    def left(x):
        return jnp.where(trow == 0, 0.0, pltpu.roll(x, 1, 0))

    def right(x):
        return jnp.where(trow == c - 1, 0.0, pltpu.roll(x, c - 1, 0))

    ones = _seg_ones(RK_HD)
    d = 1 if rev else 0
    rows, bons = [], []
    for bi in range(pr.shape[0]):
        def lerp_wide(cur_ref, up_ref, dn_ref, mu):
            cur = cur_ref[bi]
            q = w // 4
            sh = jnp.concatenate([left(cur[:, :q]), right(cur[:, q:2 * q]),
                                  up_ref[bi] * has_up, dn_ref[bi] * has_dn], axis=1)
            return cur + mu * (sh - cur)

        r = lerp_wide(pr, pr_u, pr_d, prow('mu_r'))
        k = lerp_wide(pk, pk_u, pk_d, prow('mu_k'))
        v = lerp_wide(pv, pv_u, pv_d, prow('mu_v'))

        cur = wa[bi]
        qd = (lax.broadcasted_iota(jnp.int32, cur.shape, 1) & (rd - 1)) >> ((rd // 4).bit_length() - 1)
        sh = jnp.where(qd == 0, left(cur), jnp.where(qd == 1, right(cur),
                                                     jnp.where(qd == 2, wa_u[bi] * has_up, wa_d[bi] * has_dn)))
        xwa = cur + mu_wa[...] * (sh - cur)
        xw = xwa[:, d * rd:(d + 1) * rd]
        xa = xwa[:, (2 + d) * rd:(3 + d) * rd]
        z = -(prow('w0') + _nn(_mx(jnp.tanh(xw)), w2_ref[...]))
        softplus = jnp.maximum(z, 0.0) + jnp.log(1.0 + jnp.exp(-jnp.abs(z)))
        lw = -jnp.exp(-softplus - 0.5)
        a = _sigmoid(prow('a0') + _nn(_mx(xa), a2_ref[...]))
        if has_vf:
            v = v + (vf_ref[bi] - v) * _sigmoid(prow('v0') + lora_ref[bi])
        if emit_vf:
            vfo_ref[bi] = v
        kkn = k * prow('kk')
        kk = kkn / jnp.maximum(jnp.sqrt(_segsum(kkn * kkn, ones)), 1e-12)
        kd = k * (1.0 + (a - 1.0) * prow('ka'))
        bons.append(_segsum(r * kd * prow('rk'), ones) * v)
        rows.append((r, lw, kd, v, kk, a))
    ys = _rwkv_core(rows, s_ref, rev)
    for bi, (y, bon) in enumerate(zip(ys, bons)):
        if has_acc:
            y = y + accy_ref[bi]
            bon = bon + accb_ref[bi]
        y_ref[bi] = y
        bon_ref[bi] = bon


def _rwkv_scan_grid(pr, pk, pv, wa, pvec, mu_wa, w2, a2, state, rev, vf=None, lora=None,
                    acc=None, emit_vf=False):
    b, l, w = pr.shape
    c = RK_CHUNK
    assert c == GRID_W
    nc = l // c
    wq = w // 4
    ch = (lambda s: nc - 1 - s) if rev else (lambda s: s)
    cur = lambda bi, s: (bi, ch(s), 0)
    upi = lambda lane: (lambda bi, s: (bi, jnp.maximum(ch(s) - 1, 0), lane))
    dni = lambda lane: (lambda bi, s: (bi, jnp.minimum(ch(s) + 1, nc - 1), lane))
    bb = RK_BATCH if b % RK_BATCH == 0 else 1
    seq = pl.BlockSpec((bb, c, w), cur)
    wide = [seq, pl.BlockSpec((bb, c, wq), upi(2)), pl.BlockSpec((bb, c, wq), dni(3))]
    wa_w = wa.shape[2]
    small = [pl.BlockSpec((bb, c, wa_w), cur), pl.BlockSpec((bb, c, wa_w), upi(0)), pl.BlockSpec((bb, c, wa_w), dni(0))]
    full = lambda x: pl.BlockSpec(x.shape, lambda bi, s: (0,) * x.ndim)
    st = pl.BlockSpec((bb,) + state.shape[1:], lambda bi, s: (bi, 0, 0, 0))
    in_specs = wide * 3 + small
    args = [pr, pr, pr, pk, pk, pk, pv, pv, pv, wa, wa, wa]
    if vf is not None:
        in_specs += [seq, seq]
        args += [vf, lora]
    in_specs += [full(pvec), full(mu_wa), full(w2), full(a2), st]
    args += [pvec, mu_wa, w2, a2, state]
    if acc is not None:
        in_specs += [seq, seq]
        args += list(acc)
    out_specs = [seq, seq, st]
    out_shape = [jax.ShapeDtypeStruct((b, l, w), F32), jax.ShapeDtypeStruct((b, l, w), F32),
                 jax.ShapeDtypeStruct(state.shape, F32)]
    if emit_vf:
        out_specs.append(seq)
        out_shape.append(jax.ShapeDtypeStruct((b, l, w), F32))
    return pl.pallas_call(
        functools.partial(_rwkv_grid_kernel, rev=rev, nc=nc, has_vf=vf is not None,
                          has_acc=acc is not None, emit_vf=emit_vf),
        grid=(b // bb, nc),
        in_specs=in_specs,
        out_specs=out_specs,
        out_shape=out_shape,
        compiler_params=_params(("parallel", "arbitrary")),
        name="rwkv_grid_rev" if rev else "rwkv_grid_fwd",
    )(*args)


def _hgrn_kernel(*refs, rev, heads, hd, has_acc):
    q_ref, f_ref, v_ref, fb_ref, lb_ref, s0_ref = refs[:6]
    acc_ref = refs[6] if has_acc else None
    o_ref, s_ref = refs[6 + has_acc:]

    @pl.when(pl.program_id(1) == 0)
    def _():
        s_ref[...] = s0_ref[...]

    tt = q_ref.shape[1]
    sub = HG_SUB
    sh = sub.bit_length() - 1
    nsub = tt // sub
    mid = sub // 2 if rev else sub // 2 - 1
    last = 0 if rev else sub - 1
    ri = lax.broadcasted_iota(jnp.int32, (tt, tt), 0)
    ci = lax.broadcasted_iota(jnp.int32, (tt, tt), 1)
    same = (ri >> sh) == (ci >> sh)
    base = (ri >> sh) << sh

    def upto(r):
        return same & ((ci >= r) if rev else (ci <= r))

    causal = upto(ri)
    as_mx = lambda m: m.astype(F32).astype(MXU_DT)
    lb = lb_ref[...]
    f = lb + (1.0 - lb) * _sigmoid(f_ref[0] + fb_ref[...])
    g_all, g_mid, g_last = _cumsum_mm([as_mx(causal), as_mx(upto(base + mid)), as_mx(upto(base + last))],
                                      jnp.log(f))
    k_all = 1.0 - f
    q_all = _silu(q_ref[0])
    q1 = _mx(q_all * jnp.exp(g_all - g_mid))
    k1 = _mx(k_all * jnp.exp(g_mid - g_all))
    qg = _mx(q_all * jnp.exp(g_all))
    kl = _mx(k_all * jnp.exp(g_last - g_all))
    v_all = _mx(v_ref[0])
    order = [nsub - 1 - s if rev else s for s in range(nsub)]
    rows = [slice(j * sub, (j + 1) * sub) for j in range(nsub)]
    lanes = [slice(h * hd, (h + 1) * hd) for h in range(heads)]
    zero = jnp.zeros((), F32)
    att = [_mx(jnp.where(causal, _nt(q1[:, sl], k1[:, sl]), zero)) for sl in lanes]
    intra = [_nn(att[h], v_all[:, lanes[h]]) for h in range(heads)]
    upd = {(j, h): _tn(v_all[rows[j], lanes[h]], kl[rows[j], lanes[h]]) for j in order for h in range(heads)}
    s_mem = [s_ref[0, h] for h in range(heads)]
    for j in order:
        dec = jnp.exp(g_last[j * sub:j * sub + 1])
        outs = []
        for h in range(heads):
            outs.append(intra[h][rows[j]] + _nt(qg[rows[j], lanes[h]], _mx(s_mem[h])))
            s_mem[h] = s_mem[h] * dec[:, lanes[h]] + upd[(j, h)]
        o = jnp.concatenate(outs, axis=1)
        if has_acc:
            o = o + acc_ref[0, rows[j], :]
        o_ref[0, rows[j], :] = o
    for h in range(heads):
        s_ref[0, h] = s_mem[h]


def _hgrn_scan(pq, pf, pi, f_b, lb, state, rev, acc=None):
    b, l, w = pq.shape
    heads = HG_HEADS
    hd = w // heads
    tt = min(HG_TILE, l)
    nc = l // tt
    d = 1 if rev else 0
    ch = (lambda s: nc - 1 - s) if rev else (lambda s: s)
    seq = pl.BlockSpec((1, tt, w), lambda bi, s: (bi, ch(s), 0))
    fseq = pl.BlockSpec((1, tt, w), lambda bi, s: (bi, ch(s), d))
    vec = pl.BlockSpec((1, w), lambda bi, s: (0, 0))
    st = pl.BlockSpec((1,) + state.shape[1:], lambda bi, s: (bi, 0, 0, 0))
    args = [pq, pf, pi, f_b, lb, state] + ([acc] if acc is not None else [])
    o, s = pl.pallas_call(
        functools.partial(_hgrn_kernel, rev=rev, heads=heads, hd=hd, has_acc=acc is not None),
        grid=(b, nc),
        in_specs=[seq, fseq, seq, vec, vec, st] + ([seq] if acc is not None else []),
        out_specs=[seq, st],
        out_shape=[jax.ShapeDtypeStruct((b, l, w), F32), jax.ShapeDtypeStruct(state.shape, F32)],
        compiler_params=_params(("parallel", "arbitrary")),
        name="hgrn_rev" if rev else "hgrn_fwd",
    )(*args)
    return o, s


_EV_ROWS = ('ml_norm_w', 'rk_ln_w', 'rk_ln_b', 'hg_norm_w', 'gate_b0', 'gate_b1', 'gate_b2', 'final_w')


def _epilogue_kernel(hm_ref, mo_ref, mz_ref, yr_ref, bon_ref, rz_ref, oh_ref, hz_ref,
                     g0_ref, g1_ref, g2_ref, xs_ref, gx_ref, ev_ref,
                     wpm_ref, wpr_ref, wph_ref, wout_ref, o_ref, *, final):
    ev = lambda name: ev_ref[_EV_ROWS.index(name):_EV_ROWS.index(name) + 1, :]
    d = hm_ref.shape[2]

    def seg_mean(x, seg):
        return _segsum(x, _seg_ones(seg)) * (1.0 / seg)

    seg = d // ML_HEADS
    y = hm_ref[0]
    y = y - seg_mean(y, seg)
    y = y * lax.rsqrt(seg_mean(y * y, seg) + NORM_EPS) * ev('ml_norm_w')
    u_m = _sigmoid(mo_ref[0]) * y * _silu(mz_ref[0])
    y = yr_ref[0]
    y = y - seg_mean(y, RK_HD)
    y = y * lax.rsqrt(seg_mean(y * y, RK_HD) + RK_GN_EPS) * ev('rk_ln_w') + ev('rk_ln_b') + bon_ref[0]
    u_r = y * _silu(rz_ref[0])
    seg = d // HG_HEADS
    o = oh_ref[0]
    o = o * lax.rsqrt(seg_mean(o * o, seg) + NORM_EPS) * ev('hg_norm_w')
    u_h = o * _silu(hz_ref[0])

    merged = (_sigmoid(g0_ref[0] + ev('gate_b0')) * _nn(_mx(u_m), wpm_ref[...])
              + _sigmoid(g1_ref[0] + ev('gate_b1')) * _nn(_mx(u_r), wpr_ref[...])
              + _sigmoid(g2_ref[0] + ev('gate_b2')) * _nn(_mx(u_h), wph_ref[...]))
    xs = xs_ref[0] + gx_ref[0] * _nn(_mx(merged), wout_ref[...])
    if final:
        xs = xs * lax.rsqrt(jnp.mean(xs * xs, axis=-1, keepdims=True) + NORM_EPS) * ev('final_w')
    o_ref[0] = xs


def _epilogue(hm, mo, mz, yr, bon, rz, oh, hz, gate, xs, gx, p, final_w):
    b, l, d = xs.shape
    tt = min(EP_TILE, l)
    seq = pl.BlockSpec((1, tt, d), lambda bi, i: (bi, i, 0))
    gseq = lambda lane: pl.BlockSpec((1, tt, d), lambda bi, i: (bi, i, lane))
    ev = jnp.stack([p['ml_norm_w'], p['rk_ln_w'], p['rk_ln_b'], p['hg_norm_w'],
                    p['gate_b'][0], p['gate_b'][1], p['gate_b'][2],
                    final_w if final_w is not None else jnp.ones((d,), F32)], axis=0)
    wspec = pl.BlockSpec((d, d), lambda bi, i: (0, 0))
    ws = [_mx(p[n]) for n in ('w_pm', 'w_pr', 'w_ph', 'w_out')]
    return pl.pallas_call(
        functools.partial(_epilogue_kernel, final=final_w is not None),
        grid=(b, l // tt),
        in_specs=[seq] * 8 + [gseq(0), gseq(1), gseq(2), seq,
                              pl.BlockSpec((1, 1, d), lambda bi, i: (bi, 0, 0)),
                              pl.BlockSpec(ev.shape, lambda bi, i: (0, 0))] + [wspec] * 4,
        out_specs=seq,
        out_shape=jax.ShapeDtypeStruct((b, l, d), F32),
        compiler_params=_params(("parallel", "parallel")),
        name="epilogue",
    )(hm, mo, mz, yr, bon, rz, oh, hz, gate, gate, gate, xs, gx, ev, *ws)


def _rms_norm(x, w):
    return x * lax.rsqrt(jnp.mean(x * x, axis=-1, keepdims=True) + NORM_EPS) * w


def _bi_shift_seq(u):
    half = u.shape[-1] // 2
    g = jnp.pad(u, ((0, 0), (1, 1), (0, 0)))
    return jnp.concatenate((g[:, :-2, :half], g[:, 2:, half:]), axis=-1)


def _in_layout(d):
    rd = d // 16
    return (('m_q', d), ('m_k', d), ('m_v', d), ('m_o', d), ('m_z', d), ('m_if', 4 * ML_HEADS),
            ('r_r', d), ('r_k', d), ('r_v', d), ('r_z', d), ('r_wa', 4 * rd),
            ('h_q', d), ('h_f', 2 * d), ('h_i', d), ('h_z', d), ('gate', 3 * d))


def _split_w_in(w_in):
    out, off = {}, 0
    for name, width in _in_layout(w_in.shape[0]):
        out[name] = w_in[:, off:off + width].astype(MXU_DT)
        off += width
    return out


def _zero_states(b, d):
    ml_hd = d // ML_HEADS
    ml = (jnp.zeros((b, ML_HEADS, ml_hd, ml_hd), F32), jnp.zeros((b, ML_HEADS, 1, ml_hd), F32),
          jnp.zeros((b, ML_HEADS, 8, 128), F32))
    rk = jnp.zeros((b, d // RK_GROUP, RK_GROUP, RK_GROUP), F32)
    hg_hd = d // HG_HEADS
    hg = jnp.zeros((b, HG_HEADS, hg_hd, hg_hd), F32)
    return ((ml, ml), (rk, rk), (hg, hg))


def _mlstm_branch(proj, p, init, b, l):
    gl = proj('m_if').reshape(b, l, 2, 2, ML_HEADS) + p['ml_if_b']
    acc, states = None, []
    for dr in (0, 1):
        log_i = gl[:, :, dr, 0]
        log_f = jax.nn.log_sigmoid(gl[:, :, dr, 1])
        acc, st = _mlstm_scan(proj('m_q'), proj('m_k'), proj('m_v'), p['ml_conv'], log_i, log_f,
                              init[dr], rev=bool(dr), acc=acc)
        states.append(st)
    return acc, tuple(states)


def _rwkv7_seq_branch(proj, h2, p, init, v_first, b, l, d):
    nh, n = d // RK_HD, RK_HD
    rd = d // 16
    mu = p['rk_mu']

    def lerp_shift(u, m):
        return u + m * (_bi_shift_seq(u) - u)

    r = lerp_shift(proj('r_r'), mu[0:d])
    k = lerp_shift(proj('r_k'), mu[d:2 * d])
    v = lerp_shift(proj('r_v'), mu[2 * d:3 * d])
    pwa = proj('r_wa')
    xwa = jnp.concatenate([lerp_shift(pwa[..., i * rd:(i + 1) * rd], mu[3 * d + i * rd:3 * d + (i + 1) * rd])
                           for i in range(4)], axis=-1)
    if p['rk_v0'] is None:
        v_first = v
    else:
        lora = _mm(_mm(h2, p['rk_v1']), p['rk_v2']).reshape(b, l, d)
        v = v + (v_first - v) * jax.nn.sigmoid(p['rk_v0'] + lora)
    kk = (k * p['rk_kk']).reshape(b, l, nh, n)
    kk = kk / jnp.maximum(jnp.sqrt(jnp.sum(kk * kk, axis=-1, keepdims=True)), 1e-12)
    kk = kk.reshape(b, l, d)
    acc, states, kd_sum = None, [], 0.0
    for dr in (0, 1):
        xw = xwa[..., dr * rd:(dr + 1) * rd]
        xa = xwa[..., (2 + dr) * rd:(3 + dr) * rd]
        log_w = -jax.nn.softplus(-(p['rk_w0'][dr] + _mm(jnp.tanh(xw).reshape(b * l, rd), p['rk_w2'][dr]).reshape(b, l, d))) - 0.5
        lw = -jnp.exp(log_w)
        a = jax.nn.sigmoid(p['rk_a0'][dr] + _mm(xa.reshape(b * l, rd), p['rk_a2'][dr]).reshape(b, l, d))
        kd = k * (1.0 + (a - 1.0) * p['rk_ka'])
        acc, s = _rwkv_scan_seq(r, lw, kd, v, kk, a, init[dr], rev=bool(dr), acc=acc)
        states.append(s)
        kd_sum = kd_sum + kd
    bonus = jnp.sum((r * kd_sum * p['rk_rk']).reshape(b, l, nh, n), axis=-1, keepdims=True) * v.reshape(b, l, nh, n)
    return acc, bonus.reshape(b, l, d), tuple(states), v_first


def _rwkv7_grid_branch(proj, h2, p, init, v_first, b, l, d):
    mu = p['rk_mu']
    zeros = jnp.zeros((d,), F32)
    has_vf = p['rk_v0'] is not None
    lora = _mm(_mm(h2, p['rk_v1']), p['rk_v2']).reshape(b, l, d) if has_vf else None
    acc, states, vf_out = None, [], v_first
    for dr in (0, 1):
        rows = {'mu_r': mu[0:d], 'mu_k': mu[d:2 * d], 'mu_v': mu[2 * d:3 * d], 'w0': p['rk_w0'][dr],
                'a0': p['rk_a0'][dr], 'kk': p['rk_kk'], 'ka': p['rk_ka'], 'rk': p['rk_rk'],
                'v0': p['rk_v0'] if has_vf else zeros}
        pvec = jnp.stack([rows[n] for n in _PV_ROWS], axis=0)
        emit_vf = (not has_vf) and dr == 0
        out = _rwkv_scan_grid(proj('r_r'), proj('r_k'), proj('r_v'), proj('r_wa'), pvec, mu[3 * d:][None],
                              _mx(p['rk_w2'][dr]), _mx(p['rk_a2'][dr]), init[dr], rev=bool(dr),
                              vf=v_first if has_vf else None, lora=lora, acc=acc, emit_vf=emit_vf)
        acc = (out[0], out[1])
        states.append(out[2])
        if emit_vf:
            vf_out = out[3]
    return acc[0], acc[1], tuple(states), vf_out


def _hgrn2_branch(proj, p, init):
    acc, states = None, []
    for dr in (0, 1):
        acc, s = _hgrn_scan(proj('h_q'), proj('h_f'), proj('h_i'), p['hg_f_b'][dr][None], p['hg_lb'][dr][None],
                            init[dr], rev=bool(dr), acc=acc)
        states.append(s)
    return acc, tuple(states)


def _mixer(h, p, init, on_grid, v_first, need_out, xs, gx, final_w):
    b, l, d = h.shape
    h2 = _mx(h.reshape(b * l, d))
    cache = {}

    def proj(name):
        if name not in cache:
            w = p['w_in'][name]
            cache[name] = _mm(h2, w).reshape(b, l, w.shape[1])
        return cache[name]

    hm, st_m = _mlstm_branch(proj, p, init[0], b, l)
    if on_grid:
        yr, bon, st_r, v_first = _rwkv7_grid_branch(proj, h2, p, init[1], v_first, b, l, d)
    else:
        yr, bon, st_r, v_first = _rwkv7_seq_branch(proj, h2, p, init[1], v_first, b, l, d)
    oh, st_h = _hgrn2_branch(proj, p, init[2])
    states = (st_m, st_r, st_h)
    if not need_out:
        return None, states, v_first
    out = _epilogue(hm, proj('m_o'), proj('m_z'), yr, bon, proj('r_z'), oh, proj('h_z'), proj('gate'),
                    xs, gx, p, final_w)
    return out, states, v_first


def kernel(x, c, ctx, c_ctx, norm_w, ada_w, ada_b, w_in, gate_b, ml_conv, ml_if_b, ml_norm_w, rk_mu, rk_w0, rk_w2, rk_a0, rk_a2, rk_kk, rk_ka, rk_rk, rk_v0, rk_v1, rk_v2, rk_ln_w, rk_ln_b, hg_f_b, hg_lb, hg_norm_w, w_pm, w_pr, w_ph, w_out, final_norm_w):
    batch, _, d = x.shape
    depth = w_in.shape[0]
    lb_p = jax.nn.softmax(hg_lb.astype(F32), axis=1)
    lower_bounds = jnp.cumsum(lb_p, axis=1) - lb_p[:, :1]
    xs, cs = x, ctx
    vf_x, vf_c = None, None
    cond = jnp.concatenate([jax.nn.silu(c), jax.nn.silu(c_ctx)[None]], axis=0)
    for l in range(depth):
        last = l == depth - 1
        p = {'w_in': _split_w_in(w_in[l]), 'gate_b': gate_b[l], 'ml_conv': ml_conv[l], 'ml_if_b': ml_if_b[l],
             'ml_norm_w': ml_norm_w[l], 'rk_mu': rk_mu[l], 'rk_w0': rk_w0[l], 'rk_w2': rk_w2[l],
             'rk_a0': rk_a0[l], 'rk_a2': rk_a2[l], 'rk_kk': rk_kk[l], 'rk_ka': rk_ka[l], 'rk_rk': rk_rk[l],
             'rk_v0': rk_v0[l - 1] if l > 0 else None, 'rk_v1': rk_v1[l - 1] if l > 0 else None,
             'rk_v2': rk_v2[l - 1] if l > 0 else None, 'rk_ln_w': rk_ln_w[l], 'rk_ln_b': rk_ln_b[l],
             'hg_f_b': hg_f_b[l], 'hg_lb': lower_bounds[:, l], 'hg_norm_w': hg_norm_w[l],
             'w_pm': w_pm[l], 'w_pr': w_pr[l], 'w_ph': w_ph[l], 'w_out': w_out[l]}
        mod = _mm(cond, ada_w[l]) + ada_b[l]
        shift_x, scale_x, gate_x = jnp.split(mod[:batch, None, :], 3, axis=-1)
        shift_c, scale_c, gate_c = jnp.split(mod[batch], 3, axis=-1)
        hc = _rms_norm(cs, norm_w[l]) * (1.0 + scale_c) + shift_c
        gc = jnp.broadcast_to(gate_c[None, None, :], (batch, 1, d))
        cs_new, st_c, vf_c = _mixer(hc, p, _zero_states(batch, d), False, vf_c, not last, cs, gc, None)
        hx = _rms_norm(xs, norm_w[l]) * (1.0 + scale_x) + shift_x
        xs, _, vf_x = _mixer(hx, p, st_c, True, vf_x, True, xs, gate_x, final_norm_w if last else None)
        if not last:
            cs = cs_new
    return xs
```

```python
import functools

import jax
import jax.numpy as jnp
from jax import lax
from jax.experimental import pallas as pl
from jax.experimental.pallas import tpu as pltpu

F32 = jnp.float32
MXU_DT = jnp.bfloat16

NORM_EPS = 1e-6
GRID_W = 64
ML_HEADS = 4
ML_CHUNK = 256
RK_HD = 64
RK_CHUNK = 64
RK_GROUP = 256
RK_BATCH = 4
RK_GN_EPS = 64e-5
HG_HEADS = 8
HG_SUB = 16
HG_TILE = 128
EP_TILE = 256
SEG_W = 256
VMEM_LIMIT = 56 * 1024 * 1024


def _nt(a, b):
    return lax.dot_general(a, b, (((1,), (1,)), ((), ())), preferred_element_type=F32)


def _tn(a, b):
    return lax.dot_general(a, b, (((0,), (0,)), ((), ())), preferred_element_type=F32)


def _nn(a, b):
    return jnp.dot(a, b, preferred_element_type=F32)


def _mx(a):
    return a.astype(MXU_DT)


def _split2(x):
    hi = x.astype(MXU_DT)
    return hi, (x - hi.astype(F32)).astype(MXU_DT)


def _cumsum_mm(tri, x):
    hi = x.astype(MXU_DT)
    r1 = x - hi.astype(F32)
    mid = r1.astype(MXU_DT)
    lo = (r1 - mid.astype(F32)).astype(MXU_DT)
    one = lambda t: _nn(t, hi) + _nn(t, mid) + _nn(t, lo)
    return [one(t) for t in tri] if isinstance(tri, (list, tuple)) else one(tri)


def _seg_ones(seg):
    sh = seg.bit_length() - 1
    ri = lax.broadcasted_iota(jnp.int32, (SEG_W, SEG_W), 0)
    ci = lax.broadcasted_iota(jnp.int32, (SEG_W, SEG_W), 1)
    return ((ri >> sh) == (ci >> sh)).astype(F32).astype(MXU_DT)


def _segsum(x, ones):
    outs = []
    for g in range(x.shape[1] // SEG_W):
        hi, lo = _split2(x[:, g * SEG_W:(g + 1) * SEG_W])
        outs.append(_nn(hi, ones) + _nn(lo, ones))
    return outs[0] if len(outs) == 1 else jnp.concatenate(outs, axis=1)


def _sigmoid(x):
    return jax.nn.sigmoid(x)


def _silu(x):
    return x * jax.nn.sigmoid(x)


def _params(sem):
    return pltpu.CompilerParams(dimension_semantics=sem, vmem_limit_bytes=VMEM_LIMIT)


def _mm_kernel(x_ref, w_ref, o_ref):
    o_ref[...] = _nn(_mx(x_ref[...]), w_ref[...]).astype(o_ref.dtype)


def _mm(x, w, out_dtype=F32):
    m, k = x.shape
    n = w.shape[1]
    mp = -(-m // 8) * 8
    npad = -(-n // 128) * 128
    if mp != m:
        x = jnp.pad(x, ((0, mp - m), (0, 0)))
    wb = w.astype(MXU_DT)
    if npad != n:
        wb = jnp.pad(wb, ((0, 0), (0, npad - n)))
    tm = next(t for t in (1024, 512, 256, 128, 64, 32, 16, 8) if mp % t == 0)
    tn = next(t for t in (1024, 512, 256, 128) if npad % t == 0)
    out = pl.pallas_call(
        _mm_kernel,
        grid=(mp // tm, npad // tn),
        in_specs=[pl.BlockSpec((tm, k), lambda i, j: (i, 0)),
                  pl.BlockSpec((k, tn), lambda i, j: (0, j))],
        out_specs=pl.BlockSpec((tm, tn), lambda i, j: (i, j)),
        out_shape=jax.ShapeDtypeStruct((mp, npad), out_dtype),
        compiler_params=_params(("parallel", "parallel")),
        name="mm",
    )(x, wb)
    if mp != m or npad != n:
        out = out[:m, :n]
    return out


def _mlstm_kernel(*refs, rev, heads, hd, nc, has_acc):
    (q_ref, qp_ref, qn_ref, k_ref, kp_ref, kn_ref, v_ref, cw_ref, gcol_ref, grow_ref,
     c0_ref, n0_ref, m0_ref) = refs[:13]
    acc_ref = refs[13] if has_acc else None
    h_ref, c_ref, n_ref, m_ref = refs[13 + has_acc:]
    step = pl.program_id(1)

    @pl.when(step == 0)
    def _():
        c_ref[...] = c0_ref[...]
        n_ref[...] = n0_ref[...]
        m_ref[...] = m0_ref[...]

    t = q_ref.shape[1]
    ci = nc - 1 - step if rev else step
    has_prev = jnp.where(ci > 0, 1.0, 0.0)
    has_next = jnp.where(ci < nc - 1, 1.0, 0.0)
    trow = lax.broadcasted_iota(jnp.int32, (t, 1), 0)

    def conv_silu(u, prev_blk, next_blk, w3):
        up = jnp.where(trow == 0, prev_blk[7:8, :] * has_prev, pltpu.roll(u, 1, 0))
        dn = jnp.where(trow == t - 1, next_blk[0:1, :] * has_next, pltpu.roll(u, t - 1, 0))
        return _silu(up * w3[0:1, :] + u * w3[1:2, :] + dn * w3[2:3, :])

    q_all = conv_silu(q_ref[0], qp_ref[0], qn_ref[0], cw_ref[0:3, :]) * (hd ** -0.5)
    k_all = conv_silu(k_ref[0], kp_ref[0], kn_ref[0], cw_ref[3:6, :])

    row = lax.broadcasted_iota(jnp.int32, (t, t), 0)
    col = lax.broadcasted_iota(jnp.int32, (t, t), 1)
    mask = (col >= row) if rev else (col <= row)
    last = 0 if rev else t - 1
    hs = range(heads)
    lanes = [slice(h * hd, (h + 1) * hd) for h in hs]
    q32 = [q_all[:, sl] for sl in lanes]
    qc = [_mx(x) for x in q32]
    kc = [k_all[:, sl] for sl in lanes]
    vc = [_mx(v_ref[0, :, sl]) for sl in lanes]
    bc_col = [gcol_ref[0, :, h:h + 1] for h in hs]
    i_col = [gcol_ref[0, :, heads + h:heads + h + 1] for h in hs]
    m_prev = [m_ref[0, h, 0:1, 0:1] for h in hs]
    c_mem = [c_ref[0, h] for h in hs]
    n_mem = [n_ref[0, h] for h in hs]
    dmat = [jnp.where(mask, bc_col[h] - grow_ref[0, 0, h:h + 1, :] + grow_ref[0, 0, heads + h:heads + h + 1, :],
                      -jnp.inf) for h in hs]
    inter = [bc_col[h] + m_prev[h] for h in hs]
    m_t = [jnp.maximum(jnp.max(dmat[h], axis=-1, keepdims=True), inter[h]) for h in hs]
    qk = [_nt(qc[h], _mx(kc[h])) for h in hs]
    q_c = [_nn(qc[h], _mx(c_mem[h])) for h in hs]
    s = [qk[h] * jnp.exp(dmat[h] - m_t[h]) for h in hs]
    w_inter = [jnp.exp(inter[h] - m_t[h]) for h in hs]
    num = [_nn(_mx(s[h]), vc[h]) + w_inter[h] * q_c[h] for h in hs]
    for h in hs:
        qn = jnp.sum(q32[h] * n_mem[h], axis=-1, keepdims=True)
        den = jnp.sum(s[h], axis=-1, keepdims=True) + w_inter[h] * qn
        h_out = num[h] / jnp.maximum(jnp.abs(den), jnp.exp(-m_t[h]))
        if has_acc:
            h_out = h_out + acc_ref[0, :, lanes[h]]
        h_ref[0, :, lanes[h]] = h_out
    for h in hs:
        total = bc_col[h][last:last + 1, :]
        g_col = total - bc_col[h] + i_col[h]
        m_new = jnp.maximum(total + m_prev[h], jnp.max(g_col, axis=0, keepdims=True))
        wk = jnp.exp(g_col - m_new)
        dec = jnp.exp(total + m_prev[h] - m_new)
        kw = kc[h] * wk
        c_ref[0, h] = dec * c_mem[h] + _tn(_mx(kw), vc[h])
        n_ref[0, h] = dec * n_mem[h] + jnp.sum(kw, axis=0, keepdims=True)
        m_ref[0, h] = jnp.broadcast_to(m_new, m_ref.shape[2:])


def _mlstm_scan(pq, pk, v, conv_w, log_i, log_f, state, rev, acc=None):
    b, l, w = pq.shape
    heads = ML_HEADS
    hd = w // heads
    t = min(ML_CHUNK, l)
    assert l % t == 0
    nc = l // t
    nb8 = l // 8
    lf = log_f.reshape(b, nc, t, heads)
    if rev:
        bcum = jnp.flip(jnp.cumsum(jnp.flip(lf, 2), axis=2), 2)
    else:
        bcum = jnp.cumsum(lf, axis=2)
    gcol = jnp.concatenate([bcum.reshape(b, l, heads), log_i], axis=-1)
    grow = gcol.reshape(b, nc, t, 2 * heads).transpose(0, 1, 3, 2)
    ch = (lambda c: nc - 1 - c) if rev else (lambda c: c)
    cidx = lambda bi, c: (bi, ch(c), 0)
    pidx = lambda bi, c: (bi, jnp.maximum(ch(c) * (t // 8) - 1, 0), 0)
    nidx = lambda bi, c: (bi, jnp.minimum((ch(c) + 1) * (t // 8), nb8 - 1), 0)
    ridx = lambda bi, c: (bi, ch(c), 0, 0)
    sidx = lambda bi, c: (bi, 0, 0, 0)
    c0, n0, m0 = state
    seq = pl.BlockSpec((1, t, w), cidx)
    prv = pl.BlockSpec((1, 8, w), pidx)
    nxt = pl.BlockSpec((1, 8, w), nidx)
    st_specs = [pl.BlockSpec((1, heads, hd, hd), sidx), pl.BlockSpec((1, heads, 1, hd), sidx),
                pl.BlockSpec((1, heads, 8, 128), sidx)]
    cw = conv_w.reshape(6, w)
    in_specs = [seq, prv, nxt, seq, prv, nxt, seq, pl.BlockSpec((6, w), lambda bi, c: (0, 0)),
                pl.BlockSpec((1, t, 2 * heads), cidx), pl.BlockSpec((1, 1, 2 * heads, t), ridx)] + st_specs
    args = [pq, pq, pq, pk, pk, pk, v, cw, gcol, grow, c0, n0, m0]
    if acc is not None:
        in_specs.append(seq)
        args.append(acc)
    out = pl.pallas_call(
        functools.partial(_mlstm_kernel, rev=rev, heads=heads, hd=hd, nc=nc, has_acc=acc is not None),
        grid=(b, nc),
        in_specs=in_specs,
        out_specs=[seq] + st_specs,
        out_shape=[jax.ShapeDtypeStruct((b, l, w), F32), jax.ShapeDtypeStruct(c0.shape, F32),
                   jax.ShapeDtypeStruct(n0.shape, F32), jax.ShapeDtypeStruct(m0.shape, F32)],
        compiler_params=_params(("parallel", "arbitrary")),
        name="mlstm_rev" if rev else "mlstm_fwd",
    )(*args)
    return out[0], (out[1], out[2], out[3])


def _rwkv_core(rows, s_ref, rev):
    c, w = rows[0][0].shape
    g_w = RK_GROUP
    nh = g_w // RK_HD
    assert c == RK_HD and c & (c - 1) == 0
    sh = c.bit_length() - 1
    rj = lax.broadcasted_iota(jnp.int32, (g_w, g_w), 0)
    cj = lax.broadcasted_iota(jnp.int32, (g_w, g_w), 1)
    blk = (rj >> sh) == (cj >> sh)
    tt, ss = rj & (c - 1), cj & (c - 1)
    incl = blk & ((ss >= tt) if rev else (ss <= tt))
    strict = blk & ((ss > tt) if rev else (ss < tt))
    eye = (rj == cj).astype(F32)
    tri = incl[:c, :c].astype(F32).astype(MXU_DT)
    lo_half = lax.broadcasted_iota(jnp.int32, (2 * g_w, 2 * c), 1) < c
    last = 0 if rev else c - 1
    zero = jnp.zeros((), F32)
    chains = [(bi, g) for bi in range(len(rows)) for g in range(w // g_w)]

    def bd(x):
        return _mx(jnp.where(blk, jnp.concatenate([x] * nh, axis=0), zero))

    def rsum(z):
        out = z[0:c]
        for i in range(1, nh):
            out = out + z[i * c:(i + 1) * c]
        return out

    a_bd, r_bd, v_bd, bk_t, r_t, v_g, bk_hat, dec = {}, {}, {}, {}, {}, {}, {}, {}
    for bi, (r, lw, kd, v, kk, a) in enumerate(rows):
        ka = kk * a
        gi = _cumsum_mm(tri, lw)
        tot = gi[last:last + 1, :]
        e_neg = jnp.exp(-gi)
        e_tail = jnp.exp(tot - gi)
        full = {'a': -kk * jnp.exp(gi - lw), 'b': ka * e_neg, 'k': kd * e_neg, 'r': r * jnp.exp(gi),
                'bh': ka * e_tail, 'kh': kd * e_tail, 'dec': jnp.exp(tot), 'v': v}
        for g in range(w // g_w):
            x = {n: t[:, g * g_w:(g + 1) * g_w] for n, t in full.items()}
            ch = (bi, g)
            a_bd[ch], r_bd[ch], v_bd[ch] = bd(x['a']), bd(x['r']), bd(x['v'])
            bk_t[ch] = _mx(jnp.concatenate([x['b'], x['k']], axis=0))
            bk_hat[ch] = _mx(jnp.concatenate([x['bh'], x['kh']], axis=0))
            r_t[ch], v_g[ch], dec[ch] = _mx(x['r']), x['v'], x['dec']

    l_ab, a_ak, a_rb, a_rk = {}, {}, {}, {}
    for ch in chains:
        prod = _nt(jnp.concatenate([a_bd[ch], r_bd[ch]], axis=0), bk_t[ch])
        swp = pltpu.roll(prod, c, 1)
        xb = jnp.where(lo_half, prod, swp)
        xk = jnp.where(lo_half, swp, prod)
        xb = jnp.concatenate([xb, xb], axis=1)
        xk = jnp.concatenate([xk, xk], axis=1)
        l_ab[ch] = jnp.where(strict, xb[:g_w], zero)
        a_ak[ch] = _mx(jnp.where(strict, xk[:g_w], zero))
        a_rb[ch] = _mx(rsum(jnp.where(incl, xb[g_w:], zero)))
        a_rk[ch] = _mx(rsum(jnp.where(incl, xk[g_w:], zero)))

    xs = {ch: _mx(l_ab[ch]) for ch in chains}
    ps = {ch: eye + l_ab[ch] for ch in chains}
    for _ in range(max(1, (c - 1).bit_length() - 1)):
        xs = {ch: _mx(_nn(xs[ch], xs[ch])) for ch in chains}
        ps = {ch: ps[ch] + _nn(_mx(ps[ch]), xs[ch]) for ch in chains}

    t_cat = {ch: _mx(rsum(ps[ch])) for ch in chains}
    w_cat = {ch: _mx(_nn(t_cat[ch], a_bd[ch])) for ch in chains}
    ta = {ch: _mx(_nn(t_cat[ch], a_ak[ch])) for ch in chains}
    u0 = {ch: _nn(ta[ch], v_bd[ch]) for ch in chains}
    y_k = {ch: _nn(a_rk[ch], v_bd[ch]) for ch in chains}
    s_mem = {ch: s_ref[ch[0], ch[1]] for ch in chains}
    s_b = {ch: _mx(s_mem[ch]) for ch in chains}
    u = {ch: _nt(w_cat[ch], s_b[ch]) + u0[ch] for ch in chains}
    y = {ch: _nt(r_t[ch], s_b[ch]) + _nn(a_rb[ch], bd(u[ch])) + y_k[ch] for ch in chains}
    for ch in chains:
        upd = _tn(_mx(jnp.concatenate([u[ch], v_g[ch]], axis=0)), bk_hat[ch])
        s_ref[ch[0], ch[1]] = s_mem[ch] * dec[ch] + jnp.where(blk, upd, zero)
    return [jnp.concatenate([y[(bi, g)] for g in range(w // g_w)], axis=1) for bi in range(len(rows))]


def _rwkv_seq_kernel(*refs, rev, has_acc):
    r_ref, lw_ref, kd_ref, v_ref, kk_ref, a_ref, s0_ref = refs[:7]
    acc_ref = refs[7] if has_acc else None
    y_ref, s_ref = refs[7 + has_acc:]

    @pl.when(pl.program_id(1) == 0)
    def _():
        s_ref[...] = s0_ref[...]

    y, = _rwkv_core([(r_ref[0], lw_ref[0], kd_ref[0], v_ref[0], kk_ref[0], a_ref[0])], s_ref, rev)
    if has_acc:
        y = y + acc_ref[0]
    y_ref[0] = y


def _rwkv_scan_seq(r, lw, kd, v, kk, a, state, rev, acc=None):
    b, l, w = r.shape
    c = RK_CHUNK
    nc = l // c
    cidx = (lambda bi, ch: (bi, nc - 1 - ch, 0)) if rev else (lambda bi, ch: (bi, ch, 0))
    seq = pl.BlockSpec((1, c, w), cidx)
    st = pl.BlockSpec((1,) + state.shape[1:], lambda bi, ch: (bi, 0, 0, 0))
    args = [r, lw, kd, v, kk, a, state] + ([acc] if acc is not None else [])
    y, s = pl.pallas_call(
        functools.partial(_rwkv_seq_kernel, rev=rev, has_acc=acc is not None),
        grid=(b, nc),
        in_specs=[seq] * 6 + [st] + ([seq] if acc is not None else []),
        out_specs=[seq, st],
        out_shape=[jax.ShapeDtypeStruct((b, l, w), F32), jax.ShapeDtypeStruct(state.shape, F32)],
        compiler_params=_params(("parallel", "arbitrary")),
        name="rwkv_seq_rev" if rev else "rwkv_seq_fwd",
    )(*args)
    return y, s


_PV_ROWS = ('mu_r', 'mu_k', 'mu_v', 'w0', 'a0', 'kk', 'ka', 'rk', 'v0')


def _rwkv_grid_kernel(*refs, rev, nc, has_vf, has_acc, emit_vf):
    it = iter(refs)
    pr, pr_u, pr_d, pk, pk_u, pk_d, pv, pv_u, pv_d, wa, wa_u, wa_d = (next(it) for _ in range(12))
    vf_ref, lora_ref = (next(it), next(it)) if has_vf else (None, None)
    pvec, mu_wa, w2_ref, a2_ref, s0_ref = (next(it) for _ in range(5))
    accy_ref, accb_ref = (next(it), next(it)) if has_acc else (None, None)
    y_ref, bon_ref, s_ref = (next(it) for _ in range(3))
    vfo_ref = next(it) if emit_vf else None
    step = pl.program_id(1)

    @pl.when(step == 0)
    def _():
        s_ref[...] = s0_ref[...]

    c = pr.shape[1]
    w = pr.shape[2]
    rd = w // 16
    ci = nc - 1 - step if rev else step
    has_up = jnp.where(ci > 0, 1.0, 0.0)
    has_dn = jnp.where(ci < nc - 1, 1.0, 0.0)
    trow = lax.broadcasted_iota(jnp.int32, (c, 1), 0)
    prow = lambda name: pvec[_PV_ROWS.index(name):_PV_ROWS.index(name) + 1, :]

    def left(x):
        return jnp.where(trow == 0, 0.0, pltpu.roll(x, 1, 0))

    def right(x):
        return jnp.where(trow == c - 1, 0.0, pltpu.roll(x, c - 1, 0))

    ones = _seg_ones(RK_HD)
    d = 1 if rev else 0
    rows, bons = [], []
    for bi in range(pr.shape[0]):
        def lerp_wide(cur_ref, up_ref, dn_ref, mu):
            cur = cur_ref[bi]
            q = w // 4
            sh = jnp.concatenate([left(cur[:, :q]), right(cur[:, q:2 * q]),
                                  up_ref[bi] * has_up, dn_ref[bi] * has_dn], axis=1)
            return cur + mu * (sh - cur)

        r = lerp_wide(pr, pr_u, pr_d, prow('mu_r'))
        k = lerp_wide(pk, pk_u, pk_d, prow('mu_k'))
        v = lerp_wide(pv, pv_u, pv_d, prow('mu_v'))

        cur = wa[bi]
        qd = (lax.broadcasted_iota(jnp.int32, cur.shape, 1) & (rd - 1)) >> ((rd // 4).bit_length() - 1)
        sh = jnp.where(qd == 0, left(cur), jnp.where(qd == 1, right(cur),
                                                     jnp.where(qd == 2, wa_u[bi] * has_up, wa_d[bi] * has_dn)))
        xwa = cur + mu_wa[...] * (sh - cur)
        xw = xwa[:, d * rd:(d + 1) * rd]
        xa = xwa[:, (2 + d) * rd:(3 + d) * rd]
        z = -(prow('w0') + _nn(_mx(jnp.tanh(xw)), w2_ref[...]))
        softplus = jnp.maximum(z, 0.0) + jnp.log(1.0 + jnp.exp(-jnp.abs(z)))
        lw = -jnp.exp(-softplus - 0.5)
        a = _sigmoid(prow('a0') + _nn(_mx(xa), a2_ref[...]))
        if has_vf:
            v = v + (vf_ref[bi] - v) * _sigmoid(prow('v0') + lora_ref[bi])
        if emit_vf:
            vfo_ref[bi] = v
        kkn = k * prow('kk')
        kk = kkn / jnp.maximum(jnp.sqrt(_segsum(kkn * kkn, ones)), 1e-12)
        kd = k * (1.0 + (a - 1.0) * prow('ka'))
        bons.append(_segsum(r * kd * prow('rk'), ones) * v)
        rows.append((r, lw, kd, v, kk, a))
    ys = _rwkv_core(rows, s_ref, rev)
    for bi, (y, bon) in enumerate(zip(ys, bons)):
        if has_acc:
            y = y + accy_ref[bi]
            bon = bon + accb_ref[bi]
        y_ref[bi] = y
        bon_ref[bi] = bon


def _rwkv_scan_grid(pr, pk, pv, wa, pvec, mu_wa, w2, a2, state, rev, vf=None, lora=None,
                    acc=None, emit_vf=False):
    b, l, w = pr.shape
    c = RK_CHUNK
    assert c == GRID_W
    nc = l // c
    wq = w // 4
    ch = (lambda s: nc - 1 - s) if rev else (lambda s: s)
    cur = lambda bi, s: (bi, ch(s), 0)
    upi = lambda lane: (lambda bi, s: (bi, jnp.maximum(ch(s) - 1, 0), lane))
    dni = lambda lane: (lambda bi, s: (bi, jnp.minimum(ch(s) + 1, nc - 1), lane))
    bb = RK_BATCH if b % RK_BATCH == 0 else 1
    seq = pl.BlockSpec((bb, c, w), cur)
    wide = [seq, pl.BlockSpec((bb, c, wq), upi(2)), pl.BlockSpec((bb, c, wq), dni(3))]
    wa_w = wa.shape[2]
    small = [pl.BlockSpec((bb, c, wa_w), cur), pl.BlockSpec((bb, c, wa_w), upi(0)), pl.BlockSpec((bb, c, wa_w), dni(0))]
    full = lambda x: pl.BlockSpec(x.shape, lambda bi, s: (0,) * x.ndim)
    st = pl.BlockSpec((bb,) + state.shape[1:], lambda bi, s: (bi, 0, 0, 0))
    in_specs = wide * 3 + small
    args = [pr, pr, pr, pk, pk, pk, pv, pv, pv, wa, wa, wa]
    if vf is not None:
        in_specs += [seq, seq]
        args += [vf, lora]
    in_specs += [full(pvec), full(mu_wa), full(w2), full(a2), st]
    args += [pvec, mu_wa, w2, a2, state]
    if acc is not None:
        in_specs += [seq, seq]
        args += list(acc)
    out_specs = [seq, seq, st]
    out_shape = [jax.ShapeDtypeStruct((b, l, w), F32), jax.ShapeDtypeStruct((b, l, w), F32),
                 jax.ShapeDtypeStruct(state.shape, F32)]
    if emit_vf:
        out_specs.append(seq)
        out_shape.append(jax.ShapeDtypeStruct((b, l, w), F32))
    return pl.pallas_call(
        functools.partial(_rwkv_grid_kernel, rev=rev, nc=nc, has_vf=vf is not None,
                          has_acc=acc is not None, emit_vf=emit_vf),
        grid=(b // bb, nc),
        in_specs=in_specs,
        out_specs=out_specs,
        out_shape=out_shape,
        compiler_params=_params(("parallel", "arbitrary")),
        name="rwkv_grid_rev" if rev else "rwkv_grid_fwd",
    )(*args)


def _hgrn_kernel(*refs, rev, heads, hd, has_acc):
    q_ref, f_ref, v_ref, fb_ref, lb_ref, s0_ref = refs[:6]
    acc_ref = refs[6] if has_acc else None
    o_ref, s_ref = refs[6 + has_acc:]

    @pl.when(pl.program_id(1) == 0)
    def _():
        s_ref[...] = s0_ref[...]

    tt = q_ref.shape[1]
    sub = HG_SUB
    sh = sub.bit_length() - 1
    nsub = tt // sub
    mid = sub // 2 if rev else sub // 2 - 1
    last = 0 if rev else sub - 1
    ri = lax.broadcasted_iota(jnp.int32, (tt, tt), 0)
    ci = lax.broadcasted_iota(jnp.int32, (tt, tt), 1)
    causal = ((ri >> sh) == (ci >> sh)) & ((ci >= ri) if rev else (ci <= ri))
    lb = lb_ref[...]
    f = lb + (1.0 - lb) * _sigmoid(f_ref[0] + fb_ref[...])
    g_all = _cumsum_mm(causal.astype(F32).astype(MXU_DT), jnp.log(f))

    def row_of_each_sub(r):
        return jnp.concatenate([jnp.broadcast_to(g_all[j * sub + r:j * sub + r + 1], (sub, g_all.shape[1]))
                                for j in range(nsub)], axis=0)

    g_mid, g_last = row_of_each_sub(mid), row_of_each_sub(last)
    k_all = 1.0 - f
    q_all = _silu(q_ref[0])
    q1 = _mx(q_all * jnp.exp(g_all - g_mid))
    k1 = _mx(k_all * jnp.exp(g_mid - g_all))
    qg_f = q_all * jnp.exp(g_all)
    kl_f = k_all * jnp.exp(g_last - g_all)
    v_all = _mx(v_ref[0])
    assert nsub % 2 == 0
    is_first = lambda j: (j % 2 == 1) == rev
    dec = [jnp.exp(g_all[j * sub + last:j * sub + last + 1]) for j in range(nsub)]
    one_row = jnp.ones_like(dec[0])
    per_sub = lambda vals: jnp.concatenate([jnp.broadcast_to(x, (sub, x.shape[1])) for x in vals], axis=0)
    qg, kl = _mx(qg_f), _mx(kl_f)
    qg2 = _mx(qg_f * per_sub([one_row if is_first(j) else dec[j ^ 1] for j in range(nsub)]))
    kl2 = _mx(kl_f * per_sub([dec[j ^ 1] if is_first(j) else one_row for j in range(nsub)]))
    sub_r, sub_c = ri >> sh, ci >> sh
    second_sees_first = ((sub_r ^ 1) == sub_c) & ((sub_r & 1) == (0 if rev else 1))
    lanes = [slice(h * hd, (h + 1) * hd) for h in range(heads)]
    zero = jnp.zeros((), F32)
    att = [_mx(jnp.where(causal, _nt(q1[:, sl], k1[:, sl]), zero)
               + jnp.where(second_sees_first, _nt(qg[:, sl], kl[:, sl]), zero)) for sl in lanes]
    intra = [_nn(att[h], v_all[:, lanes[h]]) for h in range(heads)]
    pairs = [nsub // 2 - 1 - s if rev else s for s in range(nsub // 2)]
    rows = [slice(2 * p * sub, 2 * (p + 1) * sub) for p in range(nsub // 2)]
    upd = {(p, h): _tn(v_all[rows[p], lanes[h]], kl2[rows[p], lanes[h]]) for p in pairs for h in range(heads)}
    s_mem = [s_ref[0, h] for h in range(heads)]
    for p in pairs:
        dec_pair = dec[2 * p] * dec[2 * p + 1]
        outs = []
        for h in range(heads):
            outs.append(intra[h][rows[p]] + _nt(qg2[rows[p], lanes[h]], _mx(s_mem[h])))
            s_mem[h] = s_mem[h] * dec_pair[:, lanes[h]] + upd[(p, h)]
        o = jnp.concatenate(outs, axis=1)
        if has_acc:
            o = o + acc_ref[0, rows[p], :]
        o_ref[0, rows[p], :] = o
    for h in range(heads):
        s_ref[0, h] = s_mem[h]


def _hgrn_scan(pq, pf, pi, f_b, lb, state, rev, acc=None):
    b, l, w = pq.shape
    heads = HG_HEADS
    hd = w // heads
    tt = min(HG_TILE, l)
    nc = l // tt
    d = 1 if rev else 0
    ch = (lambda s: nc - 1 - s) if rev else (lambda s: s)
    seq = pl.BlockSpec((1, tt, w), lambda bi, s: (bi, ch(s), 0))
    fseq = pl.BlockSpec((1, tt, w), lambda bi, s: (bi, ch(s), d))
    vec = pl.BlockSpec((1, w), lambda bi, s: (0, 0))
    st = pl.BlockSpec((1,) + state.shape[1:], lambda bi, s: (bi, 0, 0, 0))
    args = [pq, pf, pi, f_b, lb, state] + ([acc] if acc is not None else [])
    o, s = pl.pallas_call(
        functools.partial(_hgrn_kernel, rev=rev, heads=heads, hd=hd, has_acc=acc is not None),
        grid=(b, nc),
        in_specs=[seq, fseq, seq, vec, vec, st] + ([seq] if acc is not None else []),
        out_specs=[seq, st],
        out_shape=[jax.ShapeDtypeStruct((b, l, w), F32), jax.ShapeDtypeStruct(state.shape, F32)],
        compiler_params=_params(("parallel", "arbitrary")),
        name="hgrn_rev" if rev else "hgrn_fwd",
    )(*args)
    return o, s


_EV_ROWS = ('ml_norm_w', 'rk_ln_w', 'rk_ln_b', 'hg_norm_w', 'gate_b0', 'gate_b1', 'gate_b2', 'final_w')


def _epilogue_kernel(hm_ref, mo_ref, mz_ref, yr_ref, bon_ref, rz_ref, oh_ref, hz_ref,
                     g0_ref, g1_ref, g2_ref, xs_ref, gx_ref, ev_ref,
                     wpm_ref, wpr_ref, wph_ref, wout_ref, o_ref, *, final):
    ev = lambda name: ev_ref[_EV_ROWS.index(name):_EV_ROWS.index(name) + 1, :]
    d = hm_ref.shape[2]

    def seg_mean(x, seg):
        return _segsum(x, _seg_ones(seg)) * (1.0 / seg)

    seg = d // ML_HEADS
    y = hm_ref[0]
    y = y - seg_mean(y, seg)
    y = y * lax.rsqrt(seg_mean(y * y, seg) + NORM_EPS) * ev('ml_norm_w')
    u_m = _sigmoid(mo_ref[0]) * y * _silu(mz_ref[0])
    y = yr_ref[0]
    y = y - seg_mean(y, RK_HD)
    y = y * lax.rsqrt(seg_mean(y * y, RK_HD) + RK_GN_EPS) * ev('rk_ln_w') + ev('rk_ln_b') + bon_ref[0]
    u_r = y * _silu(rz_ref[0])
    seg = d // HG_HEADS
    o = oh_ref[0]
    o = o * lax.rsqrt(seg_mean(o * o, seg) + NORM_EPS) * ev('hg_norm_w')
    u_h = o * _silu(hz_ref[0])

    merged = (_sigmoid(g0_ref[0] + ev('gate_b0')) * _nn(_mx(u_m), wpm_ref[...])
              + _sigmoid(g1_ref[0] + ev('gate_b1')) * _nn(_mx(u_r), wpr_ref[...])
              + _sigmoid(g2_ref[0] + ev('gate_b2')) * _nn(_mx(u_h), wph_ref[...]))
    xs = xs_ref[0] + gx_ref[0] * _nn(_mx(merged), wout_ref[...])
    if final:
        xs = xs * lax.rsqrt(jnp.mean(xs * xs, axis=-1, keepdims=True) + NORM_EPS) * ev('final_w')
    o_ref[0] = xs


def _epilogue(hm, mo, mz, yr, bon, rz, oh, hz, gate, xs, gx, p, final_w):
    b, l, d = xs.shape
    tt = min(EP_TILE, l)
    seq = pl.BlockSpec((1, tt, d), lambda bi, i: (bi, i, 0))
    gseq = lambda lane: pl.BlockSpec((1, tt, d), lambda bi, i: (bi, i, lane))
    ev = jnp.stack([p['ml_norm_w'], p['rk_ln_w'], p['rk_ln_b'], p['hg_norm_w'],
                    p['gate_b'][0], p['gate_b'][1], p['gate_b'][2],
                    final_w if final_w is not None else jnp.ones((d,), F32)], axis=0)
    wspec = pl.BlockSpec((d, d), lambda bi, i: (0, 0))
    ws = [_mx(p[n]) for n in ('w_pm', 'w_pr', 'w_ph', 'w_out')]
    return pl.pallas_call(
        functools.partial(_epilogue_kernel, final=final_w is not None),
        grid=(b, l // tt),
        in_specs=[seq] * 8 + [gseq(0), gseq(1), gseq(2), seq,
                              pl.BlockSpec((1, 1, d), lambda bi, i: (bi, 0, 0)),
                              pl.BlockSpec(ev.shape, lambda bi, i: (0, 0))] + [wspec] * 4,
        out_specs=seq,
        out_shape=jax.ShapeDtypeStruct((b, l, d), F32),
        compiler_params=_params(("parallel", "parallel")),
        name="epilogue",
    )(hm, mo, mz, yr, bon, rz, oh, hz, gate, gate, gate, xs, gx, ev, *ws)


def _rms_norm(x, w):
    return x * lax.rsqrt(jnp.mean(x * x, axis=-1, keepdims=True) + NORM_EPS) * w


def _bi_shift_seq(u):
    half = u.shape[-1] // 2
    g = jnp.pad(u, ((0, 0), (1, 1), (0, 0)))
    return jnp.concatenate((g[:, :-2, :half], g[:, 2:, half:]), axis=-1)


def _in_layout(d):
    rd = d // 16
    return (('m_q', d), ('m_k', d), ('m_v', d), ('m_o', d), ('m_z', d), ('m_if', 4 * ML_HEADS),
            ('r_r', d), ('r_k', d), ('r_v', d), ('r_z', d), ('r_wa', 4 * rd),
            ('h_q', d), ('h_f', 2 * d), ('h_i', d), ('h_z', d), ('gate', 3 * d))


_MXU_ONLY = ('m_v', 'h_i')


def _split_w_in(w_in):
    out, off = {}, 0
    for name, width in _in_layout(w_in.shape[0]):
        out[name] = w_in[:, off:off + width].astype(MXU_DT)
        off += width
    return out


def _zero_states(b, d):
    ml_hd = d // ML_HEADS
    ml = (jnp.zeros((b, ML_HEADS, ml_hd, ml_hd), F32), jnp.zeros((b, ML_HEADS, 1, ml_hd), F32),
          jnp.zeros((b, ML_HEADS, 8, 128), F32))
    rk = jnp.zeros((b, d // RK_GROUP, RK_GROUP, RK_GROUP), F32)
    hg_hd = d // HG_HEADS
    hg = jnp.zeros((b, HG_HEADS, hg_hd, hg_hd), F32)
    return ((ml, ml), (rk, rk), (hg, hg))


def _mlstm_branch(proj, p, init, b, l):
    gl = proj('m_if').reshape(b, l, 2, 2, ML_HEADS) + p['ml_if_b']
    acc, states = None, []
    for dr in (0, 1):
        log_i = gl[:, :, dr, 0]
        log_f = jax.nn.log_sigmoid(gl[:, :, dr, 1])
        acc, st = _mlstm_scan(proj('m_q'), proj('m_k'), proj('m_v'), p['ml_conv'], log_i, log_f,
                              init[dr], rev=bool(dr), acc=acc)
        states.append(st)
    return acc, tuple(states)


def _rwkv7_seq_branch(proj, h2, p, init, v_first, b, l, d):
    nh, n = d // RK_HD, RK_HD
    rd = d // 16
    mu = p['rk_mu']

    def lerp_shift(u, m):
        return u + m * (_bi_shift_seq(u) - u)

    r = lerp_shift(proj('r_r'), mu[0:d])
    k = lerp_shift(proj('r_k'), mu[d:2 * d])
    v = lerp_shift(proj('r_v'), mu[2 * d:3 * d])
    pwa = proj('r_wa')
    xwa = jnp.concatenate([lerp_shift(pwa[..., i * rd:(i + 1) * rd], mu[3 * d + i * rd:3 * d + (i + 1) * rd])
                           for i in range(4)], axis=-1)
    if p['rk_v0'] is None:
        v_first = v
    else:
        lora = _mm(_mm(h2, p['rk_v1']), p['rk_v2']).reshape(b, l, d)
        v = v + (v_first - v) * jax.nn.sigmoid(p['rk_v0'] + lora)
    kk = (k * p['rk_kk']).reshape(b, l, nh, n)
    kk = kk / jnp.maximum(jnp.sqrt(jnp.sum(kk * kk, axis=-1, keepdims=True)), 1e-12)
    kk = kk.reshape(b, l, d)
    acc, states, kd_sum = None, [], 0.0
    for dr in (0, 1):
        xw = xwa[..., dr * rd:(dr + 1) * rd]
        xa = xwa[..., (2 + dr) * rd:(3 + dr) * rd]
        log_w = -jax.nn.softplus(-(p['rk_w0'][dr] + _mm(jnp.tanh(xw).reshape(b * l, rd), p['rk_w2'][dr]).reshape(b, l, d))) - 0.5
        lw = -jnp.exp(log_w)
        a = jax.nn.sigmoid(p['rk_a0'][dr] + _mm(xa.reshape(b * l, rd), p['rk_a2'][dr]).reshape(b, l, d))
        kd = k * (1.0 + (a - 1.0) * p['rk_ka'])
        acc, s = _rwkv_scan_seq(r, lw, kd, v, kk, a, init[dr], rev=bool(dr), acc=acc)
        states.append(s)
        kd_sum = kd_sum + kd
    bonus = jnp.sum((r * kd_sum * p['rk_rk']).reshape(b, l, nh, n), axis=-1, keepdims=True) * v.reshape(b, l, nh, n)
    return acc, bonus.reshape(b, l, d), tuple(states), v_first


def _rwkv7_grid_branch(proj, h2, p, init, v_first, b, l, d):
    mu = p['rk_mu']
    zeros = jnp.zeros((d,), F32)
    has_vf = p['rk_v0'] is not None
    lora = _mm(_mm(h2, p['rk_v1']), p['rk_v2']).reshape(b, l, d) if has_vf else None
    acc, states, vf_out = None, [], v_first
    for dr in (0, 1):
        rows = {'mu_r': mu[0:d], 'mu_k': mu[d:2 * d], 'mu_v': mu[2 * d:3 * d], 'w0': p['rk_w0'][dr],
                'a0': p['rk_a0'][dr], 'kk': p['rk_kk'], 'ka': p['rk_ka'], 'rk': p['rk_rk'],
                'v0': p['rk_v0'] if has_vf else zeros}
        pvec = jnp.stack([rows[n] for n in _PV_ROWS], axis=0)
        emit_vf = (not has_vf) and dr == 0
        out = _rwkv_scan_grid(proj('r_r'), proj('r_k'), proj('r_v'), proj('r_wa'), pvec, mu[3 * d:][None],
                              _mx(p['rk_w2'][dr]), _mx(p['rk_a2'][dr]), init[dr], rev=bool(dr),
                              vf=v_first if has_vf else None, lora=lora, acc=acc, emit_vf=emit_vf)
        acc = (out[0], out[1])
        states.append(out[2])
        if emit_vf:
            vf_out = out[3]
    return acc[0], acc[1], tuple(states), vf_out


def _hgrn2_branch(proj, p, init):
    acc, states = None, []
    for dr in (0, 1):
        acc, s = _hgrn_scan(proj('h_q'), proj('h_f'), proj('h_i'), p['hg_f_b'][dr][None], p['hg_lb'][dr][None],
                            init[dr], rev=bool(dr), acc=acc)
        states.append(s)
    return acc, tuple(states)


def _mixer(h, p, init, on_grid, v_first, need_out, xs, gx, final_w):
    b, l, d = h.shape
    h2 = _mx(h.reshape(b * l, d))
    cache = {}

    def proj(name):
        if name not in cache:
            w = p['w_in'][name]
            cache[name] = _mm(h2, w, MXU_DT if name in _MXU_ONLY else F32).reshape(b, l, w.shape[1])
        return cache[name]

    hm, st_m = _mlstm_branch(proj, p, init[0], b, l)
    if on_grid:
        yr, bon, st_r, v_first = _rwkv7_grid_branch(proj, h2, p, init[1], v_first, b, l, d)
    else:
        yr, bon, st_r, v_first = _rwkv7_seq_branch(proj, h2, p, init[1], v_first, b, l, d)
    oh, st_h = _hgrn2_branch(proj, p, init[2])
    states = (st_m, st_r, st_h)
    if not need_out:
        return None, states, v_first
    out = _epilogue(hm, proj('m_o'), proj('m_z'), yr, bon, proj('r_z'), oh, proj('h_z'), proj('gate'),
                    xs, gx, p, final_w)
    return out, states, v_first


def kernel(x, c, ctx, c_ctx, norm_w, ada_w, ada_b, w_in, gate_b, ml_conv, ml_if_b, ml_norm_w, rk_mu, rk_w0, rk_w2, rk_a0, rk_a2, rk_kk, rk_ka, rk_rk, rk_v0, rk_v1, rk_v2, rk_ln_w, rk_ln_b, hg_f_b, hg_lb, hg_norm_w, w_pm, w_pr, w_ph, w_out, final_norm_w):
    batch, _, d = x.shape
    depth = w_in.shape[0]
    lb_p = jax.nn.softmax(hg_lb.astype(F32), axis=1)
    lower_bounds = jnp.cumsum(lb_p, axis=1) - lb_p[:, :1]
    xs, cs = x, ctx
    vf_x, vf_c = None, None
    cond = jnp.concatenate([jax.nn.silu(c), jax.nn.silu(c_ctx)[None]], axis=0)
    for l in range(depth):
        last = l == depth - 1
        p = {'w_in': _split_w_in(w_in[l]), 'gate_b': gate_b[l], 'ml_conv': ml_conv[l], 'ml_if_b': ml_if_b[l],
             'ml_norm_w': ml_norm_w[l], 'rk_mu': rk_mu[l], 'rk_w0': rk_w0[l], 'rk_w2': rk_w2[l],
             'rk_a0': rk_a0[l], 'rk_a2': rk_a2[l], 'rk_kk': rk_kk[l], 'rk_ka': rk_ka[l], 'rk_rk': rk_rk[l],
             'rk_v0': rk_v0[l - 1] if l > 0 else None, 'rk_v1': rk_v1[l - 1] if l > 0 else None,
             'rk_v2': rk_v2[l - 1] if l > 0 else None, 'rk_ln_w': rk_ln_w[l], 'rk_ln_b': rk_ln_b[l],
             'hg_f_b': hg_f_b[l], 'hg_lb': lower_bounds[:, l], 'hg_norm_w': hg_norm_w[l],
             'w_pm': w_pm[l], 'w_pr': w_pr[l], 'w_ph': w_ph[l], 'w_out': w_out[l]}
        mod = _mm(cond, ada_w[l]) + ada_b[l]
        shift_x, scale_x, gate_x = jnp.split(mod[:batch, None, :], 3, axis=-1)
        shift_c, scale_c, gate_c = jnp.split(mod[batch], 3, axis=-1)
        hc = _rms_norm(cs, norm_w[l]) * (1.0 + scale_c) + shift_c
        gc = jnp.broadcast_to(gate_c[None, None, :], (batch, 1, d))
        cs_new, st_c, vf_c = _mixer(hc, p, _zero_states(batch, d), False, vf_c, not last, cs, gc, None)
        hx = _rms_norm(xs, norm_w[l]) * (1.0 + scale_x) + shift_x
        xs, _, vf_x = _mixer(hx, p, st_c, True, vf_x, True, xs, gate_x, final_norm_w if last else None)
        if not last:
            cs = cs_new
    return xs
```

```python
import functools

import jax
import jax.numpy as jnp
from jax import lax
from jax.experimental import pallas as pl
from jax.experimental.pallas import tpu as pltpu

F32 = jnp.float32
MXU_DT = jnp.bfloat16

NORM_EPS = 1e-6
GRID_W = 64
ML_HEADS = 4
ML_CHUNK = 256
RK_HD = 64
RK_CHUNK = 64
RK_GROUP = 256
RK_BATCH = 4
RK_GN_EPS = 64e-5
HG_HEADS = 8
HG_SUB = 16
HG_TILE = 128
EP_TILE = 256
SEG_W = 256
VMEM_LIMIT = 56 * 1024 * 1024


def _nt(a, b):
    return lax.dot_general(a, b, (((1,), (1,)), ((), ())), preferred_element_type=F32)


def _tn(a, b):
    return lax.dot_general(a, b, (((0,), (0,)), ((), ())), preferred_element_type=F32)


def _nn(a, b):
    return jnp.dot(a, b, preferred_element_type=F32)


def _mx(a):
    return a.astype(MXU_DT)


def _split2(x):
    hi = x.astype(MXU_DT)
    return hi, (x - hi.astype(F32)).astype(MXU_DT)


def _cumsum_mm(tri, x):
    hi = x.astype(MXU_DT)
    r1 = x - hi.astype(F32)
    mid = r1.astype(MXU_DT)
    lo = (r1 - mid.astype(F32)).astype(MXU_DT)
    one = lambda t: _nn(t, hi) + _nn(t, mid) + _nn(t, lo)
    return [one(t) for t in tri] if isinstance(tri, (list, tuple)) else one(tri)


def _seg_ones(seg):
    sh = seg.bit_length() - 1
    ri = lax.broadcasted_iota(jnp.int32, (SEG_W, SEG_W), 0)
    ci = lax.broadcasted_iota(jnp.int32, (SEG_W, SEG_W), 1)
    return ((ri >> sh) == (ci >> sh)).astype(F32).astype(MXU_DT)


def _segsum(x, ones):
    outs = []
    for g in range(x.shape[1] // SEG_W):
        hi, lo = _split2(x[:, g * SEG_W:(g + 1) * SEG_W])
        outs.append(_nn(hi, ones) + _nn(lo, ones))
    return outs[0] if len(outs) == 1 else jnp.concatenate(outs, axis=1)


def _sigmoid(x):
    return jax.nn.sigmoid(x)


def _silu(x):
    return x * jax.nn.sigmoid(x)


def _params(sem):
    return pltpu.CompilerParams(dimension_semantics=sem, vmem_limit_bytes=VMEM_LIMIT)


def _mm_kernel(x_ref, w_ref, o_ref):
    o_ref[...] = _nn(_mx(x_ref[...]), w_ref[...]).astype(o_ref.dtype)


def _mm(x, w, out_dtype=F32):
    m, k = x.shape
    n = w.shape[1]
    mp = -(-m // 8) * 8
    npad = -(-n // 128) * 128
    if mp != m:
        x = jnp.pad(x, ((0, mp - m), (0, 0)))
    wb = w.astype(MXU_DT)
    if npad != n:
        wb = jnp.pad(wb, ((0, 0), (0, npad - n)))
    tm = next(t for t in (1024, 512, 256, 128, 64, 32, 16, 8) if mp % t == 0)
    tn = next(t for t in (1024, 512, 256, 128) if npad % t == 0)
    out = pl.pallas_call(
        _mm_kernel,
        grid=(mp // tm, npad // tn),
        in_specs=[pl.BlockSpec((tm, k), lambda i, j: (i, 0)),
                  pl.BlockSpec((k, tn), lambda i, j: (0, j))],
        out_specs=pl.BlockSpec((tm, tn), lambda i, j: (i, j)),
        out_shape=jax.ShapeDtypeStruct((mp, npad), out_dtype),
        compiler_params=_params(("parallel", "parallel")),
        name="mm",
    )(x, wb)
    if mp != m or npad != n:
        out = out[:m, :n]
    return out


def _mlstm_kernel(*refs, rev, heads, hd, nc, has_acc):
    (q_ref, qp_ref, qn_ref, k_ref, kp_ref, kn_ref, v_ref, cw_ref, gcol_ref, grow_ref,
     c0_ref, n0_ref, m0_ref) = refs[:13]
    acc_ref = refs[13] if has_acc else None
    h_ref, c_ref, n_ref, m_ref = refs[13 + has_acc:]
    step = pl.program_id(1)

    @pl.when(step == 0)
    def _():
        c_ref[...] = c0_ref[...]
        n_ref[...] = n0_ref[...]
        m_ref[...] = m0_ref[...]

    t = q_ref.shape[1]
    ci = nc - 1 - step if rev else step
    has_prev = jnp.where(ci > 0, 1.0, 0.0)
    has_next = jnp.where(ci < nc - 1, 1.0, 0.0)
    trow = lax.broadcasted_iota(jnp.int32, (t, 1), 0)

    def conv_silu(u, prev_blk, next_blk, w3):
        up = jnp.where(trow == 0, prev_blk[7:8, :] * has_prev, pltpu.roll(u, 1, 0))
        dn = jnp.where(trow == t - 1, next_blk[0:1, :] * has_next, pltpu.roll(u, t - 1, 0))
        return _silu(up * w3[0:1, :] + u * w3[1:2, :] + dn * w3[2:3, :])

    q_all = conv_silu(q_ref[0], qp_ref[0], qn_ref[0], cw_ref[0:3, :]) * (hd ** -0.5)
    k_all = conv_silu(k_ref[0], kp_ref[0], kn_ref[0], cw_ref[3:6, :])

    row = lax.broadcasted_iota(jnp.int32, (t, t), 0)
    col = lax.broadcasted_iota(jnp.int32, (t, t), 1)
    mask = (col >= row) if rev else (col <= row)
    last = 0 if rev else t - 1
    hs = range(heads)
    lanes = [slice(h * hd, (h + 1) * hd) for h in hs]
    q32 = [q_all[:, sl] for sl in lanes]
    qc = [_mx(x) for x in q32]
    kc = [k_all[:, sl] for sl in lanes]
    vc = [_mx(v_ref[0, :, sl]) for sl in lanes]
    bc_col = [gcol_ref[0, :, h:h + 1] for h in hs]
    i_col = [gcol_ref[0, :, heads + h:heads + h + 1] for h in hs]
    m_prev = [m_ref[0, h, 0:1, 0:1] for h in hs]
    c_mem = [c_ref[0, h] for h in hs]
    n_mem = [n_ref[0, h] for h in hs]
    dmat = [jnp.where(mask, bc_col[h] - grow_ref[0, 0, h:h + 1, :] + grow_ref[0, 0, heads + h:heads + h + 1, :],
                      -jnp.inf) for h in hs]
    inter = [bc_col[h] + m_prev[h] for h in hs]
    m_t = [jnp.maximum(jnp.max(dmat[h], axis=-1, keepdims=True), inter[h]) for h in hs]
    qk = [_nt(qc[h], _mx(kc[h])) for h in hs]
    q_c = [_nn(qc[h], _mx(c_mem[h])) for h in hs]
    s = [qk[h] * jnp.exp(dmat[h] - m_t[h]) for h in hs]
    w_inter = [jnp.exp(inter[h] - m_t[h]) for h in hs]
    num = [_nn(_mx(s[h]), vc[h]) + w_inter[h] * q_c[h] for h in hs]
    for h in hs:
        qn = jnp.sum(q32[h] * n_mem[h], axis=-1, keepdims=True)
        den = jnp.sum(s[h], axis=-1, keepdims=True) + w_inter[h] * qn
        h_out = num[h] / jnp.maximum(jnp.abs(den), jnp.exp(-m_t[h]))
        if has_acc:
            h_out = h_out + acc_ref[0, :, lanes[h]]
        h_ref[0, :, lanes[h]] = h_out
    for h in hs:
        total = bc_col[h][last:last + 1, :]
        g_col = total - bc_col[h] + i_col[h]
        m_new = jnp.maximum(total + m_prev[h], jnp.max(g_col, axis=0, keepdims=True))
        wk = jnp.exp(g_col - m_new)
        dec = jnp.exp(total + m_prev[h] - m_new)
        kw = kc[h] * wk
        c_ref[0, h] = dec * c_mem[h] + _tn(_mx(kw), vc[h])
        n_ref[0, h] = dec * n_mem[h] + jnp.sum(kw, axis=0, keepdims=True)
        m_ref[0, h] = jnp.broadcast_to(m_new, m_ref.shape[2:])


def _mlstm_scan(pq, pk, pv, conv_w, log_i, log_f, state, rev, acc=None):
    b, l, _ = pq[0].shape
    w = conv_w.shape[-1]
    heads = ML_HEADS
    hd = w // heads
    t = min(ML_CHUNK, l)
    assert l % t == 0
    nc = l // t
    nb8 = l // 8
    lf = log_f.reshape(b, nc, t, heads)
    if rev:
        bcum = jnp.flip(jnp.cumsum(jnp.flip(lf, 2), axis=2), 2)
    else:
        bcum = jnp.cumsum(lf, axis=2)
    gcol = jnp.concatenate([bcum.reshape(b, l, heads), log_i], axis=-1)
    grow = gcol.reshape(b, nc, t, 2 * heads).transpose(0, 1, 3, 2)
    ch = (lambda c: nc - 1 - c) if rev else (lambda c: c)
    cidx = lambda bi, c: (bi, ch(c), 0)
    ridx = lambda bi, c: (bi, ch(c), 0, 0)
    sidx = lambda bi, c: (bi, 0, 0, 0)
    c0, n0, m0 = state
    seq = pl.BlockSpec((1, t, w), cidx)
    col = lambda x: pl.BlockSpec((1, t, w), lambda bi, c: (bi, ch(c), x[1] // w))
    prv = lambda x: pl.BlockSpec((1, 8, w), lambda bi, c: (bi, jnp.maximum(ch(c) * (t // 8) - 1, 0), x[1] // w))
    nxt = lambda x: pl.BlockSpec((1, 8, w), lambda bi, c: (bi, jnp.minimum((ch(c) + 1) * (t // 8), nb8 - 1), x[1] // w))
    st_specs = [pl.BlockSpec((1, heads, hd, hd), sidx), pl.BlockSpec((1, heads, 1, hd), sidx),
                pl.BlockSpec((1, heads, 8, 128), sidx)]
    cw = conv_w.reshape(6, w)
    in_specs = [col(pq), prv(pq), nxt(pq), col(pk), prv(pk), nxt(pk), col(pv),
                pl.BlockSpec((6, w), lambda bi, c: (0, 0)),
                pl.BlockSpec((1, t, 2 * heads), cidx), pl.BlockSpec((1, 1, 2 * heads, t), ridx)] + st_specs
    args = [pq[0], pq[0], pq[0], pk[0], pk[0], pk[0], pv[0], cw, gcol, grow, c0, n0, m0]
    if acc is not None:
        in_specs.append(seq)
        args.append(acc)
    out = pl.pallas_call(
        functools.partial(_mlstm_kernel, rev=rev, heads=heads, hd=hd, nc=nc, has_acc=acc is not None),
        grid=(b, nc),
        in_specs=in_specs,
        out_specs=[seq] + st_specs,
        out_shape=[jax.ShapeDtypeStruct((b, l, w), F32), jax.ShapeDtypeStruct(c0.shape, F32),
                   jax.ShapeDtypeStruct(n0.shape, F32), jax.ShapeDtypeStruct(m0.shape, F32)],
        compiler_params=_params(("parallel", "arbitrary")),
        name="mlstm_rev" if rev else "mlstm_fwd",
    )(*args)
    return out[0], (out[1], out[2], out[3])


def _rwkv_core(rows, s_ref, rev):
    c, w = rows[0][0].shape
    g_w = RK_GROUP
    nh = g_w // RK_HD
    assert c == RK_HD and c & (c - 1) == 0
    sh = c.bit_length() - 1
    rj = lax.broadcasted_iota(jnp.int32, (g_w, g_w), 0)
    cj = lax.broadcasted_iota(jnp.int32, (g_w, g_w), 1)
    blk = (rj >> sh) == (cj >> sh)
    tt, ss = rj & (c - 1), cj & (c - 1)
    incl = blk & ((ss >= tt) if rev else (ss <= tt))
    strict = blk & ((ss > tt) if rev else (ss < tt))
    eye = (rj == cj).astype(F32)
    tri = incl[:c, :c].astype(F32).astype(MXU_DT)
    lo_half = lax.broadcasted_iota(jnp.int32, (2 * g_w, 2 * c), 1) < c
    last = 0 if rev else c - 1
    zero = jnp.zeros((), F32)
    chains = [(bi, g) for bi in range(len(rows)) for g in range(w // g_w)]

    def bd(x):
        return _mx(jnp.where(blk, jnp.concatenate([x] * nh, axis=0), zero))

    def rsum(z):
        out = z[0:c]
        for i in range(1, nh):
            out = out + z[i * c:(i + 1) * c]
        return out

    a_bd, r_bd, v_bd, bk_t, r_t, v_g, bk_hat, dec = {}, {}, {}, {}, {}, {}, {}, {}
    for bi, (r, lw, kd, v, kk, a) in enumerate(rows):
        ka = kk * a
        gi = _cumsum_mm(tri, lw)
        tot = gi[last:last + 1, :]
        e_neg = jnp.exp(-gi)
        e_tail = jnp.exp(tot - gi)
        full = {'a': -kk * jnp.exp(gi - lw), 'b': ka * e_neg, 'k': kd * e_neg, 'r': r * jnp.exp(gi),
                'bh': ka * e_tail, 'kh': kd * e_tail, 'dec': jnp.exp(tot), 'v': v}
        for g in range(w // g_w):
            x = {n: t[:, g * g_w:(g + 1) * g_w] for n, t in full.items()}
            ch = (bi, g)
            a_bd[ch], r_bd[ch], v_bd[ch] = bd(x['a']), bd(x['r']), bd(x['v'])
            bk_t[ch] = _mx(jnp.concatenate([x['b'], x['k']], axis=0))
            bk_hat[ch] = _mx(jnp.concatenate([x['bh'], x['kh']], axis=0))
            r_t[ch], v_g[ch], dec[ch] = _mx(x['r']), x['v'], x['dec']

    l_ab, a_ak, a_rb, a_rk = {}, {}, {}, {}
    for ch in chains:
        prod = _nt(jnp.concatenate([a_bd[ch], r_bd[ch]], axis=0), bk_t[ch])
        swp = pltpu.roll(prod, c, 1)
        xb = jnp.where(lo_half, prod, swp)
        xk = jnp.where(lo_half, swp, prod)
        xb = jnp.concatenate([xb, xb], axis=1)
        xk = jnp.concatenate([xk, xk], axis=1)
        l_ab[ch] = jnp.where(strict, xb[:g_w], zero)
        a_ak[ch] = _mx(jnp.where(strict, xk[:g_w], zero))
        a_rb[ch] = _mx(rsum(jnp.where(incl, xb[g_w:], zero)))
        a_rk[ch] = _mx(rsum(jnp.where(incl, xk[g_w:], zero)))

    xs = {ch: _mx(l_ab[ch]) for ch in chains}
    ps = {ch: eye + l_ab[ch] for ch in chains}
    for _ in range(max(1, (c - 1).bit_length() - 1)):
        xs = {ch: _mx(_nn(xs[ch], xs[ch])) for ch in chains}
        ps = {ch: ps[ch] + _nn(_mx(ps[ch]), xs[ch]) for ch in chains}

    t_cat = {ch: _mx(rsum(ps[ch])) for ch in chains}
    w_cat = {ch: _mx(_nn(t_cat[ch], a_bd[ch])) for ch in chains}
    ta = {ch: _mx(_nn(t_cat[ch], a_ak[ch])) for ch in chains}
    u0 = {ch: _nn(ta[ch], v_bd[ch]) for ch in chains}
    y_k = {ch: _nn(a_rk[ch], v_bd[ch]) for ch in chains}
    s_mem = {ch: s_ref[ch[0], ch[1]] for ch in chains}
    s_b = {ch: _mx(s_mem[ch]) for ch in chains}
    u = {ch: _nt(w_cat[ch], s_b[ch]) + u0[ch] for ch in chains}
    y = {ch: _nt(r_t[ch], s_b[ch]) + _nn(a_rb[ch], bd(u[ch])) + y_k[ch] for ch in chains}
    for ch in chains:
        upd = _tn(_mx(jnp.concatenate([u[ch], v_g[ch]], axis=0)), bk_hat[ch])
        s_ref[ch[0], ch[1]] = s_mem[ch] * dec[ch] + jnp.where(blk, upd, zero)
    return [jnp.concatenate([y[(bi, g)] for g in range(w // g_w)], axis=1) for bi in range(len(rows))]


def _rwkv_seq_kernel(*refs, rev, has_acc):
    r_ref, lw_ref, kd_ref, v_ref, kk_ref, a_ref, s0_ref = refs[:7]
    acc_ref = refs[7] if has_acc else None
    y_ref, s_ref = refs[7 + has_acc:]

    @pl.when(pl.program_id(1) == 0)
    def _():
        s_ref[...] = s0_ref[...]

    y, = _rwkv_core([(r_ref[0], lw_ref[0], kd_ref[0], v_ref[0], kk_ref[0], a_ref[0])], s_ref, rev)
    if has_acc:
        y = y + acc_ref[0]
    y_ref[0] = y


def _rwkv_scan_seq(r, lw, kd, v, kk, a, state, rev, acc=None):
    b, l, w = r.shape
    c = RK_CHUNK
    nc = l // c
    cidx = (lambda bi, ch: (bi, nc - 1 - ch, 0)) if rev else (lambda bi, ch: (bi, ch, 0))
    seq = pl.BlockSpec((1, c, w), cidx)
    st = pl.BlockSpec((1,) + state.shape[1:], lambda bi, ch: (bi, 0, 0, 0))
    args = [r, lw, kd, v, kk, a, state] + ([acc] if acc is not None else [])
    y, s = pl.pallas_call(
        functools.partial(_rwkv_seq_kernel, rev=rev, has_acc=acc is not None),
        grid=(b, nc),
        in_specs=[seq] * 6 + [st] + ([seq] if acc is not None else []),
        out_specs=[seq, st],
        out_shape=[jax.ShapeDtypeStruct((b, l, w), F32), jax.ShapeDtypeStruct(state.shape, F32)],
        compiler_params=_params(("parallel", "arbitrary")),
        name="rwkv_seq_rev" if rev else "rwkv_seq_fwd",
    )(*args)
    return y, s


_PV_ROWS = ('mu_r', 'mu_k', 'mu_v', 'w0', 'a0', 'kk', 'ka', 'rk', 'v0')


def _rwkv_grid_kernel(*refs, rev, nc, has_vf, has_acc, emit_vf):
    it = iter(refs)
    pr, pr_u, pr_d, pk, pk_u, pk_d, pv, pv_u, pv_d, wa, wa_u, wa_d = (next(it) for _ in range(12))
    vf_ref, lora_ref = (next(it), next(it)) if has_vf else (None, None)
    pvec, mu_wa, w2_ref, a2_ref, s0_ref = (next(it) for _ in range(5))
    accy_ref, accb_ref = (next(it), next(it)) if has_acc else (None, None)
    y_ref, bon_ref, s_ref = (next(it) for _ in range(3))
    vfo_ref = next(it) if emit_vf else None
    step = pl.program_id(1)

    @pl.when(step == 0)
    def _():
        s_ref[...] = s0_ref[...]

    c = pr.shape[1]
    w = pr.shape[2]
    rd = w // 16
    ci = nc - 1 - step if rev else step
    has_up = jnp.where(ci > 0, 1.0, 0.0)
    has_dn = jnp.where(ci < nc - 1, 1.0, 0.0)
    trow = lax.broadcasted_iota(jnp.int32, (c, 1), 0)
    prow = lambda name: pvec[_PV_ROWS.index(name):_PV_ROWS.index(name) + 1, :]

    def left(x):
        return jnp.where(trow == 0, 0.0, pltpu.roll(x, 1, 0))

    def right(x):
        return jnp.where(trow == c - 1, 0.0, pltpu.roll(x, c - 1, 0))

    ones = _seg_ones(RK_HD)
    d = 1 if rev else 0
    rows, bons = [], []
    for bi in range(pr.shape[0]):
        def lerp_wide(cur_ref, up_ref, dn_ref, mu):
            cur = cur_ref[bi]
            q = w // 4
            sh = jnp.concatenate([left(cur[:, :q]), right(cur[:, q:2 * q]),
                                  up_ref[bi] * has_up, dn_ref[bi] * has_dn], axis=1)
            return cur + mu * (sh - cur)

        r = lerp_wide(pr, pr_u, pr_d, prow('mu_r'))
        k = lerp_wide(pk, pk_u, pk_d, prow('mu_k'))
        v = lerp_wide(pv, pv_u, pv_d, prow('mu_v'))

        cur = wa[bi]
        qd = (lax.broadcasted_iota(jnp.int32, cur.shape, 1) & (rd - 1)) >> ((rd // 4).bit_length() - 1)
        sh = jnp.where(qd == 0, left(cur), jnp.where(qd == 1, right(cur),
                                                     jnp.where(qd == 2, wa_u[bi] * has_up, wa_d[bi] * has_dn)))
        xwa = cur + mu_wa[...] * (sh - cur)
        xw = xwa[:, d * rd:(d + 1) * rd]
        xa = xwa[:, (2 + d) * rd:(3 + d) * rd]
        z = -(prow('w0') + _nn(_mx(jnp.tanh(xw)), w2_ref[...]))
        softplus = jnp.maximum(z, 0.0) + jnp.log(1.0 + jnp.exp(-jnp.abs(z)))
        lw = -jnp.exp(-softplus - 0.5)
        a = _sigmoid(prow('a0') + _nn(_mx(xa), a2_ref[...]))
        if has_vf:
            v = v + (vf_ref[bi] - v) * _sigmoid(prow('v0') + lora_ref[bi])
        if emit_vf:
            vfo_ref[bi] = v
        kkn = k * prow('kk')
        kk = kkn / jnp.maximum(jnp.sqrt(_segsum(kkn * kkn, ones)), 1e-12)
        kd = k * (1.0 + (a - 1.0) * prow('ka'))
        bons.append(_segsum(r * kd * prow('rk'), ones) * v)
        rows.append((r, lw, kd, v, kk, a))
    ys = _rwkv_core(rows, s_ref, rev)
    for bi, (y, bon) in enumerate(zip(ys, bons)):
        if has_acc:
            y = y + accy_ref[bi]
            bon = bon + accb_ref[bi]
        y_ref[bi] = y
        bon_ref[bi] = bon


def _rwkv_scan_grid(pr, pk, pv, wa, pvec, mu_wa, w2, a2, state, rev, vf=None, lora=None,
                    acc=None, emit_vf=False):
    b, l, _ = pr[0].shape
    w = pvec.shape[1]
    c = RK_CHUNK
    assert c == GRID_W
    nc = l // c
    wq = w // 4
    ch = (lambda s: nc - 1 - s) if rev else (lambda s: s)
    cur = lambda lane: (lambda bi, s: (bi, ch(s), lane))
    upi = lambda lane: (lambda bi, s: (bi, jnp.maximum(ch(s) - 1, 0), lane))
    dni = lambda lane: (lambda bi, s: (bi, jnp.minimum(ch(s) + 1, nc - 1), lane))
    bb = RK_BATCH if b % RK_BATCH == 0 else 1
    seq = pl.BlockSpec((bb, c, w), cur(0))
    wide = lambda x: [pl.BlockSpec((bb, c, w), cur(x[1] // w)), pl.BlockSpec((bb, c, wq), upi(x[1] // wq + 2)),
                      pl.BlockSpec((bb, c, wq), dni(x[1] // wq + 3))]
    wa_w = mu_wa.shape[1]
    small = [pl.BlockSpec((bb, c, wa_w), cur(wa[1] // wa_w)), pl.BlockSpec((bb, c, wa_w), upi(wa[1] // wa_w)),
             pl.BlockSpec((bb, c, wa_w), dni(wa[1] // wa_w))]
    full = lambda x: pl.BlockSpec(x.shape, lambda bi, s: (0,) * x.ndim)
    st = pl.BlockSpec((bb,) + state.shape[1:], lambda bi, s: (bi, 0, 0, 0))
    in_specs = wide(pr) + wide(pk) + wide(pv) + small
    args = [pr[0]] * 3 + [pk[0]] * 3 + [pv[0]] * 3 + [wa[0]] * 3
    if vf is not None:
        in_specs += [seq, seq]
        args += [vf, lora]
    in_specs += [full(pvec), full(mu_wa), full(w2), full(a2), st]
    args += [pvec, mu_wa, w2, a2, state]
    if acc is not None:
        in_specs += [seq, seq]
        args += list(acc)
    out_specs = [seq, seq, st]
    out_shape = [jax.ShapeDtypeStruct((b, l, w), F32), jax.ShapeDtypeStruct((b, l, w), F32),
                 jax.ShapeDtypeStruct(state.shape, F32)]
    if emit_vf:
        out_specs.append(seq)
        out_shape.append(jax.ShapeDtypeStruct((b, l, w), F32))
    return pl.pallas_call(
        functools.partial(_rwkv_grid_kernel, rev=rev, nc=nc, has_vf=vf is not None,
                          has_acc=acc is not None, emit_vf=emit_vf),
        grid=(b // bb, nc),
        in_specs=in_specs,
        out_specs=out_specs,
        out_shape=out_shape,
        compiler_params=_params(("parallel", "arbitrary")),
        name="rwkv_grid_rev" if rev else "rwkv_grid_fwd",
    )(*args)


def _hgrn_kernel(*refs, rev, heads, hd, has_acc):
    q_ref, f_ref, v_ref, fb_ref, lb_ref, s0_ref = refs[:6]
    acc_ref = refs[6] if has_acc else None
    o_ref, s_ref = refs[6 + has_acc:]

    @pl.when(pl.program_id(1) == 0)
    def _():
        s_ref[...] = s0_ref[...]

    tt = q_ref.shape[1]
    sub = HG_SUB
    sh = sub.bit_length() - 1
    nsub = tt // sub
    mid = sub // 2 if rev else sub // 2 - 1
    last = 0 if rev else sub - 1
    ri = lax.broadcasted_iota(jnp.int32, (tt, tt), 0)
    ci = lax.broadcasted_iota(jnp.int32, (tt, tt), 1)
    causal = ((ri >> sh) == (ci >> sh)) & ((ci >= ri) if rev else (ci <= ri))
    lb = lb_ref[...]
    f = lb + (1.0 - lb) * _sigmoid(f_ref[0] + fb_ref[...])
    g_all = _cumsum_mm(causal.astype(F32).astype(MXU_DT), jnp.log(f))

    def row_of_each_sub(r):
        return jnp.concatenate([jnp.broadcast_to(g_all[j * sub + r:j * sub + r + 1], (sub, g_all.shape[1]))
                                for j in range(nsub)], axis=0)

    g_mid, g_last = row_of_each_sub(mid), row_of_each_sub(last)
    k_all = 1.0 - f
    q_all = _silu(q_ref[0])
    q1 = _mx(q_all * jnp.exp(g_all - g_mid))
    k1 = _mx(k_all * jnp.exp(g_mid - g_all))
    qg_f = q_all * jnp.exp(g_all)
    kl_f = k_all * jnp.exp(g_last - g_all)
    v_all = _mx(v_ref[0])
    assert nsub % 2 == 0
    is_first = lambda j: (j % 2 == 1) == rev
    dec = [jnp.exp(g_all[j * sub + last:j * sub + last + 1]) for j in range(nsub)]
    one_row = jnp.ones_like(dec[0])
    per_sub = lambda vals: jnp.concatenate([jnp.broadcast_to(x, (sub, x.shape[1])) for x in vals], axis=0)
    qg, kl = _mx(qg_f), _mx(kl_f)
    qg2 = _mx(qg_f * per_sub([one_row if is_first(j) else dec[j ^ 1] for j in range(nsub)]))
    kl2 = _mx(kl_f * per_sub([dec[j ^ 1] if is_first(j) else one_row for j in range(nsub)]))
    sub_r, sub_c = ri >> sh, ci >> sh
    second_sees_first = ((sub_r ^ 1) == sub_c) & ((sub_r & 1) == (0 if rev else 1))
    lanes = [slice(h * hd, (h + 1) * hd) for h in range(heads)]
    zero = jnp.zeros((), F32)
    att = [_mx(jnp.where(causal, _nt(q1[:, sl], k1[:, sl]), zero)
               + jnp.where(second_sees_first, _nt(qg[:, sl], kl[:, sl]), zero)) for sl in lanes]
    intra = [_nn(att[h], v_all[:, lanes[h]]) for h in range(heads)]
    pairs = [nsub // 2 - 1 - s if rev else s for s in range(nsub // 2)]
    rows = [slice(2 * p * sub, 2 * (p + 1) * sub) for p in range(nsub // 2)]
    upd = {(p, h): _tn(v_all[rows[p], lanes[h]], kl2[rows[p], lanes[h]]) for p in pairs for h in range(heads)}
    s_mem = [s_ref[0, h] for h in range(heads)]
    for p in pairs:
        dec_pair = dec[2 * p] * dec[2 * p + 1]
        outs = []
        for h in range(heads):
            outs.append(intra[h][rows[p]] + _nt(qg2[rows[p], lanes[h]], _mx(s_mem[h])))
            s_mem[h] = s_mem[h] * dec_pair[:, lanes[h]] + upd[(p, h)]
        o = jnp.concatenate(outs, axis=1)
        if has_acc:
            o = o + acc_ref[0, rows[p], :]
        o_ref[0, rows[p], :] = o
    for h in range(heads):
        s_ref[0, h] = s_mem[h]


def _hgrn_scan(pq, pf, pi, f_b, lb, state, rev, acc=None):
    b, l, _ = pq[0].shape
    w = f_b.shape[1]
    heads = HG_HEADS
    hd = w // heads
    tt = min(HG_TILE, l)
    nc = l // tt
    d = 1 if rev else 0
    ch = (lambda s: nc - 1 - s) if rev else (lambda s: s)
    col = lambda x, extra=0: pl.BlockSpec((1, tt, w), lambda bi, s: (bi, ch(s), x[1] // w + extra))
    seq = pl.BlockSpec((1, tt, w), lambda bi, s: (bi, ch(s), 0))
    vec = pl.BlockSpec((1, w), lambda bi, s: (0, 0))
    st = pl.BlockSpec((1,) + state.shape[1:], lambda bi, s: (bi, 0, 0, 0))
    args = [pq[0], pf[0], pi[0], f_b, lb, state] + ([acc] if acc is not None else [])
    o, s = pl.pallas_call(
        functools.partial(_hgrn_kernel, rev=rev, heads=heads, hd=hd, has_acc=acc is not None),
        grid=(b, nc),
        in_specs=[col(pq), col(pf, d), col(pi), vec, vec, st] + ([seq] if acc is not None else []),
        out_specs=[seq, st],
        out_shape=[jax.ShapeDtypeStruct((b, l, w), F32), jax.ShapeDtypeStruct(state.shape, F32)],
        compiler_params=_params(("parallel", "arbitrary")),
        name="hgrn_rev" if rev else "hgrn_fwd",
    )(*args)
    return o, s


_EV_ROWS = ('ml_norm_w', 'rk_ln_w', 'rk_ln_b', 'hg_norm_w', 'gate_b0', 'gate_b1', 'gate_b2', 'final_w')


def _epilogue_kernel(hm_ref, mo_ref, mz_ref, yr_ref, bon_ref, rz_ref, oh_ref, hz_ref,
                     g0_ref, g1_ref, g2_ref, xs_ref, gx_ref, ev_ref,
                     wpm_ref, wpr_ref, wph_ref, wout_ref, o_ref, *, final):
    ev = lambda name: ev_ref[_EV_ROWS.index(name):_EV_ROWS.index(name) + 1, :]
    d = hm_ref.shape[2]

    def seg_mean(x, seg):
        return _segsum(x, _seg_ones(seg)) * (1.0 / seg)

    seg = d // ML_HEADS
    y = hm_ref[0]
    y = y - seg_mean(y, seg)
    y = y * lax.rsqrt(seg_mean(y * y, seg) + NORM_EPS) * ev('ml_norm_w')
    u_m = _sigmoid(mo_ref[0].astype(F32)) * y * _silu(mz_ref[0].astype(F32))
    y = yr_ref[0]
    y = y - seg_mean(y, RK_HD)
    y = y * lax.rsqrt(seg_mean(y * y, RK_HD) + RK_GN_EPS) * ev('rk_ln_w') + ev('rk_ln_b') + bon_ref[0]
    u_r = y * _silu(rz_ref[0].astype(F32))
    seg = d // HG_HEADS
    o = oh_ref[0]
    o = o * lax.rsqrt(seg_mean(o * o, seg) + NORM_EPS) * ev('hg_norm_w')
    u_h = o * _silu(hz_ref[0].astype(F32))

    merged = (_sigmoid(g0_ref[0].astype(F32) + ev('gate_b0')) * _nn(_mx(u_m), wpm_ref[...])
              + _sigmoid(g1_ref[0].astype(F32) + ev('gate_b1')) * _nn(_mx(u_r), wpr_ref[...])
              + _sigmoid(g2_ref[0].astype(F32) + ev('gate_b2')) * _nn(_mx(u_h), wph_ref[...]))
    xs = xs_ref[0] + gx_ref[0] * _nn(_mx(merged), wout_ref[...])
    if final:
        xs = xs * lax.rsqrt(jnp.mean(xs * xs, axis=-1, keepdims=True) + NORM_EPS) * ev('final_w')
    o_ref[0] = xs


def _epilogue(hm, mo, mz, yr, bon, rz, oh, hz, gate, xs, gx, p, final_w):
    b, l, d = xs.shape
    tt = min(EP_TILE, l)
    seq = pl.BlockSpec((1, tt, d), lambda bi, i: (bi, i, 0))
    col = lambda x, extra=0: pl.BlockSpec((1, tt, d), lambda bi, i: (bi, i, x[1] // d + extra))
    ev = jnp.stack([p['ml_norm_w'], p['rk_ln_w'], p['rk_ln_b'], p['hg_norm_w'],
                    p['gate_b'][0], p['gate_b'][1], p['gate_b'][2],
                    final_w if final_w is not None else jnp.ones((d,), F32)], axis=0)
    wspec = pl.BlockSpec((d, d), lambda bi, i: (0, 0))
    ws = [_mx(p[n]) for n in ('w_pm', 'w_pr', 'w_ph', 'w_out')]
    return pl.pallas_call(
        functools.partial(_epilogue_kernel, final=final_w is not None),
        grid=(b, l // tt),
        in_specs=[seq, col(mo), col(mz), seq, seq, col(rz), seq, col(hz), col(gate), col(gate, 1), col(gate, 2), seq,
                  pl.BlockSpec((1, 1, d), lambda bi, i: (bi, 0, 0)),
                  pl.BlockSpec(ev.shape, lambda bi, i: (0, 0))] + [wspec] * 4,
        out_specs=seq,
        out_shape=jax.ShapeDtypeStruct((b, l, d), F32),
        compiler_params=_params(("parallel", "parallel")),
        name="epilogue",
    )(hm, mo[0], mz[0], yr, bon, rz[0], oh, hz[0], gate[0], gate[0], gate[0], xs, gx, ev, *ws)


def _rms_norm(x, w):
    return x * lax.rsqrt(jnp.mean(x * x, axis=-1, keepdims=True) + NORM_EPS) * w


def _bi_shift_seq(u):
    half = u.shape[-1] // 2
    g = jnp.pad(u, ((0, 0), (1, 1), (0, 0)))
    return jnp.concatenate((g[:, :-2, :half], g[:, 2:, half:]), axis=-1)


def _in_layout(d):
    rd = d // 16
    return (('m_q', d), ('m_k', d), ('m_v', d), ('m_o', d), ('m_z', d), ('m_if', 4 * ML_HEADS),
            ('r_r', d), ('r_k', d), ('r_v', d), ('r_z', d), ('r_wa', 4 * rd),
            ('h_q', d), ('h_f', 2 * d), ('h_i', d), ('h_z', d), ('gate', 3 * d))


_F32_COLS = ('m_q', 'm_k', 'r_r', 'r_k', 'r_v', 'h_q', 'h_f', 'r_wa', 'm_if')
_MXU_COLS = ('m_v', 'h_i', 'm_o', 'm_z', 'r_z', 'h_z', 'gate')
_COL_ALIGN = 512


def _pack_w_in(w_in):
    d = w_in.shape[0]
    src, off = {}, 0
    for name, width in _in_layout(d):
        src[name] = (off, width)
        off += width

    def pack(names):
        cols, first, o = [], {}, 0
        for n in names:
            cols.append(w_in[:, src[n][0]:src[n][0] + src[n][1]])
            first[n] = o
            o += src[n][1]
        if o % _COL_ALIGN:
            cols.append(jnp.zeros((d, -o % _COL_ALIGN), w_in.dtype))
        return jnp.concatenate(cols, axis=1).astype(MXU_DT), first

    return pack(_F32_COLS) + pack(_MXU_COLS)


def _take(col, width):
    return col[0][..., col[1]:col[1] + width]


def _zero_states(b, d):
    ml_hd = d // ML_HEADS
    ml = (jnp.zeros((b, ML_HEADS, ml_hd, ml_hd), F32), jnp.zeros((b, ML_HEADS, 1, ml_hd), F32),
          jnp.zeros((b, ML_HEADS, 8, 128), F32))
    rk = jnp.zeros((b, d // RK_GROUP, RK_GROUP, RK_GROUP), F32)
    hg_hd = d // HG_HEADS
    hg = jnp.zeros((b, HG_HEADS, hg_hd, hg_hd), F32)
    return ((ml, ml), (rk, rk), (hg, hg))


def _mlstm_branch(proj, p, init, b, l):
    gl = _take(proj('m_if'), 4 * ML_HEADS).reshape(b, l, 2, 2, ML_HEADS) + p['ml_if_b']
    acc, states = None, []
    for dr in (0, 1):
        log_i = gl[:, :, dr, 0]
        log_f = jax.nn.log_sigmoid(gl[:, :, dr, 1])
        acc, st = _mlstm_scan(proj('m_q'), proj('m_k'), proj('m_v'), p['ml_conv'], log_i, log_f,
                              init[dr], rev=bool(dr), acc=acc)
        states.append(st)
    return acc, tuple(states)


def _rwkv7_seq_branch(proj, h2, p, init, v_first, b, l, d):
    nh, n = d // RK_HD, RK_HD
    rd = d // 16
    mu = p['rk_mu']

    def lerp_shift(u, m):
        return u + m * (_bi_shift_seq(u) - u)

    r = lerp_shift(_take(proj('r_r'), d), mu[0:d])
    k = lerp_shift(_take(proj('r_k'), d), mu[d:2 * d])
    v = lerp_shift(_take(proj('r_v'), d), mu[2 * d:3 * d])
    pwa = _take(proj('r_wa'), 4 * rd)
    xwa = jnp.concatenate([lerp_shift(pwa[..., i * rd:(i + 1) * rd], mu[3 * d + i * rd:3 * d + (i + 1) * rd])
                           for i in range(4)], axis=-1)
    if p['rk_v0'] is None:
        v_first = v
    else:
        lora = _mm(_mm(h2, p['rk_v1']), p['rk_v2']).reshape(b, l, d)
        v = v + (v_first - v) * jax.nn.sigmoid(p['rk_v0'] + lora)
    kk = (k * p['rk_kk']).reshape(b, l, nh, n)
    kk = kk / jnp.maximum(jnp.sqrt(jnp.sum(kk * kk, axis=-1, keepdims=True)), 1e-12)
    kk = kk.reshape(b, l, d)
    acc, states, kd_sum = None, [], 0.0
    for dr in (0, 1):
        xw = xwa[..., dr * rd:(dr + 1) * rd]
        xa = xwa[..., (2 + dr) * rd:(3 + dr) * rd]
        log_w = -jax.nn.softplus(-(p['rk_w0'][dr] + _mm(jnp.tanh(xw).reshape(b * l, rd), p['rk_w2'][dr]).reshape(b, l, d))) - 0.5
        lw = -jnp.exp(log_w)
        a = jax.nn.sigmoid(p['rk_a0'][dr] + _mm(xa.reshape(b * l, rd), p['rk_a2'][dr]).reshape(b, l, d))
        kd = k * (1.0 + (a - 1.0) * p['rk_ka'])
        acc, s = _rwkv_scan_seq(r, lw, kd, v, kk, a, init[dr], rev=bool(dr), acc=acc)
        states.append(s)
        kd_sum = kd_sum + kd
    bonus = jnp.sum((r * kd_sum * p['rk_rk']).reshape(b, l, nh, n), axis=-1, keepdims=True) * v.reshape(b, l, nh, n)
    return acc, bonus.reshape(b, l, d), tuple(states), v_first


def _rwkv7_grid_branch(proj, h2, p, init, v_first, b, l, d):
    mu = p['rk_mu']
    zeros = jnp.zeros((d,), F32)
    has_vf = p['rk_v0'] is not None
    lora = _mm(_mm(h2, p['rk_v1']), p['rk_v2']).reshape(b, l, d) if has_vf else None
    acc, states, vf_out = None, [], v_first
    for dr in (0, 1):
        rows = {'mu_r': mu[0:d], 'mu_k': mu[d:2 * d], 'mu_v': mu[2 * d:3 * d], 'w0': p['rk_w0'][dr],
                'a0': p['rk_a0'][dr], 'kk': p['rk_kk'], 'ka': p['rk_ka'], 'rk': p['rk_rk'],
                'v0': p['rk_v0'] if has_vf else zeros}
        pvec = jnp.stack([rows[n] for n in _PV_ROWS], axis=0)
        emit_vf = (not has_vf) and dr == 0
        out = _rwkv_scan_grid(proj('r_r'), proj('r_k'), proj('r_v'), proj('r_wa'), pvec, mu[3 * d:][None],
                              _mx(p['rk_w2'][dr]), _mx(p['rk_a2'][dr]), init[dr], rev=bool(dr),
                              vf=v_first if has_vf else None, lora=lora, acc=acc, emit_vf=emit_vf)
        acc = (out[0], out[1])
        states.append(out[2])
        if emit_vf:
            vf_out = out[3]
    return acc[0], acc[1], tuple(states), vf_out


def _hgrn2_branch(proj, p, init):
    acc, states = None, []
    for dr in (0, 1):
        acc, s = _hgrn_scan(proj('h_q'), proj('h_f'), proj('h_i'), p['hg_f_b'][dr][None], p['hg_lb'][dr][None],
                            init[dr], rev=bool(dr), acc=acc)
        states.append(s)
    return acc, tuple(states)


def _mixer(h, p, init, on_grid, v_first, need_out, xs, gx, final_w):
    b, l, d = h.shape
    h2 = _mx(h.reshape(b * l, d))
    w_f32, col_f32, w_mx, col_mx = p['w_in']
    p_f32 = _mm(h2, w_f32).reshape(b, l, w_f32.shape[1])
    p_mx = _mm(h2, w_mx, MXU_DT).reshape(b, l, w_mx.shape[1])

    def proj(name):
        return (p_f32, col_f32[name]) if name in col_f32 else (p_mx, col_mx[name])

    hm, st_m = _mlstm_branch(proj, p, init[0], b, l)
    if on_grid:
        yr, bon, st_r, v_first = _rwkv7_grid_branch(proj, h2, p, init[1], v_first, b, l, d)
    else:
        yr, bon, st_r, v_first = _rwkv7_seq_branch(proj, h2, p, init[1], v_first, b, l, d)
    oh, st_h = _hgrn2_branch(proj, p, init[2])
    states = (st_m, st_r, st_h)
    if not need_out:
        return None, states, v_first
    out = _epilogue(hm, proj('m_o'), proj('m_z'), yr, bon, proj('r_z'), oh, proj('h_z'), proj('gate'),
                    xs, gx, p, final_w)
    return out, states, v_first


def kernel(x, c, ctx, c_ctx, norm_w, ada_w, ada_b, w_in, gate_b, ml_conv, ml_if_b, ml_norm_w, rk_mu, rk_w0, rk_w2, rk_a0, rk_a2, rk_kk, rk_ka, rk_rk, rk_v0, rk_v1, rk_v2, rk_ln_w, rk_ln_b, hg_f_b, hg_lb, hg_norm_w, w_pm, w_pr, w_ph, w_out, final_norm_w):
    batch, _, d = x.shape
    depth = w_in.shape[0]
    lb_p = jax.nn.softmax(hg_lb.astype(F32), axis=1)
    lower_bounds = jnp.cumsum(lb_p, axis=1) - lb_p[:, :1]
    xs, cs = x, ctx
    vf_x, vf_c = None, None
    cond = jnp.concatenate([jax.nn.silu(c), jax.nn.silu(c_ctx)[None]], axis=0)
    for l in range(depth):
        last = l == depth - 1
        p = {'w_in': _pack_w_in(w_in[l]), 'gate_b': gate_b[l], 'ml_conv': ml_conv[l], 'ml_if_b': ml_if_b[l],
             'ml_norm_w': ml_norm_w[l], 'rk_mu': rk_mu[l], 'rk_w0': rk_w0[l], 'rk_w2': rk_w2[l],
             'rk_a0': rk_a0[l], 'rk_a2': rk_a2[l], 'rk_kk': rk_kk[l], 'rk_ka': rk_ka[l], 'rk_rk': rk_rk[l],
             'rk_v0': rk_v0[l - 1] if l > 0 else None, 'rk_v1': rk_v1[l - 1] if l > 0 else None,
             'rk_v2': rk_v2[l - 1] if l > 0 else None, 'rk_ln_w': rk_ln_w[l], 'rk_ln_b': rk_ln_b[l],
             'hg_f_b': hg_f_b[l], 'hg_lb': lower_bounds[:, l], 'hg_norm_w': hg_norm_w[l],
             'w_pm': w_pm[l], 'w_pr': w_pr[l], 'w_ph': w_ph[l], 'w_out': w_out[l]}
        mod = _mm(cond, ada_w[l]) + ada_b[l]
        shift_x, scale_x, gate_x = jnp.split(mod[:batch, None, :], 3, axis=-1)
        shift_c, scale_c, gate_c = jnp.split(mod[batch], 3, axis=-1)
        hc = _rms_norm(cs, norm_w[l]) * (1.0 + scale_c) + shift_c
        gc = jnp.broadcast_to(gate_c[None, None, :], (batch, 1, d))
        cs_new, st_c, vf_c = _mixer(hc, p, _zero_states(batch, d), False, vf_c, not last, cs, gc, None)
        hx = _rms_norm(xs, norm_w[l]) * (1.0 + scale_x) + shift_x
        xs, _, vf_x = _mixer(hx, p, st_c, True, vf_x, True, xs, gate_x, final_norm_w if last else None)
        if not last:
            cs = cs_new
    return xs
```

```python
import functools

import jax
import jax.numpy as jnp
from jax import lax
from jax.experimental import pallas as pl
from jax.experimental.pallas import tpu as pltpu

F32 = jnp.float32
MXU_DT = jnp.bfloat16

NORM_EPS = 1e-6
GRID_W = 64
ML_HEADS = 4
ML_CHUNK = 256
RK_HD = 64
RK_CHUNK = 64
RK_GROUP = 256
RK_BATCH = 4
RK_GN_EPS = 64e-5
HG_HEADS = 8
HG_SUB = 16
HG_TILE = 128
EP_TILE = 256
SEG_W = 256
VMEM_LIMIT = 56 * 1024 * 1024


def _nt(a, b):
    return lax.dot_general(a, b, (((1,), (1,)), ((), ())), preferred_element_type=F32)


def _tn(a, b):
    return lax.dot_general(a, b, (((0,), (0,)), ((), ())), preferred_element_type=F32)


def _nn(a, b):
    return jnp.dot(a, b, preferred_element_type=F32)


def _mx(a):
    return a.astype(MXU_DT)


def _split2(x):
    hi = x.astype(MXU_DT)
    return hi, (x - hi.astype(F32)).astype(MXU_DT)


def _cumsum_mm(tri, x):
    hi = x.astype(MXU_DT)
    r1 = x - hi.astype(F32)
    mid = r1.astype(MXU_DT)
    lo = (r1 - mid.astype(F32)).astype(MXU_DT)
    one = lambda t: _nn(t, hi) + _nn(t, mid) + _nn(t, lo)
    return [one(t) for t in tri] if isinstance(tri, (list, tuple)) else one(tri)


def _seg_ones(seg):
    sh = seg.bit_length() - 1
    ri = lax.broadcasted_iota(jnp.int32, (SEG_W, SEG_W), 0)
    ci = lax.broadcasted_iota(jnp.int32, (SEG_W, SEG_W), 1)
    return ((ri >> sh) == (ci >> sh)).astype(F32).astype(MXU_DT)


def _segsum(x, ones):
    outs = []
    for g in range(x.shape[1] // SEG_W):
        hi, lo = _split2(x[:, g * SEG_W:(g + 1) * SEG_W])
        outs.append(_nn(hi, ones) + _nn(lo, ones))
    return outs[0] if len(outs) == 1 else jnp.concatenate(outs, axis=1)


def _sigmoid(x):
    return jax.nn.sigmoid(x)


def _silu(x):
    return x * jax.nn.sigmoid(x)


def _params(sem):
    return pltpu.CompilerParams(dimension_semantics=sem, vmem_limit_bytes=VMEM_LIMIT)


def _mm_kernel(x_ref, w_ref, o_ref):
    o_ref[...] = _nn(_mx(x_ref[...]), w_ref[...]).astype(o_ref.dtype)


def _mm(x, w, out_dtype=F32):
    m, k = x.shape
    n = w.shape[1]
    mp = -(-m // 8) * 8
    npad = -(-n // 128) * 128
    if mp != m:
        x = jnp.pad(x, ((0, mp - m), (0, 0)))
    wb = w.astype(MXU_DT)
    if npad != n:
        wb = jnp.pad(wb, ((0, 0), (0, npad - n)))
    tm = next(t for t in (2048, 1024, 512, 256, 128, 64, 32, 16, 8) if mp % t == 0)
    tn = next(t for t in (1024, 512, 256, 128) if npad % t == 0)
    out = pl.pallas_call(
        _mm_kernel,
        grid=(mp // tm, npad // tn),
        in_specs=[pl.BlockSpec((tm, k), lambda i, j: (i, 0)),
                  pl.BlockSpec((k, tn), lambda i, j: (0, j))],
        out_specs=pl.BlockSpec((tm, tn), lambda i, j: (i, j)),
        out_shape=jax.ShapeDtypeStruct((mp, npad), out_dtype),
        compiler_params=_params(("parallel", "parallel")),
        name="mm",
    )(x, wb)
    if mp != m or npad != n:
        out = out[:m, :n]
    return out


def _mlstm_kernel(*refs, rev, heads, hd, nc, has_acc):
    (q_ref, qp_ref, qn_ref, k_ref, kp_ref, kn_ref, v_ref, cw_ref, gcol_ref, grow_ref,
     c0_ref, n0_ref, m0_ref) = refs[:13]
    acc_ref = refs[13] if has_acc else None
    h_ref, c_ref, n_ref, m_ref = refs[13 + has_acc:]
    step = pl.program_id(1)

    @pl.when(step == 0)
    def _():
        c_ref[...] = c0_ref[...]
        n_ref[...] = n0_ref[...]
        m_ref[...] = m0_ref[...]

    t = q_ref.shape[1]
    ci = nc - 1 - step if rev else step
    has_prev = jnp.where(ci > 0, 1.0, 0.0)
    has_next = jnp.where(ci < nc - 1, 1.0, 0.0)
    trow = lax.broadcasted_iota(jnp.int32, (t, 1), 0)

    def conv_silu(u, prev_blk, next_blk, w3):
        up = jnp.where(trow == 0, prev_blk[7:8, :] * has_prev, pltpu.roll(u, 1, 0))
        dn = jnp.where(trow == t - 1, next_blk[0:1, :] * has_next, pltpu.roll(u, t - 1, 0))
        return _silu(up * w3[0:1, :] + u * w3[1:2, :] + dn * w3[2:3, :])

    q_all = conv_silu(q_ref[0], qp_ref[0], qn_ref[0], cw_ref[0:3, :]) * (hd ** -0.5)
    k_all = conv_silu(k_ref[0], kp_ref[0], kn_ref[0], cw_ref[3:6, :])

    row = lax.broadcasted_iota(jnp.int32, (t, t), 0)
    col = lax.broadcasted_iota(jnp.int32, (t, t), 1)
    mask = (col >= row) if rev else (col <= row)
    last = 0 if rev else t - 1
    hs = range(heads)
    lanes = [slice(h * hd, (h + 1) * hd) for h in hs]
    q32 = [q_all[:, sl] for sl in lanes]
    qc = [_mx(x) for x in q32]
    kc = [k_all[:, sl] for sl in lanes]
    vc = [_mx(v_ref[0, :, sl]) for sl in lanes]
    bc_col = [gcol_ref[0, :, h:h + 1] for h in hs]
    i_col = [gcol_ref[0, :, heads + h:heads + h + 1] for h in hs]
    m_prev = [m_ref[0, h, 0:1, 0:1] for h in hs]
    c_mem = [c_ref[0, h] for h in hs]
    n_mem = [n_ref[0, h] for h in hs]
    dmat = [jnp.where(mask, bc_col[h] - grow_ref[0, 0, h:h + 1, :] + grow_ref[0, 0, heads + h:heads + h + 1, :],
                      -jnp.inf) for h in hs]
    inter = [bc_col[h] + m_prev[h] for h in hs]
    m_t = [jnp.maximum(jnp.max(dmat[h], axis=-1, keepdims=True), inter[h]) for h in hs]
    qk = [_nt(qc[h], _mx(kc[h])) for h in hs]
    q_c = [_nn(qc[h], _mx(c_mem[h])) for h in hs]
    s = [qk[h] * jnp.exp(dmat[h] - m_t[h]) for h in hs]
    w_inter = [jnp.exp(inter[h] - m_t[h]) for h in hs]
    num = [_nn(_mx(s[h]), vc[h]) + w_inter[h] * q_c[h] for h in hs]
    for h in hs:
        qn = jnp.sum(q32[h] * n_mem[h], axis=-1, keepdims=True)
        den = jnp.sum(s[h], axis=-1, keepdims=True) + w_inter[h] * qn
        h_out = num[h] / jnp.maximum(jnp.abs(den), jnp.exp(-m_t[h]))
        if has_acc:
            h_out = h_out + acc_ref[0, :, lanes[h]]
        h_ref[0, :, lanes[h]] = h_out
    for h in hs:
        total = bc_col[h][last:last + 1, :]
        g_col = total - bc_col[h] + i_col[h]
        m_new = jnp.maximum(total + m_prev[h], jnp.max(g_col, axis=0, keepdims=True))
        wk = jnp.exp(g_col - m_new)
        dec = jnp.exp(total + m_prev[h] - m_new)
        kw = kc[h] * wk
        c_ref[0, h] = dec * c_mem[h] + _tn(_mx(kw), vc[h])
        n_ref[0, h] = dec * n_mem[h] + jnp.sum(kw, axis=0, keepdims=True)
        m_ref[0, h] = jnp.broadcast_to(m_new, m_ref.shape[2:])


def _mlstm_scan(pq, pk, pv, conv_w, log_i, log_f, state, rev, acc=None):
    b, l, _ = pq[0].shape
    w = conv_w.shape[-1]
    heads = ML_HEADS
    hd = w // heads
    t = min(ML_CHUNK, l)
    assert l % t == 0
    nc = l // t
    nb8 = l // 8
    lf = log_f.reshape(b, nc, t, heads)
    if rev:
        bcum = jnp.flip(jnp.cumsum(jnp.flip(lf, 2), axis=2), 2)
    else:
        bcum = jnp.cumsum(lf, axis=2)
    gcol = jnp.concatenate([bcum.reshape(b, l, heads), log_i], axis=-1)
    grow = gcol.reshape(b, nc, t, 2 * heads).transpose(0, 1, 3, 2)
    ch = (lambda c: nc - 1 - c) if rev else (lambda c: c)
    cidx = lambda bi, c: (bi, ch(c), 0)
    ridx = lambda bi, c: (bi, ch(c), 0, 0)
    sidx = lambda bi, c: (bi, 0, 0, 0)
    c0, n0, m0 = state
    seq = pl.BlockSpec((1, t, w), cidx)
    col = lambda x: pl.BlockSpec((1, t, w), lambda bi, c: (bi, ch(c), x[1] // w))
    prv = lambda x: pl.BlockSpec((1, 8, w), lambda bi, c: (bi, jnp.maximum(ch(c) * (t // 8) - 1, 0), x[1] // w))
    nxt = lambda x: pl.BlockSpec((1, 8, w), lambda bi, c: (bi, jnp.minimum((ch(c) + 1) * (t // 8), nb8 - 1), x[1] // w))
    st_specs = [pl.BlockSpec((1, heads, hd, hd), sidx), pl.BlockSpec((1, heads, 1, hd), sidx),
                pl.BlockSpec((1, heads, 8, 128), sidx)]
    cw = conv_w.reshape(6, w)
    in_specs = [col(pq), prv(pq), nxt(pq), col(pk), prv(pk), nxt(pk), col(pv),
                pl.BlockSpec((6, w), lambda bi, c: (0, 0)),
                pl.BlockSpec((1, t, 2 * heads), cidx), pl.BlockSpec((1, 1, 2 * heads, t), ridx)] + st_specs
    args = [pq[0], pq[0], pq[0], pk[0], pk[0], pk[0], pv[0], cw, gcol, grow, c0, n0, m0]
    if acc is not None:
        in_specs.append(seq)
        args.append(acc)
    out = pl.pallas_call(
        functools.partial(_mlstm_kernel, rev=rev, heads=heads, hd=hd, nc=nc, has_acc=acc is not None),
        grid=(b, nc),
        in_specs=in_specs,
        out_specs=[seq] + st_specs,
        out_shape=[jax.ShapeDtypeStruct((b, l, w), F32), jax.ShapeDtypeStruct(c0.shape, F32),
                   jax.ShapeDtypeStruct(n0.shape, F32), jax.ShapeDtypeStruct(m0.shape, F32)],
        compiler_params=_params(("parallel", "arbitrary")),
        name="mlstm_rev" if rev else "mlstm_fwd",
    )(*args)
    return out[0], (out[1], out[2], out[3])


def _rwkv_core(rows, s_ref, rev):
    c, w = rows[0][0].shape
    g_w = RK_GROUP
    nh = g_w // RK_HD
    assert c == RK_HD and c & (c - 1) == 0
    sh = c.bit_length() - 1
    rj = lax.broadcasted_iota(jnp.int32, (g_w, g_w), 0)
    cj = lax.broadcasted_iota(jnp.int32, (g_w, g_w), 1)
    blk = (rj >> sh) == (cj >> sh)
    tt, ss = rj & (c - 1), cj & (c - 1)
    incl = blk & ((ss >= tt) if rev else (ss <= tt))
    strict = blk & ((ss > tt) if rev else (ss < tt))
    eye = (rj == cj).astype(F32)
    tri = incl[:c, :c].astype(F32).astype(MXU_DT)
    lo_half = lax.broadcasted_iota(jnp.int32, (2 * g_w, 2 * c), 1) < c
    last = 0 if rev else c - 1
    zero = jnp.zeros((), F32)
    chains = [(bi, g) for bi in range(len(rows)) for g in range(w // g_w)]

    def bd(x):
        return _mx(jnp.where(blk, jnp.concatenate([x] * nh, axis=0), zero))

    def rsum(z):
        out = z[0:c]
        for i in range(1, nh):
            out = out + z[i * c:(i + 1) * c]
        return out

    a_bd, r_bd, v_bd, bk_t, r_t, v_g, bk_hat, dec = {}, {}, {}, {}, {}, {}, {}, {}
    for bi, (r, lw, kd, v, kk, a) in enumerate(rows):
        ka = kk * a
        gi = _cumsum_mm(tri, lw)
        tot = gi[last:last + 1, :]
        e_neg = jnp.exp(-gi)
        e_tail = jnp.exp(tot - gi)
        full = {'a': -kk * jnp.exp(gi - lw), 'b': ka * e_neg, 'k': kd * e_neg, 'r': r * jnp.exp(gi),
                'bh': ka * e_tail, 'kh': kd * e_tail, 'dec': jnp.exp(tot), 'v': v}
        for g in range(w // g_w):
            x = {n: t[:, g * g_w:(g + 1) * g_w] for n, t in full.items()}
            ch = (bi, g)
            a_bd[ch], r_bd[ch], v_bd[ch] = bd(x['a']), bd(x['r']), bd(x['v'])
            bk_t[ch] = _mx(jnp.concatenate([x['b'], x['k']], axis=0))
            bk_hat[ch] = _mx(jnp.concatenate([x['bh'], x['kh']], axis=0))
            r_t[ch], v_g[ch], dec[ch] = _mx(x['r']), x['v'], x['dec']

    l_ab, a_ak, a_rb, a_rk = {}, {}, {}, {}
    for ch in chains:
        prod = _nt(jnp.concatenate([a_bd[ch], r_bd[ch]], axis=0), bk_t[ch])
        swp = pltpu.roll(prod, c, 1)
        xb = jnp.where(lo_half, prod, swp)
        xk = jnp.where(lo_half, swp, prod)
        xb = jnp.concatenate([xb, xb], axis=1)
        xk = jnp.concatenate([xk, xk], axis=1)
        l_ab[ch] = jnp.where(strict, xb[:g_w], zero)
        a_ak[ch] = _mx(jnp.where(strict, xk[:g_w], zero))
        a_rb[ch] = _mx(rsum(jnp.where(incl, xb[g_w:], zero)))
        a_rk[ch] = _mx(rsum(jnp.where(incl, xk[g_w:], zero)))

    xs = {ch: _mx(l_ab[ch]) for ch in chains}
    ps = {ch: eye + l_ab[ch] for ch in chains}
    for _ in range(max(1, (c - 1).bit_length() - 1)):
        xs = {ch: _mx(_nn(xs[ch], xs[ch])) for ch in chains}
        ps = {ch: ps[ch] + _nn(_mx(ps[ch]), xs[ch]) for ch in chains}

    t_cat = {ch: _mx(rsum(ps[ch])) for ch in chains}
    w_cat = {ch: _mx(_nn(t_cat[ch], a_bd[ch])) for ch in chains}
    ta = {ch: _mx(_nn(t_cat[ch], a_ak[ch])) for ch in chains}
    u0 = {ch: _nn(ta[ch], v_bd[ch]) for ch in chains}
    y_k = {ch: _nn(a_rk[ch], v_bd[ch]) for ch in chains}
    s_mem = {ch: s_ref[ch[0], ch[1]] for ch in chains}
    s_b = {ch: _mx(s_mem[ch]) for ch in chains}
    u = {ch: _nt(w_cat[ch], s_b[ch]) + u0[ch] for ch in chains}
    y = {ch: _nt(r_t[ch], s_b[ch]) + _nn(a_rb[ch], bd(u[ch])) + y_k[ch] for ch in chains}
    for ch in chains:
        upd = _tn(_mx(jnp.concatenate([u[ch], v_g[ch]], axis=0)), bk_hat[ch])
        s_ref[ch[0], ch[1]] = s_mem[ch] * dec[ch] + jnp.where(blk, upd, zero)
    return [jnp.concatenate([y[(bi, g)] for g in range(w // g_w)], axis=1) for bi in range(len(rows))]


def _rwkv_seq_kernel(*refs, rev, has_acc):
    r_ref, lw_ref, kd_ref, v_ref, kk_ref, a_ref, s0_ref = refs[:7]
    acc_ref = refs[7] if has_acc else None
    y_ref, s_ref = refs[7 + has_acc:]

    @pl.when(pl.program_id(1) == 0)
    def _():
        s_ref[...] = s0_ref[...]

    y, = _rwkv_core([(r_ref[0], lw_ref[0], kd_ref[0], v_ref[0], kk_ref[0], a_ref[0])], s_ref, rev)
    if has_acc:
        y = y + acc_ref[0]
    y_ref[0] = y


def _rwkv_scan_seq(r, lw, kd, v, kk, a, state, rev, acc=None):
    b, l, w = r.shape
    c = RK_CHUNK
    nc = l // c
    cidx = (lambda bi, ch: (bi, nc - 1 - ch, 0)) if rev else (lambda bi, ch: (bi, ch, 0))
    seq = pl.BlockSpec((1, c, w), cidx)
    st = pl.BlockSpec((1,) + state.shape[1:], lambda bi, ch: (bi, 0, 0, 0))
    args = [r, lw, kd, v, kk, a, state] + ([acc] if acc is not None else [])
    y, s = pl.pallas_call(
        functools.partial(_rwkv_seq_kernel, rev=rev, has_acc=acc is not None),
        grid=(b, nc),
        in_specs=[seq] * 6 + [st] + ([seq] if acc is not None else []),
        out_specs=[seq, st],
        out_shape=[jax.ShapeDtypeStruct((b, l, w), F32), jax.ShapeDtypeStruct(state.shape, F32)],
        compiler_params=_params(("parallel", "arbitrary")),
        name="rwkv_seq_rev" if rev else "rwkv_seq_fwd",
    )(*args)
    return y, s


_PV_ROWS = ('mu_r', 'mu_k', 'mu_v', 'w0', 'a0', 'kk', 'ka', 'rk', 'v0')


def _rwkv_grid_kernel(*refs, rev, nc, has_vf, has_acc, emit_vf):
    it = iter(refs)
    pr, pr_u, pr_d, pk, pk_u, pk_d, pv, pv_u, pv_d, wa, wa_u, wa_d = (next(it) for _ in range(12))
    vf_ref, lora_ref = (next(it), next(it)) if has_vf else (None, None)
    pvec, mu_wa, w2_ref, a2_ref, s0_ref = (next(it) for _ in range(5))
    accy_ref, accb_ref = (next(it), next(it)) if has_acc else (None, None)
    y_ref, bon_ref, s_ref = (next(it) for _ in range(3))
    vfo_ref = next(it) if emit_vf else None
    step = pl.program_id(1)

    @pl.when(step == 0)
    def _():
        s_ref[...] = s0_ref[...]

    c = pr.shape[1]
    w = pr.shape[2]
    rd = w // 16
    ci = nc - 1 - step if rev else step
    has_up = jnp.where(ci > 0, 1.0, 0.0)
    has_dn = jnp.where(ci < nc - 1, 1.0, 0.0)
    trow = lax.broadcasted_iota(jnp.int32, (c, 1), 0)
    prow = lambda name: pvec[_PV_ROWS.index(name):_PV_ROWS.index(name) + 1, :]

    def left(x):
        return jnp.where(trow == 0, 0.0, pltpu.roll(x, 1, 0))

    def right(x):
        return jnp.where(trow == c - 1, 0.0, pltpu.roll(x, c - 1, 0))

    ones = _seg_ones(RK_HD)
    d = 1 if rev else 0
    rows, bons = [], []
    for bi in range(pr.shape[0]):
        def lerp_wide(cur_ref, up_ref, dn_ref, mu):
            cur = cur_ref[bi]
            q = w // 4
            sh = jnp.concatenate([left(cur[:, :q]), right(cur[:, q:2 * q]),
                                  up_ref[bi] * has_up, dn_ref[bi] * has_dn], axis=1)
            return cur + mu * (sh - cur)

        r = lerp_wide(pr, pr_u, pr_d, prow('mu_r'))
        k = lerp_wide(pk, pk_u, pk_d, prow('mu_k'))
        v = lerp_wide(pv, pv_u, pv_d, prow('mu_v'))

        cur = wa[bi]
        qd = (lax.broadcasted_iota(jnp.int32, cur.shape, 1) & (rd - 1)) >> ((rd // 4).bit_length() - 1)
        sh = jnp.where(qd == 0, left(cur), jnp.where(qd == 1, right(cur),
                                                     jnp.where(qd == 2, wa_u[bi] * has_up, wa_d[bi] * has_dn)))
        xwa = cur + mu_wa[...] * (sh - cur)
        xw = xwa[:, d * rd:(d + 1) * rd]
        xa = xwa[:, (2 + d) * rd:(3 + d) * rd]
        z = -(prow('w0') + _nn(_mx(jnp.tanh(xw)), w2_ref[...]))
        softplus = jnp.maximum(z, 0.0) + jnp.log(1.0 + jnp.exp(-jnp.abs(z)))
        lw = -jnp.exp(-softplus - 0.5)
        a = _sigmoid(prow('a0') + _nn(_mx(xa), a2_ref[...]))
        if has_vf:
            v = v + (vf_ref[bi] - v) * _sigmoid(prow('v0') + lora_ref[bi])
        if emit_vf:
            vfo_ref[bi] = v
        kkn = k * prow('kk')
        kk = kkn / jnp.maximum(jnp.sqrt(_segsum(kkn * kkn, ones)), 1e-12)
        kd = k * (1.0 + (a - 1.0) * prow('ka'))
        bons.append(_segsum(r * kd * prow('rk'), ones) * v)
        rows.append((r, lw, kd, v, kk, a))
    ys = _rwkv_core(rows, s_ref, rev)
    for bi, (y, bon) in enumerate(zip(ys, bons)):
        if has_acc:
            y = y + accy_ref[bi]
            bon = bon + accb_ref[bi]
        y_ref[bi] = y
        bon_ref[bi] = bon


def _rwkv_scan_grid(pr, pk, pv, wa, pvec, mu_wa, w2, a2, state, rev, vf=None, lora=None,
                    acc=None, emit_vf=False):
    b, l, _ = pr[0].shape
    w = pvec.shape[1]
    c = RK_CHUNK
    assert c == GRID_W
    nc = l // c
    wq = w // 4
    ch = (lambda s: nc - 1 - s) if rev else (lambda s: s)
    cur = lambda lane: (lambda bi, s: (bi, ch(s), lane))
    upi = lambda lane: (lambda bi, s: (bi, jnp.maximum(ch(s) - 1, 0), lane))
    dni = lambda lane: (lambda bi, s: (bi, jnp.minimum(ch(s) + 1, nc - 1), lane))
    bb = RK_BATCH if b % RK_BATCH == 0 else 1
    seq = pl.BlockSpec((bb, c, w), cur(0))
    wide = lambda x: [pl.BlockSpec((bb, c, w), cur(x[1] // w)), pl.BlockSpec((bb, c, wq), upi(x[1] // wq + 2)),
                      pl.BlockSpec((bb, c, wq), dni(x[1] // wq + 3))]
    wa_w = mu_wa.shape[1]
    small = [pl.BlockSpec((bb, c, wa_w), cur(wa[1] // wa_w)), pl.BlockSpec((bb, c, wa_w), upi(wa[1] // wa_w)),
             pl.BlockSpec((bb, c, wa_w), dni(wa[1] // wa_w))]
    full = lambda x: pl.BlockSpec(x.shape, lambda bi, s: (0,) * x.ndim)
    st = pl.BlockSpec((bb,) + state.shape[1:], lambda bi, s: (bi, 0, 0, 0))
    in_specs = wide(pr) + wide(pk) + wide(pv) + small
    args = [pr[0]] * 3 + [pk[0]] * 3 + [pv[0]] * 3 + [wa[0]] * 3
    if vf is not None:
        in_specs += [seq, seq]
        args += [vf, lora]
    in_specs += [full(pvec), full(mu_wa), full(w2), full(a2), st]
    args += [pvec, mu_wa, w2, a2, state]
    if acc is not None:
        in_specs += [seq, seq]
        args += list(acc)
    out_specs = [seq, seq, st]
    out_shape = [jax.ShapeDtypeStruct((b, l, w), F32), jax.ShapeDtypeStruct((b, l, w), F32),
                 jax.ShapeDtypeStruct(state.shape, F32)]
    if emit_vf:
        out_specs.append(seq)
        out_shape.append(jax.ShapeDtypeStruct((b, l, w), F32))
    return pl.pallas_call(
        functools.partial(_rwkv_grid_kernel, rev=rev, nc=nc, has_vf=vf is not None,
                          has_acc=acc is not None, emit_vf=emit_vf),
        grid=(b // bb, nc),
        in_specs=in_specs,
        out_specs=out_specs,
        out_shape=out_shape,
        compiler_params=_params(("parallel", "arbitrary")),
        name="rwkv_grid_rev" if rev else "rwkv_grid_fwd",
    )(*args)


def _hgrn_kernel(*refs, rev, heads, hd, has_acc):
    q_ref, f_ref, v_ref, fb_ref, lb_ref, s0_ref = refs[:6]
    acc_ref = refs[6] if has_acc else None
    o_ref, s_ref = refs[6 + has_acc:]

    @pl.when(pl.program_id(1) == 0)
    def _():
        s_ref[...] = s0_ref[...]

    tt = q_ref.shape[1]
    sub = HG_SUB
    sh = sub.bit_length() - 1
    nsub = tt // sub
    mid = sub // 2 if rev else sub // 2 - 1
    last = 0 if rev else sub - 1
    ri = lax.broadcasted_iota(jnp.int32, (tt, tt), 0)
    ci = lax.broadcasted_iota(jnp.int32, (tt, tt), 1)
    causal = ((ri >> sh) == (ci >> sh)) & ((ci >= ri) if rev else (ci <= ri))
    lb = lb_ref[...]
    f = lb + (1.0 - lb) * _sigmoid(f_ref[0] + fb_ref[...])
    g_all = _cumsum_mm(causal.astype(F32).astype(MXU_DT), jnp.log(f))

    def row_of_each_sub(r):
        return jnp.concatenate([jnp.broadcast_to(g_all[j * sub + r:j * sub + r + 1], (sub, g_all.shape[1]))
                                for j in range(nsub)], axis=0)

    g_mid, g_last = row_of_each_sub(mid), row_of_each_sub(last)
    k_all = 1.0 - f
    q_all = _silu(q_ref[0])
    q1 = _mx(q_all * jnp.exp(g_all - g_mid))
    k1 = _mx(k_all * jnp.exp(g_mid - g_all))
    qg_f = q_all * jnp.exp(g_all)
    kl_f = k_all * jnp.exp(g_last - g_all)
    v_all = _mx(v_ref[0])
    assert nsub % 2 == 0
    is_first = lambda j: (j % 2 == 1) == rev
    dec = [jnp.exp(g_all[j * sub + last:j * sub + last + 1]) for j in range(nsub)]
    one_row = jnp.ones_like(dec[0])
    per_sub = lambda vals: jnp.concatenate([jnp.broadcast_to(x, (sub, x.shape[1])) for x in vals], axis=0)
    qg, kl = _mx(qg_f), _mx(kl_f)
    qg2 = _mx(qg_f * per_sub([one_row if is_first(j) else dec[j ^ 1] for j in range(nsub)]))
    kl2 = _mx(kl_f * per_sub([dec[j ^ 1] if is_first(j) else one_row for j in range(nsub)]))
    sub_r, sub_c = ri >> sh, ci >> sh
    second_sees_first = ((sub_r ^ 1) == sub_c) & ((sub_r & 1) == (0 if rev else 1))
    lanes = [slice(h * hd, (h + 1) * hd) for h in range(heads)]
    zero = jnp.zeros((), F32)
    att = [_mx(jnp.where(causal, _nt(q1[:, sl], k1[:, sl]), zero)
               + jnp.where(second_sees_first, _nt(qg[:, sl], kl[:, sl]), zero)) for sl in lanes]
    intra = [_nn(att[h], v_all[:, lanes[h]]) for h in range(heads)]
    pairs = [nsub // 2 - 1 - s if rev else s for s in range(nsub // 2)]
    rows = [slice(2 * p * sub, 2 * (p + 1) * sub) for p in range(nsub // 2)]
    upd = {(p, h): _tn(v_all[rows[p], lanes[h]], kl2[rows[p], lanes[h]]) for p in pairs for h in range(heads)}
    s_mem = [s_ref[0, h] for h in range(heads)]
    for p in pairs:
        dec_pair = dec[2 * p] * dec[2 * p + 1]
        outs = []
        for h in range(heads):
            outs.append(intra[h][rows[p]] + _nt(qg2[rows[p], lanes[h]], _mx(s_mem[h])))
            s_mem[h] = s_mem[h] * dec_pair[:, lanes[h]] + upd[(p, h)]
        o = jnp.concatenate(outs, axis=1)
        if has_acc:
            o = o + acc_ref[0, rows[p], :]
        o_ref[0, rows[p], :] = o
    for h in range(heads):
        s_ref[0, h] = s_mem[h]


def _hgrn_scan(pq, pf, pi, f_b, lb, state, rev, acc=None):
    b, l, _ = pq[0].shape
    w = f_b.shape[1]
    heads = HG_HEADS
    hd = w // heads
    tt = min(HG_TILE, l)
    nc = l // tt
    d = 1 if rev else 0
    ch = (lambda s: nc - 1 - s) if rev else (lambda s: s)
    col = lambda x, extra=0: pl.BlockSpec((1, tt, w), lambda bi, s: (bi, ch(s), x[1] // w + extra))
    seq = pl.BlockSpec((1, tt, w), lambda bi, s: (bi, ch(s), 0))
    vec = pl.BlockSpec((1, w), lambda bi, s: (0, 0))
    st = pl.BlockSpec((1,) + state.shape[1:], lambda bi, s: (bi, 0, 0, 0))
    args = [pq[0], pf[0], pi[0], f_b, lb, state] + ([acc] if acc is not None else [])
    o, s = pl.pallas_call(
        functools.partial(_hgrn_kernel, rev=rev, heads=heads, hd=hd, has_acc=acc is not None),
        grid=(b, nc),
        in_specs=[col(pq), col(pf, d), col(pi), vec, vec, st] + ([seq] if acc is not None else []),
        out_specs=[seq, st],
        out_shape=[jax.ShapeDtypeStruct((b, l, w), F32), jax.ShapeDtypeStruct(state.shape, F32)],
        compiler_params=_params(("parallel", "arbitrary")),
        name="hgrn_rev" if rev else "hgrn_fwd",
    )(*args)
    return o, s


_EV_ROWS = ('ml_norm_w', 'rk_ln_w', 'rk_ln_b', 'hg_norm_w', 'gate_b0', 'gate_b1', 'gate_b2', 'final_w')


def _epilogue_kernel(hm_ref, mo_ref, mz_ref, yr_ref, bon_ref, rz_ref, oh_ref, hz_ref,
                     g0_ref, g1_ref, g2_ref, xs_ref, gx_ref, ev_ref,
                     wpm_ref, wpr_ref, wph_ref, wout_ref, o_ref, *, final):
    ev = lambda name: ev_ref[_EV_ROWS.index(name):_EV_ROWS.index(name) + 1, :]
    d = hm_ref.shape[2]

    def seg_mean(x, seg):
        return _segsum(x, _seg_ones(seg)) * (1.0 / seg)

    seg = d // ML_HEADS
    y = hm_ref[0]
    y = y - seg_mean(y, seg)
    y = y * lax.rsqrt(seg_mean(y * y, seg) + NORM_EPS) * ev('ml_norm_w')
    u_m = _sigmoid(mo_ref[0].astype(F32)) * y * _silu(mz_ref[0].astype(F32))
    y = yr_ref[0]
    y = y - seg_mean(y, RK_HD)
    y = y * lax.rsqrt(seg_mean(y * y, RK_HD) + RK_GN_EPS) * ev('rk_ln_w') + ev('rk_ln_b') + bon_ref[0]
    u_r = y * _silu(rz_ref[0].astype(F32))
    seg = d // HG_HEADS
    o = oh_ref[0]
    o = o * lax.rsqrt(seg_mean(o * o, seg) + NORM_EPS) * ev('hg_norm_w')
    u_h = o * _silu(hz_ref[0].astype(F32))

    merged = (_sigmoid(g0_ref[0].astype(F32) + ev('gate_b0')) * _nn(_mx(u_m), wpm_ref[...])
              + _sigmoid(g1_ref[0].astype(F32) + ev('gate_b1')) * _nn(_mx(u_r), wpr_ref[...])
              + _sigmoid(g2_ref[0].astype(F32) + ev('gate_b2')) * _nn(_mx(u_h), wph_ref[...]))
    xs = xs_ref[0] + gx_ref[0] * _nn(_mx(merged), wout_ref[...])
    if final:
        xs = xs * lax.rsqrt(jnp.mean(xs * xs, axis=-1, keepdims=True) + NORM_EPS) * ev('final_w')
    o_ref[0] = xs


def _epilogue(hm, mo, mz, yr, bon, rz, oh, hz, gate, xs, gx, p, final_w):
    b, l, d = xs.shape
    tt = min(EP_TILE, l)
    seq = pl.BlockSpec((1, tt, d), lambda bi, i: (bi, i, 0))
    col = lambda x, extra=0: pl.BlockSpec((1, tt, d), lambda bi, i: (bi, i, x[1] // d + extra))
    ev = jnp.stack([p['ml_norm_w'], p['rk_ln_w'], p['rk_ln_b'], p['hg_norm_w'],
                    p['gate_b'][0], p['gate_b'][1], p['gate_b'][2],
                    final_w if final_w is not None else jnp.ones((d,), F32)], axis=0)
    wspec = pl.BlockSpec((d, d), lambda bi, i: (0, 0))
    ws = [_mx(p[n]) for n in ('w_pm', 'w_pr', 'w_ph', 'w_out')]
    return pl.pallas_call(
        functools.partial(_epilogue_kernel, final=final_w is not None),
        grid=(b, l // tt),
        in_specs=[seq, col(mo), col(mz), seq, seq, col(rz), seq, col(hz), col(gate), col(gate, 1), col(gate, 2), seq,
                  pl.BlockSpec((1, 1, d), lambda bi, i: (bi, 0, 0)),
                  pl.BlockSpec(ev.shape, lambda bi, i: (0, 0))] + [wspec] * 4,
        out_specs=seq,
        out_shape=jax.ShapeDtypeStruct((b, l, d), F32),
        compiler_params=_params(("parallel", "parallel")),
        name="epilogue",
    )(hm, mo[0], mz[0], yr, bon, rz[0], oh, hz[0], gate[0], gate[0], gate[0], xs, gx, ev, *ws)


def _rms_norm(x, w):
    return x * lax.rsqrt(jnp.mean(x * x, axis=-1, keepdims=True) + NORM_EPS) * w


def _bi_shift_seq(u):
    half = u.shape[-1] // 2
    g = jnp.pad(u, ((0, 0), (1, 1), (0, 0)))
    return jnp.concatenate((g[:, :-2, :half], g[:, 2:, half:]), axis=-1)


def _in_layout(d):
    rd = d // 16
    return (('m_q', d), ('m_k', d), ('m_v', d), ('m_o', d), ('m_z', d), ('m_if', 4 * ML_HEADS),
            ('r_r', d), ('r_k', d), ('r_v', d), ('r_z', d), ('r_wa', 4 * rd),
            ('h_q', d), ('h_f', 2 * d), ('h_i', d), ('h_z', d), ('gate', 3 * d))


_COL_GROUPS = ((False, ('m_q', 'm_k', 'r_r', 'r_k', 'r_v', 'h_q', 'h_f')),
               (False, ('r_wa', 'm_if')),
               (True, ('m_v', 'h_i', 'm_o', 'm_z', 'r_z', 'h_z', 'gate')))
_COL_ALIGN = 512


def _pack_w_in(w_in):
    d = w_in.shape[0]
    src, off = {}, 0
    for name, width in _in_layout(d):
        src[name] = (off, width)
        off += width

    def pack(names):
        cols, first, o = [], {}, 0
        for n in names:
            cols.append(w_in[:, src[n][0]:src[n][0] + src[n][1]])
            first[n] = o
            o += src[n][1]
        if o % _COL_ALIGN:
            cols.append(jnp.zeros((d, -o % _COL_ALIGN), w_in.dtype))
        return jnp.concatenate(cols, axis=1).astype(MXU_DT), first

    return [pack(names) for _, names in _COL_GROUPS]


def _take(col, width):
    return col[0][..., col[1]:col[1] + width]


def _zero_states(b, d):
    ml_hd = d // ML_HEADS
    ml = (jnp.zeros((b, ML_HEADS, ml_hd, ml_hd), F32), jnp.zeros((b, ML_HEADS, 1, ml_hd), F32),
          jnp.zeros((b, ML_HEADS, 8, 128), F32))
    rk = jnp.zeros((b, d // RK_GROUP, RK_GROUP, RK_GROUP), F32)
    hg_hd = d // HG_HEADS
    hg = jnp.zeros((b, HG_HEADS, hg_hd, hg_hd), F32)
    return ((ml, ml), (rk, rk), (hg, hg))


def _mlstm_branch(proj, p, init, b, l):
    gl = _take(proj('m_if'), 4 * ML_HEADS).reshape(b, l, 2, 2, ML_HEADS) + p['ml_if_b']
    acc, states = None, []
    for dr in (0, 1):
        log_i = gl[:, :, dr, 0]
        log_f = jax.nn.log_sigmoid(gl[:, :, dr, 1])
        acc, st = _mlstm_scan(proj('m_q'), proj('m_k'), proj('m_v'), p['ml_conv'], log_i, log_f,
                              init[dr], rev=bool(dr), acc=acc)
        states.append(st)
    return acc, tuple(states)


def _rwkv7_seq_branch(proj, h2, p, init, v_first, b, l, d):
    nh, n = d // RK_HD, RK_HD
    rd = d // 16
    mu = p['rk_mu']

    def lerp_shift(u, m):
        return u + m * (_bi_shift_seq(u) - u)

    r = lerp_shift(_take(proj('r_r'), d), mu[0:d])
    k = lerp_shift(_take(proj('r_k'), d), mu[d:2 * d])
    v = lerp_shift(_take(proj('r_v'), d), mu[2 * d:3 * d])
    pwa = _take(proj('r_wa'), 4 * rd)
    xwa = jnp.concatenate([lerp_shift(pwa[..., i * rd:(i + 1) * rd], mu[3 * d + i * rd:3 * d + (i + 1) * rd])
                           for i in range(4)], axis=-1)
    if p['rk_v0'] is None:
        v_first = v
    else:
        lora = _mm(_mm(h2, p['rk_v1']), p['rk_v2']).reshape(b, l, d)
        v = v + (v_first - v) * jax.nn.sigmoid(p['rk_v0'] + lora)
    kk = (k * p['rk_kk']).reshape(b, l, nh, n)
    kk = kk / jnp.maximum(jnp.sqrt(jnp.sum(kk * kk, axis=-1, keepdims=True)), 1e-12)
    kk = kk.reshape(b, l, d)
    acc, states, kd_sum = None, [], 0.0
    for dr in (0, 1):
        xw = xwa[..., dr * rd:(dr + 1) * rd]
        xa = xwa[..., (2 + dr) * rd:(3 + dr) * rd]
        log_w = -jax.nn.softplus(-(p['rk_w0'][dr] + _mm(jnp.tanh(xw).reshape(b * l, rd), p['rk_w2'][dr]).reshape(b, l, d))) - 0.5
        lw = -jnp.exp(log_w)
        a = jax.nn.sigmoid(p['rk_a0'][dr] + _mm(xa.reshape(b * l, rd), p['rk_a2'][dr]).reshape(b, l, d))
        kd = k * (1.0 + (a - 1.0) * p['rk_ka'])
        acc, s = _rwkv_scan_seq(r, lw, kd, v, kk, a, init[dr], rev=bool(dr), acc=acc)
        states.append(s)
        kd_sum = kd_sum + kd
    bonus = jnp.sum((r * kd_sum * p['rk_rk']).reshape(b, l, nh, n), axis=-1, keepdims=True) * v.reshape(b, l, nh, n)
    return acc, bonus.reshape(b, l, d), tuple(states), v_first


def _rwkv7_grid_branch(proj, h2, p, init, v_first, b, l, d):
    mu = p['rk_mu']
    zeros = jnp.zeros((d,), F32)
    has_vf = p['rk_v0'] is not None
    lora = _mm(_mm(h2, p['rk_v1']), p['rk_v2']).reshape(b, l, d) if has_vf else None
    acc, states, vf_out = None, [], v_first
    for dr in (0, 1):
        rows = {'mu_r': mu[0:d], 'mu_k': mu[d:2 * d], 'mu_v': mu[2 * d:3 * d], 'w0': p['rk_w0'][dr],
                'a0': p['rk_a0'][dr], 'kk': p['rk_kk'], 'ka': p['rk_ka'], 'rk': p['rk_rk'],
                'v0': p['rk_v0'] if has_vf else zeros}
        pvec = jnp.stack([rows[n] for n in _PV_ROWS], axis=0)
        emit_vf = (not has_vf) and dr == 0
        out = _rwkv_scan_grid(proj('r_r'), proj('r_k'), proj('r_v'), proj('r_wa'), pvec, mu[3 * d:][None],
                              _mx(p['rk_w2'][dr]), _mx(p['rk_a2'][dr]), init[dr], rev=bool(dr),
                              vf=v_first if has_vf else None, lora=lora, acc=acc, emit_vf=emit_vf)
        acc = (out[0], out[1])
        states.append(out[2])
        if emit_vf:
            vf_out = out[3]
    return acc[0], acc[1], tuple(states), vf_out


def _hgrn2_branch(proj, p, init):
    acc, states = None, []
    for dr in (0, 1):
        acc, s = _hgrn_scan(proj('h_q'), proj('h_f'), proj('h_i'), p['hg_f_b'][dr][None], p['hg_lb'][dr][None],
                            init[dr], rev=bool(dr), acc=acc)
        states.append(s)
    return acc, tuple(states)


def _mixer(h, p, init, on_grid, v_first, need_out, xs, gx, final_w):
    b, l, d = h.shape
    h2 = _mx(h.reshape(b * l, d))
    cols = {}
    for (w, first), (store_mx, _) in zip(p['w_in'], _COL_GROUPS):
        out = _mm(h2, w, MXU_DT if store_mx else F32).reshape(b, l, w.shape[1])
        cols.update({name: (out, off) for name, off in first.items()})

    def proj(name):
        return cols[name]

    hm, st_m = _mlstm_branch(proj, p, init[0], b, l)
    if on_grid:
        yr, bon, st_r, v_first = _rwkv7_grid_branch(proj, h2, p, init[1], v_first, b, l, d)
    else:
        yr, bon, st_r, v_first = _rwkv7_seq_branch(proj, h2, p, init[1], v_first, b, l, d)
    oh, st_h = _hgrn2_branch(proj, p, init[2])
    states = (st_m, st_r, st_h)
    if not need_out:
        return None, states, v_first
    out = _epilogue(hm, proj('m_o'), proj('m_z'), yr, bon, proj('r_z'), oh, proj('h_z'), proj('gate'),
                    xs, gx, p, final_w)
    return out, states, v_first


def kernel(x, c, ctx, c_ctx, norm_w, ada_w, ada_b, w_in, gate_b, ml_conv, ml_if_b, ml_norm_w, rk_mu, rk_w0, rk_w2, rk_a0, rk_a2, rk_kk, rk_ka, rk_rk, rk_v0, rk_v1, rk_v2, rk_ln_w, rk_ln_b, hg_f_b, hg_lb, hg_norm_w, w_pm, w_pr, w_ph, w_out, final_norm_w):
    batch, _, d = x.shape
    depth = w_in.shape[0]
    lb_p = jax.nn.softmax(hg_lb.astype(F32), axis=1)
    lower_bounds = jnp.cumsum(lb_p, axis=1) - lb_p[:, :1]
    xs, cs = x, ctx
    vf_x, vf_c = None, None
    cond = jnp.concatenate([jax.nn.silu(c), jax.nn.silu(c_ctx)[None]], axis=0)
    for l in range(depth):
        last = l == depth - 1
        p = {'w_in': _pack_w_in(w_in[l]), 'gate_b': gate_b[l], 'ml_conv': ml_conv[l], 'ml_if_b': ml_if_b[l],
             'ml_norm_w': ml_norm_w[l], 'rk_mu': rk_mu[l], 'rk_w0': rk_w0[l], 'rk_w2': rk_w2[l],
             'rk_a0': rk_a0[l], 'rk_a2': rk_a2[l], 'rk_kk': rk_kk[l], 'rk_ka': rk_ka[l], 'rk_rk': rk_rk[l],
             'rk_v0': rk_v0[l - 1] if l > 0 else None, 'rk_v1': rk_v1[l - 1] if l > 0 else None,
             'rk_v2': rk_v2[l - 1] if l > 0 else None, 'rk_ln_w': rk_ln_w[l], 'rk_ln_b': rk_ln_b[l],
             'hg_f_b': hg_f_b[l], 'hg_lb': lower_bounds[:, l], 'hg_norm_w': hg_norm_w[l],
             'w_pm': w_pm[l], 'w_pr': w_pr[l], 'w_ph': w_ph[l], 'w_out': w_out[l]}
        mod = _mm(cond, ada_w[l]) + ada_b[l]
        shift_x, scale_x, gate_x = jnp.split(mod[:batch, None, :], 3, axis=-1)
        shift_c, scale_c, gate_c = jnp.split(mod[batch], 3, axis=-1)
        hc = _rms_norm(cs, norm_w[l]) * (1.0 + scale_c) + shift_c
        gc = jnp.broadcast_to(gate_c[None, None, :], (batch, 1, d))
        cs_new, st_c, vf_c = _mixer(hc, p, _zero_states(batch, d), False, vf_c, not last, cs, gc, None)
        hx = _rms_norm(xs, norm_w[l]) * (1.0 + scale_x) + shift_x
        xs, _, vf_x = _mixer(hx, p, st_c, True, vf_x, True, xs, gate_x, final_norm_w if last else None)
        if not last:
            cs = cs_new
    return xs
```

```python
import functools

import jax
import jax.numpy as jnp
from jax import lax
from jax.experimental import pallas as pl
from jax.experimental.pallas import tpu as pltpu

F32 = jnp.float32
MXU_DT = jnp.bfloat16

NORM_EPS = 1e-6
GRID_W = 64
ML_HEADS = 4
ML_CHUNK = 256
RK_HD = 64
RK_CHUNK = 64
RK_GROUP = 256
RK_BATCH = 4
RK_GN_EPS = 64e-5
HG_HEADS = 8
HG_SUB = 16
HG_TILE = 128
EP_TILE = 256
SEG_W = 256
VMEM_LIMIT = 56 * 1024 * 1024


def _nt(a, b):
    return lax.dot_general(a, b, (((1,), (1,)), ((), ())), preferred_element_type=F32)


def _tn(a, b):
    return lax.dot_general(a, b, (((0,), (0,)), ((), ())), preferred_element_type=F32)


def _nn(a, b):
    return jnp.dot(a, b, preferred_element_type=F32)


def _mx(a):
    return a.astype(MXU_DT)


def _cumsum_mm(tri, x):
    hi = x.astype(MXU_DT)
    r1 = x - hi.astype(F32)
    mid = r1.astype(MXU_DT)
    lo = (r1 - mid.astype(F32)).astype(MXU_DT)
    one = lambda t: _nn(t, hi) + _nn(t, mid) + _nn(t, lo)
    return [one(t) for t in tri] if isinstance(tri, (list, tuple)) else one(tri)


def _seg_ones(seg):
    sh = seg.bit_length() - 1
    ri = lax.broadcasted_iota(jnp.int32, (SEG_W, SEG_W), 0)
    ci = lax.broadcasted_iota(jnp.int32, (SEG_W, SEG_W), 1)
    return ((ri >> sh) == (ci >> sh)).astype(F32).astype(MXU_DT)


def _segsum(x, ones):
    outs = [_nn(_mx(x[:, g * SEG_W:(g + 1) * SEG_W]), ones) for g in range(x.shape[1] // SEG_W)]
    return outs[0] if len(outs) == 1 else jnp.concatenate(outs, axis=1)


def _sigmoid(x):
    return jax.nn.sigmoid(x)


def _silu(x):
    return x * jax.nn.sigmoid(x)


def _params(sem):
    return pltpu.CompilerParams(dimension_semantics=sem, vmem_limit_bytes=VMEM_LIMIT)


def _mm_kernel(x_ref, w_ref, o_ref):
    o_ref[...] = _nn(_mx(x_ref[...]), w_ref[...]).astype(o_ref.dtype)


def _mm(x, w, out_dtype=F32):
    m, k = x.shape
    n = w.shape[1]
    mp = -(-m // 8) * 8
    npad = -(-n // 128) * 128
    if mp != m:
        x = jnp.pad(x, ((0, mp - m), (0, 0)))
    wb = w.astype(MXU_DT)
    if npad != n:
        wb = jnp.pad(wb, ((0, 0), (0, npad - n)))
    tm = next(t for t in (2048, 1024, 512, 256, 128, 64, 32, 16, 8) if mp % t == 0)
    tn = next(t for t in (1024, 512, 256, 128) if npad % t == 0)
    out = pl.pallas_call(
        _mm_kernel,
        grid=(mp // tm, npad // tn),
        in_specs=[pl.BlockSpec((tm, k), lambda i, j: (i, 0)),
                  pl.BlockSpec((k, tn), lambda i, j: (0, j))],
        out_specs=pl.BlockSpec((tm, tn), lambda i, j: (i, j)),
        out_shape=jax.ShapeDtypeStruct((mp, npad), out_dtype),
        compiler_params=_params(("parallel", "parallel")),
        name="mm",
    )(x, wb)
    if mp != m or npad != n:
        out = out[:m, :n]
    return out


def _mlstm_kernel(*refs, rev, heads, hd, nc, has_acc, conv):
    it = iter(refs)
    if conv:
        q_ref, qp_ref, qn_ref, k_ref, kp_ref, kn_ref, v_ref, cw_ref = (next(it) for _ in range(8))
    else:
        q_ref, k_ref, v_ref = (next(it) for _ in range(3))
    gcol_ref, grow_ref, c0_ref, n0_ref, m0_ref = (next(it) for _ in range(5))
    acc_ref = next(it) if has_acc else None
    h_ref, c_ref, n_ref, m_ref = (next(it) for _ in range(4))
    step = pl.program_id(1)

    @pl.when(step == 0)
    def _():
        c_ref[...] = c0_ref[...]
        n_ref[...] = n0_ref[...]
        m_ref[...] = m0_ref[...]

    t = q_ref.shape[1]
    ci = nc - 1 - step if rev else step
    has_prev = jnp.where(ci > 0, 1.0, 0.0)
    has_next = jnp.where(ci < nc - 1, 1.0, 0.0)
    trow = lax.broadcasted_iota(jnp.int32, (t, 1), 0)

    def conv_silu(u, prev_blk, next_blk, w3):
        up = jnp.where(trow == 0, prev_blk[7:8, :] * has_prev, pltpu.roll(u, 1, 0))
        dn = jnp.where(trow == t - 1, next_blk[0:1, :] * has_next, pltpu.roll(u, t - 1, 0))
        return _silu(up * w3[0:1, :] + u * w3[1:2, :] + dn * w3[2:3, :])

    if conv:
        q_all = conv_silu(q_ref[0], qp_ref[0], qn_ref[0], cw_ref[0:3, :]) * (hd ** -0.5)
        k_all = conv_silu(k_ref[0], kp_ref[0], kn_ref[0], cw_ref[3:6, :])
        qo_ref, ko_ref = next(it), next(it)
        qo_ref[0] = q_all
        ko_ref[0] = k_all
    else:
        q_all, k_all = q_ref[0], k_ref[0]

    row = lax.broadcasted_iota(jnp.int32, (t, t), 0)
    col = lax.broadcasted_iota(jnp.int32, (t, t), 1)
    mask = (col >= row) if rev else (col <= row)
    last = 0 if rev else t - 1
    hs = range(heads)
    lanes = [slice(h * hd, (h + 1) * hd) for h in hs]
    q32 = [q_all[:, sl] for sl in lanes]
    qc = [_mx(x) for x in q32]
    kc = [k_all[:, sl] for sl in lanes]
    vc = [_mx(v_ref[0, :, sl]) for sl in lanes]
    bc_col = [gcol_ref[0, :, h:h + 1] for h in hs]
    i_col = [gcol_ref[0, :, heads + h:heads + h + 1] for h in hs]
    m_prev = [m_ref[0, h, 0:1, 0:1] for h in hs]
    c_mem = [c_ref[0, h] for h in hs]
    n_mem = [n_ref[0, h] for h in hs]
    dmat = [jnp.where(mask, bc_col[h] - grow_ref[0, 0, h:h + 1, :] + grow_ref[0, 0, heads + h:heads + h + 1, :],
                      -jnp.inf) for h in hs]
    inter = [bc_col[h] + m_prev[h] for h in hs]
    m_t = [jnp.maximum(jnp.max(dmat[h], axis=-1, keepdims=True), inter[h]) for h in hs]
    qk = [_nt(qc[h], _mx(kc[h])) for h in hs]
    q_c = [_nn(qc[h], _mx(c_mem[h])) for h in hs]
    s = [qk[h] * jnp.exp(dmat[h] - m_t[h]) for h in hs]
    w_inter = [jnp.exp(inter[h] - m_t[h]) for h in hs]
    num = [_nn(_mx(s[h]), vc[h]) + w_inter[h] * q_c[h] for h in hs]
    for h in hs:
        qn = jnp.sum(q32[h] * n_mem[h], axis=-1, keepdims=True)
        den = jnp.sum(s[h], axis=-1, keepdims=True) + w_inter[h] * qn
        h_out = num[h] / jnp.maximum(jnp.abs(den), jnp.exp(-m_t[h]))
        if has_acc:
            h_out = h_out + acc_ref[0, :, lanes[h]]
        h_ref[0, :, lanes[h]] = h_out
    for h in hs:
        total = bc_col[h][last:last + 1, :]
        g_col = total - bc_col[h] + i_col[h]
        m_new = jnp.maximum(total + m_prev[h], jnp.max(g_col, axis=0, keepdims=True))
        wk = jnp.exp(g_col - m_new)
        dec = jnp.exp(total + m_prev[h] - m_new)
        kw = kc[h] * wk
        c_ref[0, h] = dec * c_mem[h] + _tn(_mx(kw), vc[h])
        n_ref[0, h] = dec * n_mem[h] + jnp.sum(kw, axis=0, keepdims=True)
        m_ref[0, h] = jnp.broadcast_to(m_new, m_ref.shape[2:])


def _mlstm_scan(pq, pk, pv, conv_w, log_i, log_f, state, rev, acc=None, qk=None):
    b, l, _ = pq[0].shape
    w = conv_w.shape[-1]
    heads = ML_HEADS
    hd = w // heads
    t = min(ML_CHUNK, l)
    assert l % t == 0
    nc = l // t
    nb8 = l // 8
    lf = log_f.reshape(b, nc, t, heads)
    if rev:
        bcum = jnp.flip(jnp.cumsum(jnp.flip(lf, 2), axis=2), 2)
    else:
        bcum = jnp.cumsum(lf, axis=2)
    gcol = jnp.concatenate([bcum.reshape(b, l, heads), log_i], axis=-1)
    grow = gcol.reshape(b, nc, t, 2 * heads).transpose(0, 1, 3, 2)
    ch = (lambda c: nc - 1 - c) if rev else (lambda c: c)
    cidx = lambda bi, c: (bi, ch(c), 0)
    ridx = lambda bi, c: (bi, ch(c), 0, 0)
    sidx = lambda bi, c: (bi, 0, 0, 0)
    c0, n0, m0 = state
    seq = pl.BlockSpec((1, t, w), cidx)
    col = lambda x: pl.BlockSpec((1, t, w), lambda bi, c: (bi, ch(c), x[1] // w))
    prv = lambda x: pl.BlockSpec((1, 8, w), lambda bi, c: (bi, jnp.maximum(ch(c) * (t // 8) - 1, 0), x[1] // w))
    nxt = lambda x: pl.BlockSpec((1, 8, w), lambda bi, c: (bi, jnp.minimum((ch(c) + 1) * (t // 8), nb8 - 1), x[1] // w))
    st_specs = [pl.BlockSpec((1, heads, hd, hd), sidx), pl.BlockSpec((1, heads, 1, hd), sidx),
                pl.BlockSpec((1, heads, 8, 128), sidx)]
    conv = qk is None
    if conv:
        in_specs = [col(pq), prv(pq), nxt(pq), col(pk), prv(pk), nxt(pk), col(pv),
                    pl.BlockSpec((6, w), lambda bi, c: (0, 0))]
        args = [pq[0], pq[0], pq[0], pk[0], pk[0], pk[0], pv[0], conv_w.reshape(6, w)]
    else:
        in_specs = [seq, seq, col(pv)]
        args = [qk[0], qk[1], pv[0]]
    in_specs += [pl.BlockSpec((1, t, 2 * heads), cidx), pl.BlockSpec((1, 1, 2 * heads, t), ridx)] + st_specs
    args += [gcol, grow, c0, n0, m0]
    if acc is not None:
        in_specs.append(seq)
        args.append(acc)
    seq_out = jax.ShapeDtypeStruct((b, l, w), F32)
    out = pl.pallas_call(
        functools.partial(_mlstm_kernel, rev=rev, heads=heads, hd=hd, nc=nc, has_acc=acc is not None, conv=conv),
        grid=(b, nc),
        in_specs=in_specs,
        out_specs=[seq] + st_specs + ([seq, seq] if conv else []),
        out_shape=[seq_out, jax.ShapeDtypeStruct(c0.shape, F32), jax.ShapeDtypeStruct(n0.shape, F32),
                   jax.ShapeDtypeStruct(m0.shape, F32)] + ([seq_out, seq_out] if conv else []),
        compiler_params=_params(("parallel", "arbitrary")),
        name="mlstm_rev" if rev else "mlstm_fwd",
    )(*args)
    return out[0], (out[1], out[2], out[3]), ((out[4], out[5]) if conv else qk)


def _rwkv_core(rows, s_ref, rev):
    c, w = rows[0][0].shape
    g_w = RK_GROUP
    nh = g_w // RK_HD
    assert c == RK_HD and c & (c - 1) == 0
    sh = c.bit_length() - 1
    rj = lax.broadcasted_iota(jnp.int32, (g_w, g_w), 0)
    cj = lax.broadcasted_iota(jnp.int32, (g_w, g_w), 1)
    blk = (rj >> sh) == (cj >> sh)
    tt, ss = rj & (c - 1), cj & (c - 1)
    incl = blk & ((ss >= tt) if rev else (ss <= tt))
    strict = blk & ((ss > tt) if rev else (ss < tt))
    eye = (rj == cj).astype(F32)
    tri = incl[:c, :c].astype(F32).astype(MXU_DT)
    lo_half = lax.broadcasted_iota(jnp.int32, (2 * g_w, 2 * c), 1) < c
    last = 0 if rev else c - 1
    zero = jnp.zeros((), F32)
    chains = [(bi, g) for bi in range(len(rows)) for g in range(w // g_w)]

    def bd(x):
        return _mx(jnp.where(blk, jnp.concatenate([x] * nh, axis=0), zero))

    def rsum(z):
        out = z[0:c]
        for i in range(1, nh):
            out = out + z[i * c:(i + 1) * c]
        return out

    a_bd, r_bd, v_bd, bk_t, r_t, v_g, bk_hat, dec = {}, {}, {}, {}, {}, {}, {}, {}
    for bi, (r, lw, kd, v, kk, a) in enumerate(rows):
        ka = kk * a
        gi = _cumsum_mm(tri, lw)
        tot = gi[last:last + 1, :]
        e_neg = jnp.exp(-gi)
        e_tail = jnp.exp(tot - gi)
        full = {'a': -kk * jnp.exp(gi - lw), 'b': ka * e_neg, 'k': kd * e_neg, 'r': r * jnp.exp(gi),
                'bh': ka * e_tail, 'kh': kd * e_tail, 'dec': jnp.exp(tot), 'v': v}
        for g in range(w // g_w):
            x = {n: t[:, g * g_w:(g + 1) * g_w] for n, t in full.items()}
            ch = (bi, g)
            a_bd[ch], r_bd[ch], v_bd[ch] = bd(x['a']), bd(x['r']), bd(x['v'])
            bk_t[ch] = _mx(jnp.concatenate([x['b'], x['k']], axis=0))
            bk_hat[ch] = _mx(jnp.concatenate([x['bh'], x['kh']], axis=0))
            r_t[ch], v_g[ch], dec[ch] = _mx(x['r']), x['v'], x['dec']

    l_ab, a_ak, a_rb, a_rk = {}, {}, {}, {}
    for ch in chains:
        prod = _nt(jnp.concatenate([a_bd[ch], r_bd[ch]], axis=0), bk_t[ch])
        swp = pltpu.roll(prod, c, 1)
        xb = jnp.where(lo_half, prod, swp)
        xk = jnp.where(lo_half, swp, prod)
        xb = jnp.concatenate([xb, xb], axis=1)
        xk = jnp.concatenate([xk, xk], axis=1)
        l_ab[ch] = jnp.where(strict, xb[:g_w], zero)
        a_ak[ch] = _mx(jnp.where(strict, xk[:g_w], zero))
        a_rb[ch] = _mx(rsum(jnp.where(incl, xb[g_w:], zero)))
        a_rk[ch] = _mx(rsum(jnp.where(incl, xk[g_w:], zero)))

    xs = {ch: _mx(l_ab[ch]) for ch in chains}
    ps = {ch: eye + l_ab[ch] for ch in chains}
    for _ in range(max(1, (c - 1).bit_length() - 1)):
        xs = {ch: _mx(_nn(xs[ch], xs[ch])) for ch in chains}
        ps = {ch: ps[ch] + _nn(_mx(ps[ch]), xs[ch]) for ch in chains}

    t_cat = {ch: _mx(rsum(ps[ch])) for ch in chains}
    w_cat = {ch: _mx(_nn(t_cat[ch], a_bd[ch])) for ch in chains}
    ta = {ch: _mx(_nn(t_cat[ch], a_ak[ch])) for ch in chains}
    u0 = {ch: _nn(ta[ch], v_bd[ch]) for ch in chains}
    y_k = {ch: _nn(a_rk[ch], v_bd[ch]) for ch in chains}
    s_mem = {ch: s_ref[ch[0], ch[1]] for ch in chains}
    s_b = {ch: _mx(s_mem[ch]) for ch in chains}
    u = {ch: _nt(w_cat[ch], s_b[ch]) + u0[ch] for ch in chains}
    y = {ch: _nt(r_t[ch], s_b[ch]) + _nn(a_rb[ch], bd(u[ch])) + y_k[ch] for ch in chains}
    for ch in chains:
        upd = _tn(_mx(jnp.concatenate([u[ch], v_g[ch]], axis=0)), bk_hat[ch])
        s_ref[ch[0], ch[1]] = s_mem[ch] * dec[ch] + jnp.where(blk, upd, zero)
    return [jnp.concatenate([y[(bi, g)] for g in range(w // g_w)], axis=1) for bi in range(len(rows))]


def _rwkv_seq_kernel(*refs, rev, has_acc):
    r_ref, lw_ref, kd_ref, v_ref, kk_ref, a_ref, s0_ref = refs[:7]
    acc_ref = refs[7] if has_acc else None
    y_ref, s_ref = refs[7 + has_acc:]

    @pl.when(pl.program_id(1) == 0)
    def _():
        s_ref[...] = s0_ref[...]

    y, = _rwkv_core([(r_ref[0], lw_ref[0], kd_ref[0], v_ref[0], kk_ref[0], a_ref[0])], s_ref, rev)
    if has_acc:
        y = y + acc_ref[0]
    y_ref[0] = y


def _rwkv_scan_seq(r, lw, kd, v, kk, a, state, rev, acc=None):
    b, l, w = r.shape
    c = RK_CHUNK
    nc = l // c
    cidx = (lambda bi, ch: (bi, nc - 1 - ch, 0)) if rev else (lambda bi, ch: (bi, ch, 0))
    seq = pl.BlockSpec((1, c, w), cidx)
    st = pl.BlockSpec((1,) + state.shape[1:], lambda bi, ch: (bi, 0, 0, 0))
    args = [r, lw, kd, v, kk, a, state] + ([acc] if acc is not None else [])
    y, s = pl.pallas_call(
        functools.partial(_rwkv_seq_kernel, rev=rev, has_acc=acc is not None),
        grid=(b, nc),
        in_specs=[seq] * 6 + [st] + ([seq] if acc is not None else []),
        out_specs=[seq, st],
        out_shape=[jax.ShapeDtypeStruct((b, l, w), F32), jax.ShapeDtypeStruct(state.shape, F32)],
        compiler_params=_params(("parallel", "arbitrary")),
        name="rwkv_seq_rev" if rev else "rwkv_seq_fwd",
    )(*args)
    return y, s


_PV_ROWS = ('mu_r', 'mu_k', 'mu_v', 'w0', 'a0', 'kk', 'ka', 'rk', 'v0')


def _rwkv_grid_kernel(*refs, rev, nc, has_vf, has_acc, emit_vf):
    it = iter(refs)
    pr, pr_u, pr_d, pk, pk_u, pk_d, pv, pv_u, pv_d, wa, wa_u, wa_d = (next(it) for _ in range(12))
    vf_ref, lora_ref = (next(it), next(it)) if has_vf else (None, None)
    pvec, mu_wa, w2_ref, a2_ref, s0_ref = (next(it) for _ in range(5))
    accy_ref, accb_ref = (next(it), next(it)) if has_acc else (None, None)
    y_ref, bon_ref, s_ref = (next(it) for _ in range(3))
    vfo_ref = next(it) if emit_vf else None
    step = pl.program_id(1)

    @pl.when(step == 0)
    def _():
        s_ref[...] = s0_ref[...]

    c = pr.shape[1]
    w = pr.shape[2]
    rd = w // 16
    ci = nc - 1 - step if rev else step
    has_up = jnp.where(ci > 0, 1.0, 0.0)
    has_dn = jnp.where(ci < nc - 1, 1.0, 0.0)
    trow = lax.broadcasted_iota(jnp.int32, (c, 1), 0)
    prow = lambda name: pvec[_PV_ROWS.index(name):_PV_ROWS.index(name) + 1, :]

    def left(x):
        return jnp.where(trow == 0, 0.0, pltpu.roll(x, 1, 0))

    def right(x):
        return jnp.where(trow == c - 1, 0.0, pltpu.roll(x, c - 1, 0))

    ones = _seg_ones(RK_HD)
    d = 1 if rev else 0
    rows, bons = [], []
    for bi in range(pr.shape[0]):
        def lerp_wide(cur_ref, up_ref, dn_ref, mu):
            cur = cur_ref[bi]
            q = w // 4
            sh = jnp.concatenate([left(cur[:, :q]), right(cur[:, q:2 * q]),
                                  up_ref[bi] * has_up, dn_ref[bi] * has_dn], axis=1)
            return cur + mu * (sh - cur)

        r = lerp_wide(pr, pr_u, pr_d, prow('mu_r'))
        k = lerp_wide(pk, pk_u, pk_d, prow('mu_k'))
        v = lerp_wide(pv, pv_u, pv_d, prow('mu_v'))

        cur = wa[bi]
        qd = (lax.broadcasted_iota(jnp.int32, cur.shape, 1) & (rd - 1)) >> ((rd // 4).bit_length() - 1)
        sh = jnp.where(qd == 0, left(cur), jnp.where(qd == 1, right(cur),
                                                     jnp.where(qd == 2, wa_u[bi] * has_up, wa_d[bi] * has_dn)))
        xwa = cur + mu_wa[...] * (sh - cur)
        xw = xwa[:, d * rd:(d + 1) * rd]
        xa = xwa[:, (2 + d) * rd:(3 + d) * rd]
        z = -(prow('w0') + _nn(_mx(jnp.tanh(xw)), w2_ref[...]))
        softplus = jnp.maximum(z, 0.0) + jnp.log(1.0 + jnp.exp(-jnp.abs(z)))
        lw = -jnp.exp(-softplus - 0.5)
        a = _sigmoid(prow('a0') + _nn(_mx(xa), a2_ref[...]))
        if has_vf:
            v = v + (vf_ref[bi] - v) * _sigmoid(prow('v0') + lora_ref[bi])
        if emit_vf:
            vfo_ref[bi] = v
        kkn = k * prow('kk')
        kk = kkn / jnp.maximum(jnp.sqrt(_segsum(kkn * kkn, ones)), 1e-12)
        kd = k * (1.0 + (a - 1.0) * prow('ka'))
        bons.append(_segsum(r * kd * prow('rk'), ones) * v)
        rows.append((r, lw, kd, v, kk, a))
    ys = _rwkv_core(rows, s_ref, rev)
    for bi, (y, bon) in enumerate(zip(ys, bons)):
        if has_acc:
            y = y + accy_ref[bi]
            bon = bon + accb_ref[bi]
        y_ref[bi] = y
        bon_ref[bi] = bon


def _rwkv_scan_grid(pr, pk, pv, wa, pvec, mu_wa, w2, a2, state, rev, vf=None, lora=None,
                    acc=None, emit_vf=False):
    b, l, _ = pr[0].shape
    w = pvec.shape[1]
    c = RK_CHUNK
    assert c == GRID_W
    nc = l // c
    wq = w // 4
    ch = (lambda s: nc - 1 - s) if rev else (lambda s: s)
    cur = lambda lane: (lambda bi, s: (bi, ch(s), lane))
    upi = lambda lane: (lambda bi, s: (bi, jnp.maximum(ch(s) - 1, 0), lane))
    dni = lambda lane: (lambda bi, s: (bi, jnp.minimum(ch(s) + 1, nc - 1), lane))
    bb = RK_BATCH if b % RK_BATCH == 0 else 1
    seq = pl.BlockSpec((bb, c, w), cur(0))
    wide = lambda x: [pl.BlockSpec((bb, c, w), cur(x[1] // w)), pl.BlockSpec((bb, c, wq), upi(x[1] // wq + 2)),
                      pl.BlockSpec((bb, c, wq), dni(x[1] // wq + 3))]
    wa_w = mu_wa.shape[1]
    small = [pl.BlockSpec((bb, c, wa_w), cur(wa[1] // wa_w)), pl.BlockSpec((bb, c, wa_w), upi(wa[1] // wa_w)),
             pl.BlockSpec((bb, c, wa_w), dni(wa[1] // wa_w))]
    full = lambda x: pl.BlockSpec(x.shape, lambda bi, s: (0,) * x.ndim)
    st = pl.BlockSpec((bb,) + state.shape[1:], lambda bi, s: (bi, 0, 0, 0))
    in_specs = wide(pr) + wide(pk) + wide(pv) + small
    args = [pr[0]] * 3 + [pk[0]] * 3 + [pv[0]] * 3 + [wa[0]] * 3
    if vf is not None:
        in_specs += [seq, seq]
        args += [vf, lora]
    in_specs += [full(pvec), full(mu_wa), full(w2), full(a2), st]
    args += [pvec, mu_wa, w2, a2, state]
    if acc is not None:
        in_specs += [seq, seq]
        args += list(acc)
    out_specs = [seq, seq, st]
    out_shape = [jax.ShapeDtypeStruct((b, l, w), F32), jax.ShapeDtypeStruct((b, l, w), F32),
                 jax.ShapeDtypeStruct(state.shape, F32)]
    if emit_vf:
        out_specs.append(seq)
        out_shape.append(jax.ShapeDtypeStruct((b, l, w), F32))
    return pl.pallas_call(
        functools.partial(_rwkv_grid_kernel, rev=rev, nc=nc, has_vf=vf is not None,
                          has_acc=acc is not None, emit_vf=emit_vf),
        grid=(b // bb, nc),
        in_specs=in_specs,
        out_specs=out_specs,
        out_shape=out_shape,
        compiler_params=_params(("parallel", "arbitrary")),
        name="rwkv_grid_rev" if rev else "rwkv_grid_fwd",
    )(*args)


def _hgrn_kernel(*refs, rev, heads, hd, has_acc):
    q_ref, f_ref, v_ref, fb_ref, lb_ref, s0_ref = refs[:6]
    acc_ref = refs[6] if has_acc else None
    o_ref, s_ref = refs[6 + has_acc:]

    @pl.when(pl.program_id(1) == 0)
    def _():
        s_ref[...] = s0_ref[...]

    tt = q_ref.shape[1]
    sub = HG_SUB
    sh = sub.bit_length() - 1
    nsub = tt // sub
    mid = sub // 2 if rev else sub // 2 - 1
    last = 0 if rev else sub - 1
    ri = lax.broadcasted_iota(jnp.int32, (tt, tt), 0)
    ci = lax.broadcasted_iota(jnp.int32, (tt, tt), 1)
    causal = ((ri >> sh) == (ci >> sh)) & ((ci >= ri) if rev else (ci <= ri))
    lb = lb_ref[...]
    f = lb + (1.0 - lb) * _sigmoid(f_ref[0] + fb_ref[...])
    g_all = _cumsum_mm(causal.astype(F32).astype(MXU_DT), jnp.log(f))

    def row_of_each_sub(r):
        return jnp.concatenate([jnp.broadcast_to(g_all[j * sub + r:j * sub + r + 1], (sub, g_all.shape[1]))
                                for j in range(nsub)], axis=0)

    g_mid, g_last = row_of_each_sub(mid), row_of_each_sub(last)
    k_all = 1.0 - f
    q_all = _silu(q_ref[0])
    q1 = _mx(q_all * jnp.exp(g_all - g_mid))
    k1 = _mx(k_all * jnp.exp(g_mid - g_all))
    qg_f = q_all * jnp.exp(g_all)
    kl_f = k_all * jnp.exp(g_last - g_all)
    v_all = _mx(v_ref[0])
    assert nsub % 2 == 0
    is_first = lambda j: (j % 2 == 1) == rev
    dec = [jnp.exp(g_all[j * sub + last:j * sub + last + 1]) for j in range(nsub)]
    one_row = jnp.ones_like(dec[0])
    per_sub = lambda vals: jnp.concatenate([jnp.broadcast_to(x, (sub, x.shape[1])) for x in vals], axis=0)
    qg, kl = _mx(qg_f), _mx(kl_f)
    qg2 = _mx(qg_f * per_sub([one_row if is_first(j) else dec[j ^ 1] for j in range(nsub)]))
    kl2 = _mx(kl_f * per_sub([dec[j ^ 1] if is_first(j) else one_row for j in range(nsub)]))
    sub_r, sub_c = ri >> sh, ci >> sh
    second_sees_first = ((sub_r ^ 1) == sub_c) & ((sub_r & 1) == (0 if rev else 1))
    lanes = [slice(h * hd, (h + 1) * hd) for h in range(heads)]
    zero = jnp.zeros((), F32)
    att = [_mx(jnp.where(causal, _nt(q1[:, sl], k1[:, sl]), zero)
               + jnp.where(second_sees_first, _nt(qg[:, sl], kl[:, sl]), zero)) for sl in lanes]
    intra = [_nn(att[h], v_all[:, lanes[h]]) for h in range(heads)]
    pairs = [nsub // 2 - 1 - s if rev else s for s in range(nsub // 2)]
    rows = [slice(2 * p * sub, 2 * (p + 1) * sub) for p in range(nsub // 2)]
    upd = {(p, h): _tn(v_all[rows[p], lanes[h]], kl2[rows[p], lanes[h]]) for p in pairs for h in range(heads)}
    s_mem = [s_ref[0, h] for h in range(heads)]
    for p in pairs:
        dec_pair = dec[2 * p] * dec[2 * p + 1]
        outs = []
        for h in range(heads):
            outs.append(intra[h][rows[p]] + _nt(qg2[rows[p], lanes[h]], _mx(s_mem[h])))
            s_mem[h] = s_mem[h] * dec_pair[:, lanes[h]] + upd[(p, h)]
        o = jnp.concatenate(outs, axis=1)
        if has_acc:
            o = o + acc_ref[0, rows[p], :]
        o_ref[0, rows[p], :] = o
    for h in range(heads):
        s_ref[0, h] = s_mem[h]


def _hgrn_scan(pq, pf, pi, f_b, lb, state, rev, acc=None):
    b, l, _ = pq[0].shape
    w = f_b.shape[1]
    heads = HG_HEADS
    hd = w // heads
    tt = min(HG_TILE, l)
    nc = l // tt
    d = 1 if rev else 0
    ch = (lambda s: nc - 1 - s) if rev else (lambda s: s)
    col = lambda x, extra=0: pl.BlockSpec((1, tt, w), lambda bi, s: (bi, ch(s), x[1] // w + extra))
    seq = pl.BlockSpec((1, tt, w), lambda bi, s: (bi, ch(s), 0))
    vec = pl.BlockSpec((1, w), lambda bi, s: (0, 0))
    st = pl.BlockSpec((1,) + state.shape[1:], lambda bi, s: (bi, 0, 0, 0))
    args = [pq[0], pf[0], pi[0], f_b, lb, state] + ([acc] if acc is not None else [])
    o, s = pl.pallas_call(
        functools.partial(_hgrn_kernel, rev=rev, heads=heads, hd=hd, has_acc=acc is not None),
        grid=(b, nc),
        in_specs=[col(pq), col(pf, d), col(pi), vec, vec, st] + ([seq] if acc is not None else []),
        out_specs=[seq, st],
        out_shape=[jax.ShapeDtypeStruct((b, l, w), F32), jax.ShapeDtypeStruct(state.shape, F32)],
        compiler_params=_params(("parallel", "arbitrary")),
        name="hgrn_rev" if rev else "hgrn_fwd",
    )(*args)
    return o, s


_EV_ROWS = ('ml_norm_w', 'rk_ln_w', 'rk_ln_b', 'hg_norm_w', 'gate_b0', 'gate_b1', 'gate_b2', 'final_w')


def _epilogue_kernel(hm_ref, mo_ref, mz_ref, yr_ref, bon_ref, rz_ref, oh_ref, hz_ref,
                     g0_ref, g1_ref, g2_ref, xs_ref, gx_ref, ev_ref,
                     wpm_ref, wpr_ref, wph_ref, wout_ref, o_ref, *, final):
    ev = lambda name: ev_ref[_EV_ROWS.index(name):_EV_ROWS.index(name) + 1, :]
    d = hm_ref.shape[2]

    def seg_mean(x, seg):
        return _segsum(x, _seg_ones(seg)) * (1.0 / seg)

    seg = d // ML_HEADS
    y = hm_ref[0]
    y = y - seg_mean(y, seg)
    y = y * lax.rsqrt(seg_mean(y * y, seg) + NORM_EPS) * ev('ml_norm_w')
    u_m = _sigmoid(mo_ref[0].astype(F32)) * y * _silu(mz_ref[0].astype(F32))
    y = yr_ref[0]
    y = y - seg_mean(y, RK_HD)
    y = y * lax.rsqrt(seg_mean(y * y, RK_HD) + RK_GN_EPS) * ev('rk_ln_w') + ev('rk_ln_b') + bon_ref[0]
    u_r = y * _silu(rz_ref[0].astype(F32))
    seg = d // HG_HEADS
    o = oh_ref[0]
    o = o * lax.rsqrt(seg_mean(o * o, seg) + NORM_EPS) * ev('hg_norm_w')
    u_h = o * _silu(hz_ref[0].astype(F32))

    merged = (_sigmoid(g0_ref[0].astype(F32) + ev('gate_b0')) * _nn(_mx(u_m), wpm_ref[...])
              + _sigmoid(g1_ref[0].astype(F32) + ev('gate_b1')) * _nn(_mx(u_r), wpr_ref[...])
              + _sigmoid(g2_ref[0].astype(F32) + ev('gate_b2')) * _nn(_mx(u_h), wph_ref[...]))
    xs = xs_ref[0] + gx_ref[0] * _nn(_mx(merged), wout_ref[...])
    if final:
        xs = xs * lax.rsqrt(jnp.mean(xs * xs, axis=-1, keepdims=True) + NORM_EPS) * ev('final_w')
    o_ref[0] = xs


def _epilogue(hm, mo, mz, yr, bon, rz, oh, hz, gate, xs, gx, p, final_w):
    b, l, d = xs.shape
    tt = min(EP_TILE, l)
    seq = pl.BlockSpec((1, tt, d), lambda bi, i: (bi, i, 0))
    col = lambda x, extra=0: pl.BlockSpec((1, tt, d), lambda bi, i: (bi, i, x[1] // d + extra))
    ev = jnp.stack([p['ml_norm_w'], p['rk_ln_w'], p['rk_ln_b'], p['hg_norm_w'],
                    p['gate_b'][0], p['gate_b'][1], p['gate_b'][2],
                    final_w if final_w is not None else jnp.ones((d,), F32)], axis=0)
    wspec = pl.BlockSpec((d, d), lambda bi, i: (0, 0))
    ws = [_mx(p[n]) for n in ('w_pm', 'w_pr', 'w_ph', 'w_out')]
    return pl.pallas_call(
        functools.partial(_epilogue_kernel, final=final_w is not None),
        grid=(b, l // tt),
        in_specs=[seq, col(mo), col(mz), seq, seq, col(rz), seq, col(hz), col(gate), col(gate, 1), col(gate, 2), seq,
                  pl.BlockSpec((1, 1, d), lambda bi, i: (bi, 0, 0)),
                  pl.BlockSpec(ev.shape, lambda bi, i: (0, 0))] + [wspec] * 4,
        out_specs=seq,
        out_shape=jax.ShapeDtypeStruct((b, l, d), F32),
        compiler_params=_params(("parallel", "parallel")),
        name="epilogue",
    )(hm, mo[0], mz[0], yr, bon, rz[0], oh, hz[0], gate[0], gate[0], gate[0], xs, gx, ev, *ws)


def _rms_norm(x, w):
    return x * lax.rsqrt(jnp.mean(x * x, axis=-1, keepdims=True) + NORM_EPS) * w


def _bi_shift_seq(u):
    half = u.shape[-1] // 2
    g = jnp.pad(u, ((0, 0), (1, 1), (0, 0)))
    return jnp.concatenate((g[:, :-2, :half], g[:, 2:, half:]), axis=-1)


def _in_layout(d):
    rd = d // 16
    return (('m_q', d), ('m_k', d), ('m_v', d), ('m_o', d), ('m_z', d), ('m_if', 4 * ML_HEADS),
            ('r_r', d), ('r_k', d), ('r_v', d), ('r_z', d), ('r_wa', 4 * rd),
            ('h_q', d), ('h_f', 2 * d), ('h_i', d), ('h_z', d), ('gate', 3 * d))


_COL_GROUPS = ((False, ('m_q', 'm_k', 'r_r', 'r_k', 'r_v', 'h_q', 'h_f')),
               (False, ('r_wa', 'm_if')),
               (True, ('m_v', 'h_i', 'm_o', 'm_z', 'r_z', 'h_z', 'gate')))
_COL_ALIGN = 512


def _pack_w_in(w_in):
    d = w_in.shape[0]
    src, off = {}, 0
    for name, width in _in_layout(d):
        src[name] = (off, width)
        off += width

    def pack(names):
        cols, first, o = [], {}, 0
        for n in names:
            cols.append(w_in[:, src[n][0]:src[n][0] + src[n][1]])
            first[n] = o
            o += src[n][1]
        if o % _COL_ALIGN:
            cols.append(jnp.zeros((d, -o % _COL_ALIGN), w_in.dtype))
        return jnp.concatenate(cols, axis=1).astype(MXU_DT), first

    return [pack(names) for _, names in _COL_GROUPS]


def _take(col, width):
    return col[0][..., col[1]:col[1] + width]


def _zero_states(b, d):
    ml_hd = d // ML_HEADS
    ml = (jnp.zeros((b, ML_HEADS, ml_hd, ml_hd), F32), jnp.zeros((b, ML_HEADS, 1, ml_hd), F32),
          jnp.zeros((b, ML_HEADS, 8, 128), F32))
    rk = jnp.zeros((b, d // RK_GROUP, RK_GROUP, RK_GROUP), F32)
    hg_hd = d // HG_HEADS
    hg = jnp.zeros((b, HG_HEADS, hg_hd, hg_hd), F32)
    return ((ml, ml), (rk, rk), (hg, hg))


def _mlstm_branch(proj, p, init, b, l):
    gl = _take(proj('m_if'), 4 * ML_HEADS).reshape(b, l, 2, 2, ML_HEADS) + p['ml_if_b']
    acc, qk, states = None, None, []
    for dr in (0, 1):
        log_i = gl[:, :, dr, 0]
        log_f = jax.nn.log_sigmoid(gl[:, :, dr, 1])
        acc, st, qk = _mlstm_scan(proj('m_q'), proj('m_k'), proj('m_v'), p['ml_conv'], log_i, log_f,
                                  init[dr], rev=bool(dr), acc=acc, qk=qk)
        states.append(st)
    return acc, tuple(states)


def _rwkv7_seq_branch(proj, h2, p, init, v_first, b, l, d):
    nh, n = d // RK_HD, RK_HD
    rd = d // 16
    mu = p['rk_mu']

    def lerp_shift(u, m):
        return u + m * (_bi_shift_seq(u) - u)

    r = lerp_shift(_take(proj('r_r'), d), mu[0:d])
    k = lerp_shift(_take(proj('r_k'), d), mu[d:2 * d])
    v = lerp_shift(_take(proj('r_v'), d), mu[2 * d:3 * d])
    pwa = _take(proj('r_wa'), 4 * rd)
    xwa = jnp.concatenate([lerp_shift(pwa[..., i * rd:(i + 1) * rd], mu[3 * d + i * rd:3 * d + (i + 1) * rd])
                           for i in range(4)], axis=-1)
    if p['rk_v0'] is None:
        v_first = v
    else:
        lora = _mm(_mm(h2, p['rk_v1']), p['rk_v2']).reshape(b, l, d)
        v = v + (v_first - v) * jax.nn.sigmoid(p['rk_v0'] + lora)
    kk = (k * p['rk_kk']).reshape(b, l, nh, n)
    kk = kk / jnp.maximum(jnp.sqrt(jnp.sum(kk * kk, axis=-1, keepdims=True)), 1e-12)
    kk = kk.reshape(b, l, d)
    acc, states, kd_sum = None, [], 0.0
    for dr in (0, 1):
        xw = xwa[..., dr * rd:(dr + 1) * rd]
        xa = xwa[..., (2 + dr) * rd:(3 + dr) * rd]
        log_w = -jax.nn.softplus(-(p['rk_w0'][dr] + _mm(jnp.tanh(xw).reshape(b * l, rd), p['rk_w2'][dr]).reshape(b, l, d))) - 0.5
        lw = -jnp.exp(log_w)
        a = jax.nn.sigmoid(p['rk_a0'][dr] + _mm(xa.reshape(b * l, rd), p['rk_a2'][dr]).reshape(b, l, d))
        kd = k * (1.0 + (a - 1.0) * p['rk_ka'])
        acc, s = _rwkv_scan_seq(r, lw, kd, v, kk, a, init[dr], rev=bool(dr), acc=acc)
        states.append(s)
        kd_sum = kd_sum + kd
    bonus = jnp.sum((r * kd_sum * p['rk_rk']).reshape(b, l, nh, n), axis=-1, keepdims=True) * v.reshape(b, l, nh, n)
    return acc, bonus.reshape(b, l, d), tuple(states), v_first


def _rwkv7_grid_branch(proj, h2, p, init, v_first, b, l, d):
    mu = p['rk_mu']
    zeros = jnp.zeros((d,), F32)
    has_vf = p['rk_v0'] is not None
    lora = _mm(_mm(h2, p['rk_v1']), p['rk_v2']).reshape(b, l, d) if has_vf else None
    acc, states, vf_out = None, [], v_first
    for dr in (0, 1):
        rows = {'mu_r': mu[0:d], 'mu_k': mu[d:2 * d], 'mu_v': mu[2 * d:3 * d], 'w0': p['rk_w0'][dr],
                'a0': p['rk_a0'][dr], 'kk': p['rk_kk'], 'ka': p['rk_ka'], 'rk': p['rk_rk'],
                'v0': p['rk_v0'] if has_vf else zeros}
        pvec = jnp.stack([rows[n] for n in _PV_ROWS], axis=0)
        emit_vf = (not has_vf) and dr == 0
        out = _rwkv_scan_grid(proj('r_r'), proj('r_k'), proj('r_v'), proj('r_wa'), pvec, mu[3 * d:][None],
                              _mx(p['rk_w2'][dr]), _mx(p['rk_a2'][dr]), init[dr], rev=bool(dr),
                              vf=v_first if has_vf else None, lora=lora, acc=acc, emit_vf=emit_vf)
        acc = (out[0], out[1])
        states.append(out[2])
        if emit_vf:
            vf_out = out[3]
    return acc[0], acc[1], tuple(states), vf_out


def _hgrn2_branch(proj, p, init):
    acc, states = None, []
    for dr in (0, 1):
        acc, s = _hgrn_scan(proj('h_q'), proj('h_f'), proj('h_i'), p['hg_f_b'][dr][None], p['hg_lb'][dr][None],
                            init[dr], rev=bool(dr), acc=acc)
        states.append(s)
    return acc, tuple(states)


def _mixer(h, p, init, on_grid, v_first, need_out, xs, gx, final_w):
    b, l, d = h.shape
    h2 = _mx(h.reshape(b * l, d))
    cols = {}
    for (w, first), (store_mx, _) in zip(p['w_in'], _COL_GROUPS):
        out = _mm(h2, w, MXU_DT if store_mx else F32).reshape(b, l, w.shape[1])
        cols.update({name: (out, off) for name, off in first.items()})

    def proj(name):
        return cols[name]

    hm, st_m = _mlstm_branch(proj, p, init[0], b, l)
    if on_grid:
        yr, bon, st_r, v_first = _rwkv7_grid_branch(proj, h2, p, init[1], v_first, b, l, d)
    else:
        yr, bon, st_r, v_first = _rwkv7_seq_branch(proj, h2, p, init[1], v_first, b, l, d)
    oh, st_h = _hgrn2_branch(proj, p, init[2])
    states = (st_m, st_r, st_h)
    if not need_out:
        return None, states, v_first
    out = _epilogue(hm, proj('m_o'), proj('m_z'), yr, bon, proj('r_z'), oh, proj('h_z'), proj('gate'),
                    xs, gx, p, final_w)
    return out, states, v_first


def kernel(x, c, ctx, c_ctx, norm_w, ada_w, ada_b, w_in, gate_b, ml_conv, ml_if_b, ml_norm_w, rk_mu, rk_w0, rk_w2, rk_a0, rk_a2, rk_kk, rk_ka, rk_rk, rk_v0, rk_v1, rk_v2, rk_ln_w, rk_ln_b, hg_f_b, hg_lb, hg_norm_w, w_pm, w_pr, w_ph, w_out, final_norm_w):
    batch, _, d = x.shape
    depth = w_in.shape[0]
    lb_p = jax.nn.softmax(hg_lb.astype(F32), axis=1)
    lower_bounds = jnp.cumsum(lb_p, axis=1) - lb_p[:, :1]
    xs, cs = x, ctx
    vf_x, vf_c = None, None
    cond = jnp.concatenate([jax.nn.silu(c), jax.nn.silu(c_ctx)[None]], axis=0)
    for l in range(depth):
        last = l == depth - 1
        p = {'w_in': _pack_w_in(w_in[l]), 'gate_b': gate_b[l], 'ml_conv': ml_conv[l], 'ml_if_b': ml_if_b[l],
             'ml_norm_w': ml_norm_w[l], 'rk_mu': rk_mu[l], 'rk_w0': rk_w0[l], 'rk_w2': rk_w2[l],
             'rk_a0': rk_a0[l], 'rk_a2': rk_a2[l], 'rk_kk': rk_kk[l], 'rk_ka': rk_ka[l], 'rk_rk': rk_rk[l],
             'rk_v0': rk_v0[l - 1] if l > 0 else None, 'rk_v1': rk_v1[l - 1] if l > 0 else None,
             'rk_v2': rk_v2[l - 1] if l > 0 else None, 'rk_ln_w': rk_ln_w[l], 'rk_ln_b': rk_ln_b[l],
             'hg_f_b': hg_f_b[l], 'hg_lb': lower_bounds[:, l], 'hg_norm_w': hg_norm_w[l],
             'w_pm': w_pm[l], 'w_pr': w_pr[l], 'w_ph': w_ph[l], 'w_out': w_out[l]}
        mod = _mm(cond, ada_w[l]) + ada_b[l]
        shift_x, scale_x, gate_x = jnp.split(mod[:batch, None, :], 3, axis=-1)
        shift_c, scale_c, gate_c = jnp.split(mod[batch], 3, axis=-1)
        hc = _rms_norm(cs, norm_w[l]) * (1.0 + scale_c) + shift_c
        gc = jnp.broadcast_to(gate_c[None, None, :], (batch, 1, d))
        cs_new, st_c, vf_c = _mixer(hc, p, _zero_states(batch, d), False, vf_c, not last, cs, gc, None)
        hx = _rms_norm(xs, norm_w[l]) * (1.0 + scale_x) + shift_x
        xs, _, vf_x = _mixer(hx, p, st_c, True, vf_x, True, xs, gate_x, final_norm_w if last else None)
        if not last:
            cs = cs_new
    return xs
```

```python
import functools

import jax
import jax.numpy as jnp
from jax import lax
from jax.experimental import pallas as pl
from jax.experimental.pallas import tpu as pltpu

F32 = jnp.float32
MXU_DT = jnp.bfloat16

NORM_EPS = 1e-6
GRID_W = 64
ML_HEADS = 4
ML_CHUNK = 256
RK_HD = 64
RK_CHUNK = 64
RK_GROUP = 256
RK_BATCH = 4
RK_GN_EPS = 64e-5
HG_HEADS = 8
HG_SUB = 16
HG_TILE = 128
EP_TILE = 256
SEG_W = 256
VMEM_LIMIT = 56 * 1024 * 1024


def _nt(a, b):
    return lax.dot_general(a, b, (((1,), (1,)), ((), ())), preferred_element_type=F32)


def _tn(a, b):
    return lax.dot_general(a, b, (((0,), (0,)), ((), ())), preferred_element_type=F32)


def _nn(a, b):
    return jnp.dot(a, b, preferred_element_type=F32)


def _mx(a):
    return a.astype(MXU_DT)


def _cumsum_mm(tri, x):
    hi = x.astype(MXU_DT)
    r1 = x - hi.astype(F32)
    mid = r1.astype(MXU_DT)
    lo = (r1 - mid.astype(F32)).astype(MXU_DT)
    one = lambda t: _nn(t, hi) + _nn(t, mid) + _nn(t, lo)
    return [one(t) for t in tri] if isinstance(tri, (list, tuple)) else one(tri)


def _seg_ones(seg):
    sh = seg.bit_length() - 1
    ri = lax.broadcasted_iota(jnp.int32, (SEG_W, SEG_W), 0)
    ci = lax.broadcasted_iota(jnp.int32, (SEG_W, SEG_W), 1)
    return ((ri >> sh) == (ci >> sh)).astype(F32).astype(MXU_DT)


def _segsum(x, ones):
    outs = [_nn(_mx(x[:, g * SEG_W:(g + 1) * SEG_W]), ones) for g in range(x.shape[1] // SEG_W)]
    return outs[0] if len(outs) == 1 else jnp.concatenate(outs, axis=1)


def _sigmoid(x):
    return jax.nn.sigmoid(x)


def _silu(x):
    return x * jax.nn.sigmoid(x)


def _params(sem):
    return pltpu.CompilerParams(dimension_semantics=sem, vmem_limit_bytes=VMEM_LIMIT)


def _mm_kernel(x_ref, w_ref, o_ref):
    o_ref[...] = _nn(_mx(x_ref[...]), w_ref[...]).astype(o_ref.dtype)


def _mm(x, w, out_dtype=F32):
    m, k = x.shape
    n = w.shape[1]
    mp = -(-m // 8) * 8
    npad = -(-n // 128) * 128
    if mp != m:
        x = jnp.pad(x, ((0, mp - m), (0, 0)))
    wb = w.astype(MXU_DT)
    if npad != n:
        wb = jnp.pad(wb, ((0, 0), (0, npad - n)))
    tm = next(t for t in (2048, 1024, 512, 256, 128, 64, 32, 16, 8) if mp % t == 0)
    tn = next(t for t in (1024, 512, 256, 128) if npad % t == 0)
    out = pl.pallas_call(
        _mm_kernel,
        grid=(mp // tm, npad // tn),
        in_specs=[pl.BlockSpec((tm, k), lambda i, j: (i, 0)),
                  pl.BlockSpec((k, tn), lambda i, j: (0, j))],
        out_specs=pl.BlockSpec((tm, tn), lambda i, j: (i, j)),
        out_shape=jax.ShapeDtypeStruct((mp, npad), out_dtype),
        compiler_params=_params(("parallel", "parallel")),
        name="mm",
    )(x, wb)
    if mp != m or npad != n:
        out = out[:m, :n]
    return out


def _mlstm_kernel(*refs, rev, heads, hd, nc, has_acc, conv):
    it = iter(refs)
    if conv:
        q_ref, qp_ref, qn_ref, k_ref, kp_ref, kn_ref, v_ref, cw_ref = (next(it) for _ in range(8))
    else:
        q_ref, k_ref, v_ref = (next(it) for _ in range(3))
    gcol_ref, grow_ref, c0_ref, n0_ref, m0_ref = (next(it) for _ in range(5))
    acc_ref = next(it) if has_acc else None
    h_ref, c_ref, n_ref, m_ref = (next(it) for _ in range(4))
    step = pl.program_id(1)

    @pl.when(step == 0)
    def _():
        c_ref[...] = c0_ref[...]
        n_ref[...] = n0_ref[...]
        m_ref[...] = m0_ref[...]

    t = q_ref.shape[1]
    ci = nc - 1 - step if rev else step
    has_prev = jnp.where(ci > 0, 1.0, 0.0)
    has_next = jnp.where(ci < nc - 1, 1.0, 0.0)
    trow = lax.broadcasted_iota(jnp.int32, (t, 1), 0)

    def conv_silu(u, prev_blk, next_blk, w3):
        up = jnp.where(trow == 0, prev_blk[7:8, :] * has_prev, pltpu.roll(u, 1, 0))
        dn = jnp.where(trow == t - 1, next_blk[0:1, :] * has_next, pltpu.roll(u, t - 1, 0))
        return _silu(up * w3[0:1, :] + u * w3[1:2, :] + dn * w3[2:3, :])

    if conv:
        q_all = conv_silu(q_ref[0], qp_ref[0], qn_ref[0], cw_ref[0:3, :]) * (hd ** -0.5)
        k_all = conv_silu(k_ref[0], kp_ref[0], kn_ref[0], cw_ref[3:6, :])
        qo_ref, ko_ref = next(it), next(it)
        qo_ref[0] = q_all
        ko_ref[0] = k_all
    else:
        q_all, k_all = q_ref[0], k_ref[0]

    row = lax.broadcasted_iota(jnp.int32, (t, t), 0)
    col = lax.broadcasted_iota(jnp.int32, (t, t), 1)
    mask = (col >= row) if rev else (col <= row)
    last = 0 if rev else t - 1
    hs = range(heads)
    lanes = [slice(h * hd, (h + 1) * hd) for h in hs]
    q32 = [q_all[:, sl] for sl in lanes]
    qc = [_mx(x) for x in q32]
    kc = [k_all[:, sl] for sl in lanes]
    vc = [_mx(v_ref[0, :, sl]) for sl in lanes]
    bc_col = [gcol_ref[0, :, h:h + 1] for h in hs]
    i_col = [gcol_ref[0, :, heads + h:heads + h + 1] for h in hs]
    m_prev = [m_ref[0, h, 0:1, 0:1] for h in hs]
    c_mem = [c_ref[0, h] for h in hs]
    n_mem = [n_ref[0, h] for h in hs]
    dmat = [jnp.where(mask, bc_col[h] - grow_ref[0, 0, h:h + 1, :] + grow_ref[0, 0, heads + h:heads + h + 1, :],
                      -jnp.inf) for h in hs]
    inter = [bc_col[h] + m_prev[h] for h in hs]
    m_t = [jnp.maximum(jnp.max(dmat[h], axis=-1, keepdims=True), inter[h]) for h in hs]
    qk = [_nt(qc[h], _mx(kc[h])) for h in hs]
    q_c = [_nn(qc[h], _mx(c_mem[h])) for h in hs]
    s = [qk[h] * jnp.exp(dmat[h] - m_t[h]) for h in hs]
    w_inter = [jnp.exp(inter[h] - m_t[h]) for h in hs]
    num = [_nn(_mx(s[h]), vc[h]) + w_inter[h] * q_c[h] for h in hs]
    for h in hs:
        qn = jnp.sum(q32[h] * n_mem[h], axis=-1, keepdims=True)
        den = jnp.sum(s[h], axis=-1, keepdims=True) + w_inter[h] * qn
        h_out = num[h] * (1.0 / jnp.maximum(jnp.abs(den), jnp.exp(-m_t[h])))
        if has_acc:
            h_out = h_out + acc_ref[0, :, lanes[h]]
        h_ref[0, :, lanes[h]] = h_out
    for h in hs:
        total = bc_col[h][last:last + 1, :]
        g_col = total - bc_col[h] + i_col[h]
        m_new = jnp.maximum(total + m_prev[h], jnp.max(g_col, axis=0, keepdims=True))
        wk = jnp.exp(g_col - m_new)
        dec = jnp.exp(total + m_prev[h] - m_new)
        kw = kc[h] * wk
        c_ref[0, h] = dec * c_mem[h] + _tn(_mx(kw), vc[h])
        n_ref[0, h] = dec * n_mem[h] + jnp.sum(kw, axis=0, keepdims=True)
        m_ref[0, h] = jnp.broadcast_to(m_new, m_ref.shape[2:])


def _mlstm_scan(pq, pk, pv, conv_w, log_i, log_f, state, rev, acc=None, qk=None):
    b, l, _ = pq[0].shape
    w = conv_w.shape[-1]
    heads = ML_HEADS
    hd = w // heads
    t = min(ML_CHUNK, l)
    assert l % t == 0
    nc = l // t
    nb8 = l // 8
    lf = log_f.reshape(b, nc, t, heads)
    if rev:
        bcum = jnp.flip(jnp.cumsum(jnp.flip(lf, 2), axis=2), 2)
    else:
        bcum = jnp.cumsum(lf, axis=2)
    gcol = jnp.concatenate([bcum.reshape(b, l, heads), log_i], axis=-1)
    grow = gcol.reshape(b, nc, t, 2 * heads).transpose(0, 1, 3, 2)
    ch = (lambda c: nc - 1 - c) if rev else (lambda c: c)
    cidx = lambda bi, c: (bi, ch(c), 0)
    ridx = lambda bi, c: (bi, ch(c), 0, 0)
    sidx = lambda bi, c: (bi, 0, 0, 0)
    c0, n0, m0 = state
    seq = pl.BlockSpec((1, t, w), cidx)
    col = lambda x: pl.BlockSpec((1, t, w), lambda bi, c: (bi, ch(c), x[1] // w))
    prv = lambda x: pl.BlockSpec((1, 8, w), lambda bi, c: (bi, jnp.maximum(ch(c) * (t // 8) - 1, 0), x[1] // w))
    nxt = lambda x: pl.BlockSpec((1, 8, w), lambda bi, c: (bi, jnp.minimum((ch(c) + 1) * (t // 8), nb8 - 1), x[1] // w))
    st_specs = [pl.BlockSpec((1, heads, hd, hd), sidx), pl.BlockSpec((1, heads, 1, hd), sidx),
                pl.BlockSpec((1, heads, 8, 128), sidx)]
    conv = qk is None
    if conv:
        in_specs = [col(pq), prv(pq), nxt(pq), col(pk), prv(pk), nxt(pk), col(pv),
                    pl.BlockSpec((6, w), lambda bi, c: (0, 0))]
        args = [pq[0], pq[0], pq[0], pk[0], pk[0], pk[0], pv[0], conv_w.reshape(6, w)]
    else:
        in_specs = [seq, seq, col(pv)]
        args = [qk[0], qk[1], pv[0]]
    in_specs += [pl.BlockSpec((1, t, 2 * heads), cidx), pl.BlockSpec((1, 1, 2 * heads, t), ridx)] + st_specs
    args += [gcol, grow, c0, n0, m0]
    if acc is not None:
        in_specs.append(seq)
        args.append(acc)
    seq_out = jax.ShapeDtypeStruct((b, l, w), F32)
    out = pl.pallas_call(
        functools.partial(_mlstm_kernel, rev=rev, heads=heads, hd=hd, nc=nc, has_acc=acc is not None, conv=conv),
        grid=(b, nc),
        in_specs=in_specs,
        out_specs=[seq] + st_specs + ([seq, seq] if conv else []),
        out_shape=[seq_out, jax.ShapeDtypeStruct(c0.shape, F32), jax.ShapeDtypeStruct(n0.shape, F32),
                   jax.ShapeDtypeStruct(m0.shape, F32)] + ([seq_out, seq_out] if conv else []),
        compiler_params=_params(("parallel", "arbitrary")),
        name="mlstm_rev" if rev else "mlstm_fwd",
    )(*args)
    return out[0], (out[1], out[2], out[3]), ((out[4], out[5]) if conv else qk)


def _rwkv_core(rows, s_ref, rev):
    c, w = rows[0][0].shape
    g_w = RK_GROUP
    nh = g_w // RK_HD
    assert c == RK_HD and c & (c - 1) == 0
    sh = c.bit_length() - 1
    rj = lax.broadcasted_iota(jnp.int32, (g_w, g_w), 0)
    cj = lax.broadcasted_iota(jnp.int32, (g_w, g_w), 1)
    blk = (rj >> sh) == (cj >> sh)
    tt, ss = rj & (c - 1), cj & (c - 1)
    incl = blk & ((ss >= tt) if rev else (ss <= tt))
    strict = blk & ((ss > tt) if rev else (ss < tt))
    eye = (rj == cj).astype(F32)
    tri = incl[:c, :c].astype(F32).astype(MXU_DT)
    lo_half = lax.broadcasted_iota(jnp.int32, (2 * g_w, 2 * c), 1) < c
    last = 0 if rev else c - 1
    zero = jnp.zeros((), F32)
    chains = [(bi, g) for bi in range(len(rows)) for g in range(w // g_w)]

    def bd(x):
        return _mx(jnp.where(blk, jnp.concatenate([x] * nh, axis=0), zero))

    def rsum(z):
        out = z[0:c]
        for i in range(1, nh):
            out = out + z[i * c:(i + 1) * c]
        return out

    a_bd, r_bd, v_bd, bk_t, r_t, v_g, bk_hat, dec = {}, {}, {}, {}, {}, {}, {}, {}
    for bi, (r, lw, kd, v, kk, a) in enumerate(rows):
        ka = kk * a
        gi = _cumsum_mm(tri, lw)
        tot = gi[last:last + 1, :]
        e_neg = jnp.exp(-gi)
        e_tail = jnp.exp(tot - gi)
        full = {'a': -kk * jnp.exp(gi - lw), 'b': ka * e_neg, 'k': kd * e_neg, 'r': r * jnp.exp(gi),
                'bh': ka * e_tail, 'kh': kd * e_tail, 'dec': jnp.exp(tot), 'v': v}
        for g in range(w // g_w):
            x = {n: t[:, g * g_w:(g + 1) * g_w] for n, t in full.items()}
            ch = (bi, g)
            a_bd[ch], r_bd[ch], v_bd[ch] = bd(x['a']), bd(x['r']), bd(x['v'])
            bk_t[ch] = _mx(jnp.concatenate([x['b'], x['k']], axis=0))
            bk_hat[ch] = _mx(jnp.concatenate([x['bh'], x['kh']], axis=0))
            r_t[ch], v_g[ch], dec[ch] = _mx(x['r']), x['v'], x['dec']

    l_ab, a_ak, a_rb, a_rk = {}, {}, {}, {}
    for ch in chains:
        prod = _nt(jnp.concatenate([a_bd[ch], r_bd[ch]], axis=0), bk_t[ch])
        swp = pltpu.roll(prod, c, 1)
        xb = jnp.where(lo_half, prod, swp)
        xk = jnp.where(lo_half, swp, prod)
        xb = jnp.concatenate([xb, xb], axis=1)
        xk = jnp.concatenate([xk, xk], axis=1)
        l_ab[ch] = jnp.where(strict, xb[:g_w], zero)
        a_ak[ch] = _mx(jnp.where(strict, xk[:g_w], zero))
        a_rb[ch] = _mx(rsum(jnp.where(incl, xb[g_w:], zero)))
        a_rk[ch] = _mx(rsum(jnp.where(incl, xk[g_w:], zero)))

    xs = {ch: _mx(l_ab[ch]) for ch in chains}
    ps = {ch: eye + l_ab[ch] for ch in chains}
    for _ in range(max(1, (c - 1).bit_length() - 1)):
        xs = {ch: _mx(_nn(xs[ch], xs[ch])) for ch in chains}
        ps = {ch: ps[ch] + _nn(_mx(ps[ch]), xs[ch]) for ch in chains}

    t_cat = {ch: _mx(rsum(ps[ch])) for ch in chains}
    w_cat = {ch: _mx(_nn(t_cat[ch], a_bd[ch])) for ch in chains}
    ta = {ch: _mx(_nn(t_cat[ch], a_ak[ch])) for ch in chains}
    u0 = {ch: _nn(ta[ch], v_bd[ch]) for ch in chains}
    y_k = {ch: _nn(a_rk[ch], v_bd[ch]) for ch in chains}
    s_mem = {ch: s_ref[ch[0], ch[1]] for ch in chains}
    s_b = {ch: _mx(s_mem[ch]) for ch in chains}
    u = {ch: _nt(w_cat[ch], s_b[ch]) + u0[ch] for ch in chains}
    y = {ch: _nt(r_t[ch], s_b[ch]) + _nn(a_rb[ch], bd(u[ch])) + y_k[ch] for ch in chains}
    for ch in chains:
        upd = _tn(_mx(jnp.concatenate([u[ch], v_g[ch]], axis=0)), bk_hat[ch])
        s_ref[ch[0], ch[1]] = s_mem[ch] * dec[ch] + jnp.where(blk, upd, zero)
    return [jnp.concatenate([y[(bi, g)] for g in range(w // g_w)], axis=1) for bi in range(len(rows))]


def _rwkv_seq_kernel(*refs, rev, has_acc):
    r_ref, lw_ref, kd_ref, v_ref, kk_ref, a_ref, s0_ref = refs[:7]
    acc_ref = refs[7] if has_acc else None
    y_ref, s_ref = refs[7 + has_acc:]

    @pl.when(pl.program_id(1) == 0)
    def _():
        s_ref[...] = s0_ref[...]

    y, = _rwkv_core([(r_ref[0], lw_ref[0], kd_ref[0], v_ref[0], kk_ref[0], a_ref[0])], s_ref, rev)
    if has_acc:
        y = y + acc_ref[0]
    y_ref[0] = y


def _rwkv_scan_seq(r, lw, kd, v, kk, a, state, rev, acc=None):
    b, l, w = r.shape
    c = RK_CHUNK
    nc = l // c
    cidx = (lambda bi, ch: (bi, nc - 1 - ch, 0)) if rev else (lambda bi, ch: (bi, ch, 0))
    seq = pl.BlockSpec((1, c, w), cidx)
    st = pl.BlockSpec((1,) + state.shape[1:], lambda bi, ch: (bi, 0, 0, 0))
    args = [r, lw, kd, v, kk, a, state] + ([acc] if acc is not None else [])
    y, s = pl.pallas_call(
        functools.partial(_rwkv_seq_kernel, rev=rev, has_acc=acc is not None),
        grid=(b, nc),
        in_specs=[seq] * 6 + [st] + ([seq] if acc is not None else []),
        out_specs=[seq, st],
        out_shape=[jax.ShapeDtypeStruct((b, l, w), F32), jax.ShapeDtypeStruct(state.shape, F32)],
        compiler_params=_params(("parallel", "arbitrary")),
        name="rwkv_seq_rev" if rev else "rwkv_seq_fwd",
    )(*args)
    return y, s


_PV_ROWS = ('mu_r', 'mu_k', 'mu_v', 'w0', 'a0', 'kk', 'ka', 'rk', 'v0')


def _rwkv_grid_kernel(*refs, rev, nc, has_vf, has_acc, emit_vf):
    it = iter(refs)
    pr, pr_u, pr_d, pk, pk_u, pk_d, pv, pv_u, pv_d, wa, wa_u, wa_d = (next(it) for _ in range(12))
    vf_ref, lora_ref = (next(it), next(it)) if has_vf else (None, None)
    pvec, mu_wa, w2_ref, a2_ref, s0_ref = (next(it) for _ in range(5))
    accy_ref, accb_ref = (next(it), next(it)) if has_acc else (None, None)
    y_ref, bon_ref, s_ref = (next(it) for _ in range(3))
    vfo_ref = next(it) if emit_vf else None
    step = pl.program_id(1)

    @pl.when(step == 0)
    def _():
        s_ref[...] = s0_ref[...]

    c = pr.shape[1]
    w = pr.shape[2]
    rd = w // 16
    ci = nc - 1 - step if rev else step
    has_up = jnp.where(ci > 0, 1.0, 0.0)
    has_dn = jnp.where(ci < nc - 1, 1.0, 0.0)
    trow = lax.broadcasted_iota(jnp.int32, (c, 1), 0)
    prow = lambda name: pvec[_PV_ROWS.index(name):_PV_ROWS.index(name) + 1, :]

    def left(x):
        return jnp.where(trow == 0, 0.0, pltpu.roll(x, 1, 0))

    def right(x):
        return jnp.where(trow == c - 1, 0.0, pltpu.roll(x, c - 1, 0))

    ones = _seg_ones(RK_HD)
    d = 1 if rev else 0
    rows, bons = [], []
    for bi in range(pr.shape[0]):
        def lerp_wide(cur_ref, up_ref, dn_ref, mu):
            cur = cur_ref[bi]
            q = w // 4
            sh = jnp.concatenate([left(cur[:, :q]), right(cur[:, q:2 * q]),
                                  up_ref[bi] * has_up, dn_ref[bi] * has_dn], axis=1)
            return cur + mu * (sh - cur)

        r = lerp_wide(pr, pr_u, pr_d, prow('mu_r'))
        k = lerp_wide(pk, pk_u, pk_d, prow('mu_k'))
        v = lerp_wide(pv, pv_u, pv_d, prow('mu_v'))

        cur = wa[bi]
        qd = (lax.broadcasted_iota(jnp.int32, cur.shape, 1) & (rd - 1)) >> ((rd // 4).bit_length() - 1)
        sh = jnp.where(qd == 0, left(cur), jnp.where(qd == 1, right(cur),
                                                     jnp.where(qd == 2, wa_u[bi] * has_up, wa_d[bi] * has_dn)))
        xwa = cur + mu_wa[...] * (sh - cur)
        xw = xwa[:, d * rd:(d + 1) * rd]
        xa = xwa[:, (2 + d) * rd:(3 + d) * rd]
        z = -(prow('w0') + _nn(_mx(jnp.tanh(xw)), w2_ref[...]))
        softplus = jnp.maximum(z, 0.0) + jnp.log(1.0 + jnp.exp(-jnp.abs(z)))
        lw = -jnp.exp(-softplus - 0.5)
        a = _sigmoid(prow('a0') + _nn(_mx(xa), a2_ref[...]))
        if has_vf:
            v = v + (vf_ref[bi] - v) * _sigmoid(prow('v0') + lora_ref[bi])
        if emit_vf:
            vfo_ref[bi] = v
        kkn = k * prow('kk')
        kk = kkn * lax.rsqrt(jnp.maximum(_segsum(kkn * kkn, ones), 1e-24))
        kd = k * (1.0 + (a - 1.0) * prow('ka'))
        bons.append(_segsum(r * kd * prow('rk'), ones) * v)
        rows.append((r, lw, kd, v, kk, a))
    ys = _rwkv_core(rows, s_ref, rev)
    for bi, (y, bon) in enumerate(zip(ys, bons)):
        if has_acc:
            y = y + accy_ref[bi]
            bon = bon + accb_ref[bi]
        y_ref[bi] = y
        bon_ref[bi] = bon


def _rwkv_scan_grid(pr, pk, pv, wa, pvec, mu_wa, w2, a2, state, rev, vf=None, lora=None,
                    acc=None, emit_vf=False):
    b, l, _ = pr[0].shape
    w = pvec.shape[1]
    c = RK_CHUNK
    assert c == GRID_W
    nc = l // c
    wq = w // 4
    ch = (lambda s: nc - 1 - s) if rev else (lambda s: s)
    cur = lambda lane: (lambda bi, s: (bi, ch(s), lane))
    upi = lambda lane: (lambda bi, s: (bi, jnp.maximum(ch(s) - 1, 0), lane))
    dni = lambda lane: (lambda bi, s: (bi, jnp.minimum(ch(s) + 1, nc - 1), lane))
    bb = RK_BATCH if b % RK_BATCH == 0 else 1
    seq = pl.BlockSpec((bb, c, w), cur(0))
    wide = lambda x: [pl.BlockSpec((bb, c, w), cur(x[1] // w)), pl.BlockSpec((bb, c, wq), upi(x[1] // wq + 2)),
                      pl.BlockSpec((bb, c, wq), dni(x[1] // wq + 3))]
    wa_w = mu_wa.shape[1]
    small = [pl.BlockSpec((bb, c, wa_w), cur(wa[1] // wa_w)), pl.BlockSpec((bb, c, wa_w), upi(wa[1] // wa_w)),
             pl.BlockSpec((bb, c, wa_w), dni(wa[1] // wa_w))]
    full = lambda x: pl.BlockSpec(x.shape, lambda bi, s: (0,) * x.ndim)
    st = pl.BlockSpec((bb,) + state.shape[1:], lambda bi, s: (bi, 0, 0, 0))
    in_specs = wide(pr) + wide(pk) + wide(pv) + small
    args = [pr[0]] * 3 + [pk[0]] * 3 + [pv[0]] * 3 + [wa[0]] * 3
    if vf is not None:
        in_specs += [seq, seq]
        args += [vf, lora]
    in_specs += [full(pvec), full(mu_wa), full(w2), full(a2), st]
    args += [pvec, mu_wa, w2, a2, state]
    if acc is not None:
        in_specs += [seq, seq]
        args += list(acc)
    out_specs = [seq, seq, st]
    out_shape = [jax.ShapeDtypeStruct((b, l, w), F32), jax.ShapeDtypeStruct((b, l, w), F32),
                 jax.ShapeDtypeStruct(state.shape, F32)]
    if emit_vf:
        out_specs.append(seq)
        out_shape.append(jax.ShapeDtypeStruct((b, l, w), F32))
    return pl.pallas_call(
        functools.partial(_rwkv_grid_kernel, rev=rev, nc=nc, has_vf=vf is not None,
                          has_acc=acc is not None, emit_vf=emit_vf),
        grid=(b // bb, nc),
        in_specs=in_specs,
        out_specs=out_specs,
        out_shape=out_shape,
        compiler_params=_params(("parallel", "arbitrary")),
        name="rwkv_grid_rev" if rev else "rwkv_grid_fwd",
    )(*args)


def _hgrn_kernel(*refs, rev, heads, hd, has_acc):
    q_ref, f_ref, v_ref, fb_ref, lb_ref, s0_ref = refs[:6]
    acc_ref = refs[6] if has_acc else None
    o_ref, s_ref = refs[6 + has_acc:]

    @pl.when(pl.program_id(1) == 0)
    def _():
        s_ref[...] = s0_ref[...]

    tt = q_ref.shape[1]
    sub = HG_SUB
    sh = sub.bit_length() - 1
    nsub = tt // sub
    mid = sub // 2 if rev else sub // 2 - 1
    last = 0 if rev else sub - 1
    ri = lax.broadcasted_iota(jnp.int32, (tt, tt), 0)
    ci = lax.broadcasted_iota(jnp.int32, (tt, tt), 1)
    causal = ((ri >> sh) == (ci >> sh)) & ((ci >= ri) if rev else (ci <= ri))
    lb = lb_ref[...]
    f = lb + (1.0 - lb) * _sigmoid(f_ref[0] + fb_ref[...])
    g_all = _cumsum_mm(causal.astype(F32).astype(MXU_DT), jnp.log(f))

    def row_of_each_sub(r):
        return jnp.concatenate([jnp.broadcast_to(g_all[j * sub + r:j * sub + r + 1], (sub, g_all.shape[1]))
                                for j in range(nsub)], axis=0)

    g_mid, g_last = row_of_each_sub(mid), row_of_each_sub(last)
    k_all = 1.0 - f
    q_all = _silu(q_ref[0])
    q1 = _mx(q_all * jnp.exp(g_all - g_mid))
    k1 = _mx(k_all * jnp.exp(g_mid - g_all))
    qg_f = q_all * jnp.exp(g_all)
    kl_f = k_all * jnp.exp(g_last - g_all)
    v_all = _mx(v_ref[0])
    assert nsub % 2 == 0
    is_first = lambda j: (j % 2 == 1) == rev
    dec = [jnp.exp(g_all[j * sub + last:j * sub + last + 1]) for j in range(nsub)]
    one_row = jnp.ones_like(dec[0])
    per_sub = lambda vals: jnp.concatenate([jnp.broadcast_to(x, (sub, x.shape[1])) for x in vals], axis=0)
    qg, kl = _mx(qg_f), _mx(kl_f)
    qg2 = _mx(qg_f * per_sub([one_row if is_first(j) else dec[j ^ 1] for j in range(nsub)]))
    kl2 = _mx(kl_f * per_sub([dec[j ^ 1] if is_first(j) else one_row for j in range(nsub)]))
    sub_r, sub_c = ri >> sh, ci >> sh
    second_sees_first = ((sub_r ^ 1) == sub_c) & ((sub_r & 1) == (0 if rev else 1))
    lanes = [slice(h * hd, (h + 1) * hd) for h in range(heads)]
    zero = jnp.zeros((), F32)
    att = [_mx(jnp.where(causal, _nt(q1[:, sl], k1[:, sl]), zero)
               + jnp.where(second_sees_first, _nt(qg[:, sl], kl[:, sl]), zero)) for sl in lanes]
    intra = [_nn(att[h], v_all[:, lanes[h]]) for h in range(heads)]
    pairs = [nsub // 2 - 1 - s if rev else s for s in range(nsub // 2)]
    rows = [slice(2 * p * sub, 2 * (p + 1) * sub) for p in range(nsub // 2)]
    upd = {(p, h): _tn(v_all[rows[p], lanes[h]], kl2[rows[p], lanes[h]]) for p in pairs for h in range(heads)}
    s_mem = [s_ref[0, h] for h in range(heads)]
    for p in pairs:
        dec_pair = dec[2 * p] * dec[2 * p + 1]
        outs = []
        for h in range(heads):
            outs.append(intra[h][rows[p]] + _nt(qg2[rows[p], lanes[h]], _mx(s_mem[h])))
            s_mem[h] = s_mem[h] * dec_pair[:, lanes[h]] + upd[(p, h)]
        o = jnp.concatenate(outs, axis=1)
        if has_acc:
            o = o + acc_ref[0, rows[p], :]
        o_ref[0, rows[p], :] = o
    for h in range(heads):
        s_ref[0, h] = s_mem[h]


def _hgrn_scan(pq, pf, pi, f_b, lb, state, rev, acc=None):
    b, l, _ = pq[0].shape
    w = f_b.shape[1]
    heads = HG_HEADS
    hd = w // heads
    tt = min(HG_TILE, l)
    nc = l // tt
    d = 1 if rev else 0
    ch = (lambda s: nc - 1 - s) if rev else (lambda s: s)
    col = lambda x, extra=0: pl.BlockSpec((1, tt, w), lambda bi, s: (bi, ch(s), x[1] // w + extra))
    seq = pl.BlockSpec((1, tt, w), lambda bi, s: (bi, ch(s), 0))
    vec = pl.BlockSpec((1, w), lambda bi, s: (0, 0))
    st = pl.BlockSpec((1,) + state.shape[1:], lambda bi, s: (bi, 0, 0, 0))
    args = [pq[0], pf[0], pi[0], f_b, lb, state] + ([acc] if acc is not None else [])
    o, s = pl.pallas_call(
        functools.partial(_hgrn_kernel, rev=rev, heads=heads, hd=hd, has_acc=acc is not None),
        grid=(b, nc),
        in_specs=[col(pq), col(pf, d), col(pi), vec, vec, st] + ([seq] if acc is not None else []),
        out_specs=[seq, st],
        out_shape=[jax.ShapeDtypeStruct((b, l, w), F32), jax.ShapeDtypeStruct(state.shape, F32)],
        compiler_params=_params(("parallel", "arbitrary")),
        name="hgrn_rev" if rev else "hgrn_fwd",
    )(*args)
    return o, s


_EV_ROWS = ('ml_norm_w', 'rk_ln_w', 'rk_ln_b', 'hg_norm_w', 'gate_b0', 'gate_b1', 'gate_b2', 'final_w', 'next_w')


def _epilogue_kernel(hm_ref, mo_ref, mz_ref, yr_ref, bon_ref, rz_ref, oh_ref, hz_ref,
                     g0_ref, g1_ref, g2_ref, xs_ref, gx_ref, ev_ref,
                     wpm_ref, wpr_ref, wph_ref, wout_ref, *rest, final, has_next):
    o_ref = rest[2 * has_next]
    ev = lambda name: ev_ref[_EV_ROWS.index(name):_EV_ROWS.index(name) + 1, :]
    d = hm_ref.shape[2]

    def seg_mean(x, seg):
        if seg % 128:
            return _segsum(x, _seg_ones(seg)) * (1.0 / seg)
        parts = [jnp.broadcast_to(jnp.mean(x[:, i:i + seg], axis=-1, keepdims=True), (x.shape[0], seg))
                 for i in range(0, d, seg)]
        return jnp.concatenate(parts, axis=1)

    seg = d // ML_HEADS
    y = hm_ref[0]
    y = y - seg_mean(y, seg)
    y = y * lax.rsqrt(seg_mean(y * y, seg) + NORM_EPS) * ev('ml_norm_w')
    u_m = _sigmoid(mo_ref[0].astype(F32)) * y * _silu(mz_ref[0].astype(F32))
    y = yr_ref[0]
    y = y - seg_mean(y, RK_HD)
    y = y * lax.rsqrt(seg_mean(y * y, RK_HD) + RK_GN_EPS) * ev('rk_ln_w') + ev('rk_ln_b') + bon_ref[0]
    u_r = y * _silu(rz_ref[0].astype(F32))
    seg = d // HG_HEADS
    o = oh_ref[0]
    o = o * lax.rsqrt(seg_mean(o * o, seg) + NORM_EPS) * ev('hg_norm_w')
    u_h = o * _silu(hz_ref[0].astype(F32))

    merged = (_sigmoid(g0_ref[0].astype(F32) + ev('gate_b0')) * _nn(_mx(u_m), wpm_ref[...])
              + _sigmoid(g1_ref[0].astype(F32) + ev('gate_b1')) * _nn(_mx(u_r), wpr_ref[...])
              + _sigmoid(g2_ref[0].astype(F32) + ev('gate_b2')) * _nn(_mx(u_h), wph_ref[...]))
    xs = xs_ref[0] + gx_ref[0] * _nn(_mx(merged), wout_ref[...])
    if final:
        xs = xs * lax.rsqrt(jnp.mean(xs * xs, axis=-1, keepdims=True) + NORM_EPS) * ev('final_w')
    o_ref[0] = xs
    if has_next:
        normed = xs * lax.rsqrt(jnp.mean(xs * xs, axis=-1, keepdims=True) + NORM_EPS) * ev('next_w')
        rest[3][0] = _mx(normed * rest[0][0] + rest[1][0])


def _epilogue(hm, mo, mz, yr, bon, rz, oh, hz, gate, xs, gx, p, final_w, nxt):
    b, l, d = xs.shape
    tt = min(EP_TILE, l)
    seq = pl.BlockSpec((1, tt, d), lambda bi, i: (bi, i, 0))
    col = lambda x, extra=0: pl.BlockSpec((1, tt, d), lambda bi, i: (bi, i, x[1] // d + extra))
    ev = jnp.stack([p['ml_norm_w'], p['rk_ln_w'], p['rk_ln_b'], p['hg_norm_w'],
                    p['gate_b'][0], p['gate_b'][1], p['gate_b'][2],
                    final_w if final_w is not None else jnp.ones((d,), F32),
                    nxt[0] if nxt is not None else jnp.ones((d,), F32)], axis=0)
    wspec = pl.BlockSpec((d, d), lambda bi, i: (0, 0))
    ws = [_mx(p[n]) for n in ('w_pm', 'w_pr', 'w_ph', 'w_out')]
    per_batch = pl.BlockSpec((1, 1, d), lambda bi, i: (bi, 0, 0))
    has_next = nxt is not None
    out = pl.pallas_call(
        functools.partial(_epilogue_kernel, final=final_w is not None, has_next=has_next),
        grid=(b, l // tt),
        in_specs=[seq, col(mo), col(mz), seq, seq, col(rz), seq, col(hz), col(gate), col(gate, 1), col(gate, 2), seq,
                  per_batch, pl.BlockSpec(ev.shape, lambda bi, i: (0, 0))] + [wspec] * 4 + [per_batch] * (2 * has_next),
        out_specs=[seq] * (1 + has_next),
        out_shape=[jax.ShapeDtypeStruct((b, l, d), F32)] + [jax.ShapeDtypeStruct((b, l, d), MXU_DT)] * has_next,
        compiler_params=_params(("parallel", "parallel")),
        name="epilogue",
    )(hm, mo[0], mz[0], yr, bon, rz[0], oh, hz[0], gate[0], gate[0], gate[0], xs, gx, ev, *ws,
      *(nxt[1:] if has_next else ()))
    return out[0], (out[1] if has_next else None)


def _rms_norm(x, w):
    return x * lax.rsqrt(jnp.mean(x * x, axis=-1, keepdims=True) + NORM_EPS) * w


def _bi_shift_seq(u):
    half = u.shape[-1] // 2
    g = jnp.pad(u, ((0, 0), (1, 1), (0, 0)))
    return jnp.concatenate((g[:, :-2, :half], g[:, 2:, half:]), axis=-1)


def _in_layout(d):
    rd = d // 16
    return (('m_q', d), ('m_k', d), ('m_v', d), ('m_o', d), ('m_z', d), ('m_if', 4 * ML_HEADS),
            ('r_r', d), ('r_k', d), ('r_v', d), ('r_z', d), ('r_wa', 4 * rd),
            ('h_q', d), ('h_f', 2 * d), ('h_i', d), ('h_z', d), ('gate', 3 * d))


_COL_GROUPS = ((False, ('m_q', 'm_k', 'r_r', 'r_k', 'r_v', 'h_q', 'h_f')),
               (False, ('r_wa', 'm_if')),
               (True, ('m_v', 'h_i', 'm_o', 'm_z', 'r_z', 'h_z', 'gate')))
_COL_ALIGN = 512


def _pack_w_in(w_in):
    d = w_in.shape[0]
    src, off = {}, 0
    for name, width in _in_layout(d):
        src[name] = (off, width)
        off += width

    def pack(names):
        cols, first, o = [], {}, 0
        for n in names:
            cols.append(w_in[:, src[n][0]:src[n][0] + src[n][1]])
            first[n] = o
            o += src[n][1]
        if o % _COL_ALIGN:
            cols.append(jnp.zeros((d, -o % _COL_ALIGN), w_in.dtype))
        return jnp.concatenate(cols, axis=1).astype(MXU_DT), first

    return [pack(names) for _, names in _COL_GROUPS]


def _take(col, width):
    return col[0][..., col[1]:col[1] + width]


def _zero_states(b, d):
    ml_hd = d // ML_HEADS
    ml = (jnp.zeros((b, ML_HEADS, ml_hd, ml_hd), F32), jnp.zeros((b, ML_HEADS, 1, ml_hd), F32),
          jnp.zeros((b, ML_HEADS, 8, 128), F32))
    rk = jnp.zeros((b, d // RK_GROUP, RK_GROUP, RK_GROUP), F32)
    hg_hd = d // HG_HEADS
    hg = jnp.zeros((b, HG_HEADS, hg_hd, hg_hd), F32)
    return ((ml, ml), (rk, rk), (hg, hg))


def _mlstm_branch(proj, p, init, b, l):
    gl = _take(proj('m_if'), 4 * ML_HEADS).reshape(b, l, 2, 2, ML_HEADS) + p['ml_if_b']
    acc, qk, states = None, None, []
    for dr in (0, 1):
        log_i = gl[:, :, dr, 0]
        log_f = jax.nn.log_sigmoid(gl[:, :, dr, 1])
        acc, st, qk = _mlstm_scan(proj('m_q'), proj('m_k'), proj('m_v'), p['ml_conv'], log_i, log_f,
                                  init[dr], rev=bool(dr), acc=acc, qk=qk)
        states.append(st)
    return acc, tuple(states)


def _rwkv7_seq_branch(proj, h2, p, init, v_first, b, l, d):
    nh, n = d // RK_HD, RK_HD
    rd = d // 16
    mu = p['rk_mu']

    def lerp_shift(u, m):
        return u + m * (_bi_shift_seq(u) - u)

    r = lerp_shift(_take(proj('r_r'), d), mu[0:d])
    k = lerp_shift(_take(proj('r_k'), d), mu[d:2 * d])
    v = lerp_shift(_take(proj('r_v'), d), mu[2 * d:3 * d])
    pwa = _take(proj('r_wa'), 4 * rd)
    xwa = jnp.concatenate([lerp_shift(pwa[..., i * rd:(i + 1) * rd], mu[3 * d + i * rd:3 * d + (i + 1) * rd])
                           for i in range(4)], axis=-1)
    if p['rk_v0'] is None:
        v_first = v
    else:
        lora = _mm(_mm(h2, p['rk_v1']), p['rk_v2']).reshape(b, l, d)
        v = v + (v_first - v) * jax.nn.sigmoid(p['rk_v0'] + lora)
    kk = (k * p['rk_kk']).reshape(b, l, nh, n)
    kk = kk / jnp.maximum(jnp.sqrt(jnp.sum(kk * kk, axis=-1, keepdims=True)), 1e-12)
    kk = kk.reshape(b, l, d)
    acc, states, kd_sum = None, [], 0.0
    for dr in (0, 1):
        xw = xwa[..., dr * rd:(dr + 1) * rd]
        xa = xwa[..., (2 + dr) * rd:(3 + dr) * rd]
        log_w = -jax.nn.softplus(-(p['rk_w0'][dr] + _mm(jnp.tanh(xw).reshape(b * l, rd), p['rk_w2'][dr]).reshape(b, l, d))) - 0.5
        lw = -jnp.exp(log_w)
        a = jax.nn.sigmoid(p['rk_a0'][dr] + _mm(xa.reshape(b * l, rd), p['rk_a2'][dr]).reshape(b, l, d))
        kd = k * (1.0 + (a - 1.0) * p['rk_ka'])
        acc, s = _rwkv_scan_seq(r, lw, kd, v, kk, a, init[dr], rev=bool(dr), acc=acc)
        states.append(s)
        kd_sum = kd_sum + kd
    bonus = jnp.sum((r * kd_sum * p['rk_rk']).reshape(b, l, nh, n), axis=-1, keepdims=True) * v.reshape(b, l, nh, n)
    return acc, bonus.reshape(b, l, d), tuple(states), v_first


def _rwkv7_grid_branch(proj, h2, p, init, v_first, b, l, d):
    mu = p['rk_mu']
    zeros = jnp.zeros((d,), F32)
    has_vf = p['rk_v0'] is not None
    lora = _mm(_mm(h2, p['rk_v1']), p['rk_v2']).reshape(b, l, d) if has_vf else None
    acc, states, vf_out = None, [], v_first
    for dr in (0, 1):
        rows = {'mu_r': mu[0:d], 'mu_k': mu[d:2 * d], 'mu_v': mu[2 * d:3 * d], 'w0': p['rk_w0'][dr],
                'a0': p['rk_a0'][dr], 'kk': p['rk_kk'], 'ka': p['rk_ka'], 'rk': p['rk_rk'],
                'v0': p['rk_v0'] if has_vf else zeros}
        pvec = jnp.stack([rows[n] for n in _PV_ROWS], axis=0)
        emit_vf = (not has_vf) and dr == 0
        out = _rwkv_scan_grid(proj('r_r'), proj('r_k'), proj('r_v'), proj('r_wa'), pvec, mu[3 * d:][None],
                              _mx(p['rk_w2'][dr]), _mx(p['rk_a2'][dr]), init[dr], rev=bool(dr),
                              vf=v_first if has_vf else None, lora=lora, acc=acc, emit_vf=emit_vf)
        acc = (out[0], out[1])
        states.append(out[2])
        if emit_vf:
            vf_out = out[3]
    return acc[0], acc[1], tuple(states), vf_out


def _hgrn2_branch(proj, p, init):
    acc, states = None, []
    for dr in (0, 1):
        acc, s = _hgrn_scan(proj('h_q'), proj('h_f'), proj('h_i'), p['hg_f_b'][dr][None], p['hg_lb'][dr][None],
                            init[dr], rev=bool(dr), acc=acc)
        states.append(s)
    return acc, tuple(states)


def _mixer(h, p, init, on_grid, v_first, need_out, xs, gx, final_w, nxt):
    b, l, d = h.shape
    h2 = h.reshape(b * l, d)
    cols = {}
    for (w, first), (store_mx, _) in zip(p['w_in'], _COL_GROUPS):
        out = _mm(h2, w, MXU_DT if store_mx else F32).reshape(b, l, w.shape[1])
        cols.update({name: (out, off) for name, off in first.items()})

    def proj(name):
        return cols[name]

    hm, st_m = _mlstm_branch(proj, p, init[0], b, l)
    if on_grid:
        yr, bon, st_r, v_first = _rwkv7_grid_branch(proj, h2, p, init[1], v_first, b, l, d)
    else:
        yr, bon, st_r, v_first = _rwkv7_seq_branch(proj, h2, p, init[1], v_first, b, l, d)
    oh, st_h = _hgrn2_branch(proj, p, init[2])
    states = (st_m, st_r, st_h)
    if not need_out:
        return None, None, states, v_first
    out, h_next = _epilogue(hm, proj('m_o'), proj('m_z'), yr, bon, proj('r_z'), oh, proj('h_z'), proj('gate'),
                            xs, gx, p, final_w, nxt)
    return out, h_next, states, v_first


def kernel(x, c, ctx, c_ctx, norm_w, ada_w, ada_b, w_in, gate_b, ml_conv, ml_if_b, ml_norm_w, rk_mu, rk_w0, rk_w2, rk_a0, rk_a2, rk_kk, rk_ka, rk_rk, rk_v0, rk_v1, rk_v2, rk_ln_w, rk_ln_b, hg_f_b, hg_lb, hg_norm_w, w_pm, w_pr, w_ph, w_out, final_norm_w):
    batch, _, d = x.shape
    depth = w_in.shape[0]
    lb_p = jax.nn.softmax(hg_lb.astype(F32), axis=1)
    lower_bounds = jnp.cumsum(lb_p, axis=1) - lb_p[:, :1]
    xs, cs = x, ctx
    vf_x, vf_c = None, None
    cond = jnp.concatenate([jax.nn.silu(c), jax.nn.silu(c_ctx)[None]], axis=0)
    mods = []
    for l in range(depth):
        mod = _mm(cond, ada_w[l]) + ada_b[l]
        sh_x, sc_x, g_x = jnp.split(mod[:batch, None, :], 3, axis=-1)
        sh_c, sc_c, g_c = (jnp.broadcast_to(t[None, None, :], (batch, 1, d)) for t in jnp.split(mod[batch], 3, axis=-1))
        mods.append(((sh_x, 1.0 + sc_x, g_x), (sh_c, 1.0 + sc_c, g_c)))
    hx = _mx(_rms_norm(xs, norm_w[0]) * mods[0][0][1] + mods[0][0][0])
    hc = _mx(_rms_norm(cs, norm_w[0]) * mods[0][1][1] + mods[0][1][0])
    for l in range(depth):
        last = l == depth - 1
        p = {'w_in': _pack_w_in(w_in[l]), 'gate_b': gate_b[l], 'ml_conv': ml_conv[l], 'ml_if_b': ml_if_b[l],
             'ml_norm_w': ml_norm_w[l], 'rk_mu': rk_mu[l], 'rk_w0': rk_w0[l], 'rk_w2': rk_w2[l],
             'rk_a0': rk_a0[l], 'rk_a2': rk_a2[l], 'rk_kk': rk_kk[l], 'rk_ka': rk_ka[l], 'rk_rk': rk_rk[l],
             'rk_v0': rk_v0[l - 1] if l > 0 else None, 'rk_v1': rk_v1[l - 1] if l > 0 else None,
             'rk_v2': rk_v2[l - 1] if l > 0 else None, 'rk_ln_w': rk_ln_w[l], 'rk_ln_b': rk_ln_b[l],
             'hg_f_b': hg_f_b[l], 'hg_lb': lower_bounds[:, l], 'hg_norm_w': hg_norm_w[l],
             'w_pm': w_pm[l], 'w_pr': w_pr[l], 'w_ph': w_ph[l], 'w_out': w_out[l]}
        (_, _, gate_x), (_, _, gate_c) = mods[l]
        nxt_x = None if last else (norm_w[l + 1], mods[l + 1][0][1], mods[l + 1][0][0])
        nxt_c = None if last else (norm_w[l + 1], mods[l + 1][1][1], mods[l + 1][1][0])
        cs, hc, st_c, vf_c = _mixer(hc, p, _zero_states(batch, d), False, vf_c, not last, cs, gate_c, None, nxt_c)
        xs, hx, _, vf_x = _mixer(hx, p, st_c, True, vf_x, True, xs, gate_x, final_norm_w if last else None, nxt_x)
    return xs
```

```python
import functools

import jax
import jax.numpy as jnp
from jax import lax
from jax.experimental import pallas as pl
from jax.experimental.pallas import tpu as pltpu

F32 = jnp.float32
MXU_DT = jnp.bfloat16

NORM_EPS = 1e-6
GRID_W = 64
ML_HEADS = 4
ML_CHUNK = 256
RK_HD = 64
RK_CHUNK = 64
RK_GROUP = 256
RK_BATCH = 4
RK_GN_EPS = 64e-5
HG_HEADS = 8
HG_SUB = 16
HG_TILE = 128
EP_TILE = 256
SEG_W = 256
VMEM_LIMIT = 56 * 1024 * 1024


def _nt(a, b):
    return lax.dot_general(a, b, (((1,), (1,)), ((), ())), preferred_element_type=F32)


def _tn(a, b):
    return lax.dot_general(a, b, (((0,), (0,)), ((), ())), preferred_element_type=F32)


def _nn(a, b):
    return jnp.dot(a, b, preferred_element_type=F32)


def _mx(a):
    return a.astype(MXU_DT)


def _cumsum_mm(tri, x):
    hi = x.astype(MXU_DT)
    r1 = x - hi.astype(F32)
    mid = r1.astype(MXU_DT)
    lo = (r1 - mid.astype(F32)).astype(MXU_DT)
    one = lambda t: _nn(t, hi) + _nn(t, mid) + _nn(t, lo)
    return [one(t) for t in tri] if isinstance(tri, (list, tuple)) else one(tri)


def _seg_ones(seg):
    sh = seg.bit_length() - 1
    ri = lax.broadcasted_iota(jnp.int32, (SEG_W, SEG_W), 0)
    ci = lax.broadcasted_iota(jnp.int32, (SEG_W, SEG_W), 1)
    return ((ri >> sh) == (ci >> sh)).astype(F32).astype(MXU_DT)


def _segsum(x, ones):
    outs = [_nn(_mx(x[:, g * SEG_W:(g + 1) * SEG_W]), ones) for g in range(x.shape[1] // SEG_W)]
    return outs[0] if len(outs) == 1 else jnp.concatenate(outs, axis=1)


def _sigmoid(x):
    return jax.nn.sigmoid(x)


def _silu(x):
    return x * jax.nn.sigmoid(x)


def _params(sem):
    return pltpu.CompilerParams(dimension_semantics=sem, vmem_limit_bytes=VMEM_LIMIT)


def _mm_kernel(x_ref, w_ref, o_ref):
    o_ref[...] = _nn(_mx(x_ref[...]), w_ref[...]).astype(o_ref.dtype)


def _mm(x, w, out_dtype=F32):
    m, k = x.shape
    n = w.shape[1]
    mp = -(-m // 8) * 8
    npad = -(-n // 128) * 128
    if mp != m:
        x = jnp.pad(x, ((0, mp - m), (0, 0)))
    wb = w.astype(MXU_DT)
    if npad != n:
        wb = jnp.pad(wb, ((0, 0), (0, npad - n)))
    tm = next(t for t in (2048, 1024, 512, 256, 128, 64, 32, 16, 8) if mp % t == 0)
    tn = next(t for t in (1024, 512, 256, 128) if npad % t == 0)
    out = pl.pallas_call(
        _mm_kernel,
        grid=(mp // tm, npad // tn),
        in_specs=[pl.BlockSpec((tm, k), lambda i, j: (i, 0)),
                  pl.BlockSpec((k, tn), lambda i, j: (0, j))],
        out_specs=pl.BlockSpec((tm, tn), lambda i, j: (i, j)),
        out_shape=jax.ShapeDtypeStruct((mp, npad), out_dtype),
        compiler_params=_params(("parallel", "parallel")),
        name="mm",
    )(x, wb)
    if mp != m or npad != n:
        out = out[:m, :n]
    return out


def _mlstm_kernel(*refs, rev, heads, hd, nc, has_acc, conv):
    it = iter(refs)
    if conv:
        q_ref, qp_ref, qn_ref, k_ref, kp_ref, kn_ref, v_ref, cw_ref = (next(it) for _ in range(8))
    else:
        q_ref, k_ref, v_ref = (next(it) for _ in range(3))
    gcol_ref, grow_ref, c0_ref, n0_ref, m0_ref = (next(it) for _ in range(5))
    acc_ref = next(it) if has_acc else None
    h_ref, c_ref, n_ref, m_ref = (next(it) for _ in range(4))
    step = pl.program_id(1)

    @pl.when(step == 0)
    def _():
        c_ref[...] = c0_ref[...]
        n_ref[...] = n0_ref[...]
        m_ref[...] = m0_ref[...]

    t = q_ref.shape[1]
    ci = nc - 1 - step if rev else step
    has_prev = jnp.where(ci > 0, 1.0, 0.0)
    has_next = jnp.where(ci < nc - 1, 1.0, 0.0)
    trow = lax.broadcasted_iota(jnp.int32, (t, 1), 0)

    def conv_silu(u, prev_blk, next_blk, w3):
        up = jnp.where(trow == 0, prev_blk[7:8, :] * has_prev, pltpu.roll(u, 1, 0))
        dn = jnp.where(trow == t - 1, next_blk[0:1, :] * has_next, pltpu.roll(u, t - 1, 0))
        return _silu(up * w3[0:1, :] + u * w3[1:2, :] + dn * w3[2:3, :])

    if conv:
        q_all = conv_silu(q_ref[0], qp_ref[0], qn_ref[0], cw_ref[0:3, :]) * (hd ** -0.5)
        k_all = conv_silu(k_ref[0], kp_ref[0], kn_ref[0], cw_ref[3:6, :])
        qo_ref, ko_ref = next(it), next(it)
        qo_ref[0] = q_all
        ko_ref[0] = k_all
    else:
        q_all, k_all = q_ref[0], k_ref[0]

    row = lax.broadcasted_iota(jnp.int32, (t, t), 0)
    col = lax.broadcasted_iota(jnp.int32, (t, t), 1)
    mask = (col >= row) if rev else (col <= row)
    last = 0 if rev else t - 1
    hs = range(heads)
    lanes = [slice(h * hd, (h + 1) * hd) for h in hs]
    q32 = [q_all[:, sl] for sl in lanes]
    qc = [_mx(x) for x in q32]
    kc = [k_all[:, sl] for sl in lanes]
    vc = [_mx(v_ref[0, :, sl]) for sl in lanes]
    bc_col = [gcol_ref[0, :, h:h + 1] for h in hs]
    i_col = [gcol_ref[0, :, heads + h:heads + h + 1] for h in hs]
    m_prev = [m_ref[0, h, 0:1, 0:1] for h in hs]
    c_mem = [c_ref[0, h] for h in hs]
    n_mem = [n_ref[0, h] for h in hs]
    dmat = [jnp.where(mask, bc_col[h] - grow_ref[0, 0, h:h + 1, :] + grow_ref[0, 0, heads + h:heads + h + 1, :],
                      -jnp.inf) for h in hs]
    inter = [bc_col[h] + m_prev[h] for h in hs]
    m_t = [jnp.maximum(jnp.max(dmat[h], axis=-1, keepdims=True), inter[h]) for h in hs]
    qk = [_nt(qc[h], _mx(kc[h])) for h in hs]
    q_c = [_nn(qc[h], _mx(c_mem[h])) for h in hs]
    s = [qk[h] * jnp.exp(dmat[h] - m_t[h]) for h in hs]
    w_inter = [jnp.exp(inter[h] - m_t[h]) for h in hs]
    num = [_nn(_mx(s[h]), vc[h]) + w_inter[h] * q_c[h] for h in hs]
    for h in hs:
        qn = jnp.sum(q32[h] * n_mem[h], axis=-1, keepdims=True)
        den = jnp.sum(s[h], axis=-1, keepdims=True) + w_inter[h] * qn
        h_out = num[h] * (1.0 / jnp.maximum(jnp.abs(den), jnp.exp(-m_t[h])))
        if has_acc:
            h_out = h_out + acc_ref[0, :, lanes[h]]
        h_ref[0, :, lanes[h]] = h_out
    for h in hs:
        total = bc_col[h][last:last + 1, :]
        g_col = total - bc_col[h] + i_col[h]
        m_new = jnp.maximum(total + m_prev[h], jnp.max(g_col, axis=0, keepdims=True))
        wk = jnp.exp(g_col - m_new)
        dec = jnp.exp(total + m_prev[h] - m_new)
        kw = kc[h] * wk
        c_ref[0, h] = dec * c_mem[h] + _tn(_mx(kw), vc[h])
        n_ref[0, h] = dec * n_mem[h] + jnp.sum(kw, axis=0, keepdims=True)
        m_ref[0, h] = jnp.broadcast_to(m_new, m_ref.shape[2:])


def _mlstm_scan(pq, pk, pv, conv_w, log_i, log_f, state, rev, acc=None, qk=None):
    b, l, _ = pq[0].shape
    w = conv_w.shape[-1]
    heads = ML_HEADS
    hd = w // heads
    t = min(ML_CHUNK, l)
    assert l % t == 0
    nc = l // t
    nb8 = l // 8
    lf = log_f.reshape(b, heads, nc, t)
    bcum = jnp.flip(jnp.cumsum(jnp.flip(lf, 3), axis=3), 3) if rev else jnp.cumsum(lf, axis=3)
    rows = jnp.concatenate([bcum, log_i.reshape(b, heads, nc, t)], axis=1)
    grow = rows.transpose(0, 2, 1, 3)
    gcol = rows.reshape(b, 2 * heads, l).transpose(0, 2, 1)
    ch = (lambda c: nc - 1 - c) if rev else (lambda c: c)
    cidx = lambda bi, c: (bi, ch(c), 0)
    ridx = lambda bi, c: (bi, ch(c), 0, 0)
    sidx = lambda bi, c: (bi, 0, 0, 0)
    c0, n0, m0 = state
    seq = pl.BlockSpec((1, t, w), cidx)
    col = lambda x: pl.BlockSpec((1, t, w), lambda bi, c: (bi, ch(c), x[1] // w))
    prv = lambda x: pl.BlockSpec((1, 8, w), lambda bi, c: (bi, jnp.maximum(ch(c) * (t // 8) - 1, 0), x[1] // w))
    nxt = lambda x: pl.BlockSpec((1, 8, w), lambda bi, c: (bi, jnp.minimum((ch(c) + 1) * (t // 8), nb8 - 1), x[1] // w))
    st_specs = [pl.BlockSpec((1, heads, hd, hd), sidx), pl.BlockSpec((1, heads, 1, hd), sidx),
                pl.BlockSpec((1, heads, 8, 128), sidx)]
    conv = qk is None
    if conv:
        in_specs = [col(pq), prv(pq), nxt(pq), col(pk), prv(pk), nxt(pk), col(pv),
                    pl.BlockSpec((6, w), lambda bi, c: (0, 0))]
        args = [pq[0], pq[0], pq[0], pk[0], pk[0], pk[0], pv[0], conv_w.reshape(6, w)]
    else:
        in_specs = [seq, seq, col(pv)]
        args = [qk[0], qk[1], pv[0]]
    in_specs += [pl.BlockSpec((1, t, 2 * heads), cidx), pl.BlockSpec((1, 1, 2 * heads, t), ridx)] + st_specs
    args += [gcol, grow, c0, n0, m0]
    if acc is not None:
        in_specs.append(seq)
        args.append(acc)
    seq_out = jax.ShapeDtypeStruct((b, l, w), F32)
    out = pl.pallas_call(
        functools.partial(_mlstm_kernel, rev=rev, heads=heads, hd=hd, nc=nc, has_acc=acc is not None, conv=conv),
        grid=(b, nc),
        in_specs=in_specs,
        out_specs=[seq] + st_specs + ([seq, seq] if conv else []),
        out_shape=[seq_out, jax.ShapeDtypeStruct(c0.shape, F32), jax.ShapeDtypeStruct(n0.shape, F32),
                   jax.ShapeDtypeStruct(m0.shape, F32)] + ([seq_out, seq_out] if conv else []),
        compiler_params=_params(("parallel", "arbitrary")),
        name="mlstm_rev" if rev else "mlstm_fwd",
    )(*args)
    return out[0], (out[1], out[2], out[3]), ((out[4], out[5]) if conv else qk)


def _rwkv_core(rows, s_ref, rev):
    c, w = rows[0][0].shape
    g_w = RK_GROUP
    nh = g_w // RK_HD
    assert c == RK_HD and c & (c - 1) == 0
    sh = c.bit_length() - 1
    rj = lax.broadcasted_iota(jnp.int32, (g_w, g_w), 0)
    cj = lax.broadcasted_iota(jnp.int32, (g_w, g_w), 1)
    blk = (rj >> sh) == (cj >> sh)
    tt, ss = rj & (c - 1), cj & (c - 1)
    incl = blk & ((ss >= tt) if rev else (ss <= tt))
    strict = blk & ((ss > tt) if rev else (ss < tt))
    eye = (rj == cj).astype(F32)
    tri = incl[:c, :c].astype(F32).astype(MXU_DT)
    lo_half = lax.broadcasted_iota(jnp.int32, (2 * g_w, 2 * c), 1) < c
    last = 0 if rev else c - 1
    zero = jnp.zeros((), F32)
    chains = [(bi, g) for bi in range(len(rows)) for g in range(w // g_w)]

    def bd(x):
        return _mx(jnp.where(blk, jnp.concatenate([x] * nh, axis=0), zero))

    def rsum(z):
        out = z[0:c]
        for i in range(1, nh):
            out = out + z[i * c:(i + 1) * c]
        return out

    a_bd, r_bd, v_bd, bk_t, r_t, v_g, bk_hat, dec = {}, {}, {}, {}, {}, {}, {}, {}
    for bi, (r, lw, kd, v, kk, a) in enumerate(rows):
        ka = kk * a
        gi = _cumsum_mm(tri, lw)
        tot = gi[last:last + 1, :]
        e_neg = jnp.exp(-gi)
        e_tail = jnp.exp(tot - gi)
        full = {'a': -kk * jnp.exp(gi - lw), 'b': ka * e_neg, 'k': kd * e_neg, 'r': r * jnp.exp(gi),
                'bh': ka * e_tail, 'kh': kd * e_tail, 'dec': jnp.exp(tot), 'v': v}
        for g in range(w // g_w):
            x = {n: t[:, g * g_w:(g + 1) * g_w] for n, t in full.items()}
            ch = (bi, g)
            a_bd[ch], r_bd[ch], v_bd[ch] = bd(x['a']), bd(x['r']), bd(x['v'])
            bk_t[ch] = _mx(jnp.concatenate([x['b'], x['k']], axis=0))
            bk_hat[ch] = _mx(jnp.concatenate([x['bh'], x['kh']], axis=0))
            r_t[ch], v_g[ch], dec[ch] = _mx(x['r']), x['v'], x['dec']

    l_ab, a_ak, a_rb, a_rk = {}, {}, {}, {}
    for ch in chains:
        prod = _nt(jnp.concatenate([a_bd[ch], r_bd[ch]], axis=0), bk_t[ch])
        swp = pltpu.roll(prod, c, 1)
        xb = jnp.where(lo_half, prod, swp)
        xk = jnp.where(lo_half, swp, prod)
        xb = jnp.concatenate([xb, xb], axis=1)
        xk = jnp.concatenate([xk, xk], axis=1)
        l_ab[ch] = jnp.where(strict, xb[:g_w], zero)
        a_ak[ch] = _mx(jnp.where(strict, xk[:g_w], zero))
        a_rb[ch] = _mx(rsum(jnp.where(incl, xb[g_w:], zero)))
        a_rk[ch] = _mx(rsum(jnp.where(incl, xk[g_w:], zero)))

    xs = {ch: _mx(l_ab[ch]) for ch in chains}
    ps = {ch: eye + l_ab[ch] for ch in chains}
    for _ in range(max(1, (c - 1).bit_length() - 1)):
        xs = {ch: _mx(_nn(xs[ch], xs[ch])) for ch in chains}
        ps = {ch: ps[ch] + _nn(_mx(ps[ch]), xs[ch]) for ch in chains}

    t_cat = {ch: _mx(rsum(ps[ch])) for ch in chains}
    w_cat = {ch: _mx(_nn(t_cat[ch], a_bd[ch])) for ch in chains}
    ta = {ch: _mx(_nn(t_cat[ch], a_ak[ch])) for ch in chains}
    u0 = {ch: _nn(ta[ch], v_bd[ch]) for ch in chains}
    y_k = {ch: _nn(a_rk[ch], v_bd[ch]) for ch in chains}
    s_mem = {ch: s_ref[ch[0], ch[1]] for ch in chains}
    s_b = {ch: _mx(s_mem[ch]) for ch in chains}
    u = {ch: _nt(w_cat[ch], s_b[ch]) + u0[ch] for ch in chains}
    y = {ch: _nt(r_t[ch], s_b[ch]) + _nn(a_rb[ch], bd(u[ch])) + y_k[ch] for ch in chains}
    for ch in chains:
        upd = _tn(_mx(jnp.concatenate([u[ch], v_g[ch]], axis=0)), bk_hat[ch])
        s_ref[ch[0], ch[1]] = s_mem[ch] * dec[ch] + jnp.where(blk, upd, zero)
    return [jnp.concatenate([y[(bi, g)] for g in range(w // g_w)], axis=1) for bi in range(len(rows))]


def _rwkv_seq_kernel(*refs, rev, has_acc):
    r_ref, lw_ref, kd_ref, v_ref, kk_ref, a_ref, s0_ref = refs[:7]
    acc_ref = refs[7] if has_acc else None
    y_ref, s_ref = refs[7 + has_acc:]

    @pl.when(pl.program_id(1) == 0)
    def _():
        s_ref[...] = s0_ref[...]

    y, = _rwkv_core([(r_ref[0], lw_ref[0], kd_ref[0], v_ref[0], kk_ref[0], a_ref[0])], s_ref, rev)
    if has_acc:
        y = y + acc_ref[0]
    y_ref[0] = y


def _rwkv_scan_seq(r, lw, kd, v, kk, a, state, rev, acc=None):
    b, l, w = r.shape
    c = RK_CHUNK
    nc = l // c
    cidx = (lambda bi, ch: (bi, nc - 1 - ch, 0)) if rev else (lambda bi, ch: (bi, ch, 0))
    seq = pl.BlockSpec((1, c, w), cidx)
    st = pl.BlockSpec((1,) + state.shape[1:], lambda bi, ch: (bi, 0, 0, 0))
    args = [r, lw, kd, v, kk, a, state] + ([acc] if acc is not None else [])
    y, s = pl.pallas_call(
        functools.partial(_rwkv_seq_kernel, rev=rev, has_acc=acc is not None),
        grid=(b, nc),
        in_specs=[seq] * 6 + [st] + ([seq] if acc is not None else []),
        out_specs=[seq, st],
        out_shape=[jax.ShapeDtypeStruct((b, l, w), F32), jax.ShapeDtypeStruct(state.shape, F32)],
        compiler_params=_params(("parallel", "arbitrary")),
        name="rwkv_seq_rev" if rev else "rwkv_seq_fwd",
    )(*args)
    return y, s


_PV_ROWS = ('mu_r', 'mu_k', 'mu_v', 'w0', 'a0', 'kk', 'ka', 'rk', 'v0')


def _rwkv_grid_kernel(*refs, rev, nc, has_vf, has_acc, emit_vf):
    it = iter(refs)
    pr, pr_u, pr_d, pk, pk_u, pk_d, pv, pv_u, pv_d, wa, wa_u, wa_d = (next(it) for _ in range(12))
    vf_ref, lora_ref = (next(it), next(it)) if has_vf else (None, None)
    pvec, mu_wa, w2_ref, a2_ref, s0_ref = (next(it) for _ in range(5))
    accy_ref, accb_ref = (next(it), next(it)) if has_acc else (None, None)
    y_ref, bon_ref, s_ref = (next(it) for _ in range(3))
    vfo_ref = next(it) if emit_vf else None
    step = pl.program_id(1)

    @pl.when(step == 0)
    def _():
        s_ref[...] = s0_ref[...]

    c = pr.shape[1]
    w = pr.shape[2]
    rd = w // 16
    ci = nc - 1 - step if rev else step
    has_up = jnp.where(ci > 0, 1.0, 0.0)
    has_dn = jnp.where(ci < nc - 1, 1.0, 0.0)
    trow = lax.broadcasted_iota(jnp.int32, (c, 1), 0)
    prow = lambda name: pvec[_PV_ROWS.index(name):_PV_ROWS.index(name) + 1, :]

    def left(x):
        return jnp.where(trow == 0, 0.0, pltpu.roll(x, 1, 0))

    def right(x):
        return jnp.where(trow == c - 1, 0.0, pltpu.roll(x, c - 1, 0))

    ones = _seg_ones(RK_HD)
    d = 1 if rev else 0
    rows, bons = [], []
    for bi in range(pr.shape[0]):
        def lerp_wide(cur_ref, up_ref, dn_ref, mu):
            cur = cur_ref[bi]
            q = w // 4
            sh = jnp.concatenate([left(cur[:, :q]), right(cur[:, q:2 * q]),
                                  up_ref[bi] * has_up, dn_ref[bi] * has_dn], axis=1)
            return cur + mu * (sh - cur)

        r = lerp_wide(pr, pr_u, pr_d, prow('mu_r'))
        k = lerp_wide(pk, pk_u, pk_d, prow('mu_k'))
        v = lerp_wide(pv, pv_u, pv_d, prow('mu_v'))

        cur = wa[bi]
        qd = (lax.broadcasted_iota(jnp.int32, cur.shape, 1) & (rd - 1)) >> ((rd // 4).bit_length() - 1)
        sh = jnp.where(qd == 0, left(cur), jnp.where(qd == 1, right(cur),
                                                     jnp.where(qd == 2, wa_u[bi] * has_up, wa_d[bi] * has_dn)))
        xwa = cur + mu_wa[...] * (sh - cur)
        xw = xwa[:, d * rd:(d + 1) * rd]
        xa = xwa[:, (2 + d) * rd:(3 + d) * rd]
        z = -(prow('w0') + _nn(_mx(jnp.tanh(xw)), w2_ref[...]))
        softplus = jnp.maximum(z, 0.0) + jnp.log(1.0 + jnp.exp(-jnp.abs(z)))
        lw = -jnp.exp(-softplus - 0.5)
        a = _sigmoid(prow('a0') + _nn(_mx(xa), a2_ref[...]))
        if has_vf:
            v = v + (vf_ref[bi] - v) * _sigmoid(prow('v0') + lora_ref[bi])
        if emit_vf:
            vfo_ref[bi] = v
        kkn = k * prow('kk')
        kk = kkn * lax.rsqrt(jnp.maximum(_segsum(kkn * kkn, ones), 1e-24))
        kd = k * (1.0 + (a - 1.0) * prow('ka'))
        bons.append(_segsum(r * kd * prow('rk'), ones) * v)
        rows.append((r, lw, kd, v, kk, a))
    ys = _rwkv_core(rows, s_ref, rev)
    for bi, (y, bon) in enumerate(zip(ys, bons)):
        if has_acc:
            y = y + accy_ref[bi]
            bon = bon + accb_ref[bi]
        y_ref[bi] = y
        bon_ref[bi] = bon


def _rwkv_scan_grid(pr, pk, pv, wa, pvec, mu_wa, w2, a2, state, rev, vf=None, lora=None,
                    acc=None, emit_vf=False):
    b, l, _ = pr[0].shape
    w = pvec.shape[1]
    c = RK_CHUNK
    assert c == GRID_W
    nc = l // c
    wq = w // 4
    ch = (lambda s: nc - 1 - s) if rev else (lambda s: s)
    cur = lambda lane: (lambda bi, s: (bi, ch(s), lane))
    upi = lambda lane: (lambda bi, s: (bi, jnp.maximum(ch(s) - 1, 0), lane))
    dni = lambda lane: (lambda bi, s: (bi, jnp.minimum(ch(s) + 1, nc - 1), lane))
    bb = RK_BATCH if b % RK_BATCH == 0 else 1
    seq = pl.BlockSpec((bb, c, w), cur(0))
    wide = lambda x: [pl.BlockSpec((bb, c, w), cur(x[1] // w)), pl.BlockSpec((bb, c, wq), upi(x[1] // wq + 2)),
                      pl.BlockSpec((bb, c, wq), dni(x[1] // wq + 3))]
    wa_w = mu_wa.shape[1]
    small = [pl.BlockSpec((bb, c, wa_w), cur(wa[1] // wa_w)), pl.BlockSpec((bb, c, wa_w), upi(wa[1] // wa_w)),
             pl.BlockSpec((bb, c, wa_w), dni(wa[1] // wa_w))]
    full = lambda x: pl.BlockSpec(x.shape, lambda bi, s: (0,) * x.ndim)
    st = pl.BlockSpec((bb,) + state.shape[1:], lambda bi, s: (bi, 0, 0, 0))
    in_specs = wide(pr) + wide(pk) + wide(pv) + small
    args = [pr[0]] * 3 + [pk[0]] * 3 + [pv[0]] * 3 + [wa[0]] * 3
    if vf is not None:
        in_specs += [seq, seq]
        args += [vf, lora]
    in_specs += [full(pvec), full(mu_wa), full(w2), full(a2), st]
    args += [pvec, mu_wa, w2, a2, state]
    if acc is not None:
        in_specs += [seq, seq]
        args += list(acc)
    out_specs = [seq, seq, st]
    out_shape = [jax.ShapeDtypeStruct((b, l, w), F32), jax.ShapeDtypeStruct((b, l, w), F32),
                 jax.ShapeDtypeStruct(state.shape, F32)]
    if emit_vf:
        out_specs.append(seq)
        out_shape.append(jax.ShapeDtypeStruct((b, l, w), F32))
    return pl.pallas_call(
        functools.partial(_rwkv_grid_kernel, rev=rev, nc=nc, has_vf=vf is not None,
                          has_acc=acc is not None, emit_vf=emit_vf),
        grid=(b // bb, nc),
        in_specs=in_specs,
        out_specs=out_specs,
        out_shape=out_shape,
        compiler_params=_params(("parallel", "arbitrary")),
        name="rwkv_grid_rev" if rev else "rwkv_grid_fwd",
    )(*args)


def _hgrn_kernel(*refs, rev, heads, hd, has_acc):
    q_ref, f_ref, v_ref, fb_ref, lb_ref, s0_ref = refs[:6]
    acc_ref = refs[6] if has_acc else None
    o_ref, s_ref = refs[6 + has_acc:]

    @pl.when(pl.program_id(1) == 0)
    def _():
        s_ref[...] = s0_ref[...]

    tt = q_ref.shape[1]
    sub = HG_SUB
    sh = sub.bit_length() - 1
    nsub = tt // sub
    mid = sub // 2 if rev else sub // 2 - 1
    last = 0 if rev else sub - 1
    ri = lax.broadcasted_iota(jnp.int32, (tt, tt), 0)
    ci = lax.broadcasted_iota(jnp.int32, (tt, tt), 1)
    causal = ((ri >> sh) == (ci >> sh)) & ((ci >= ri) if rev else (ci <= ri))
    lb = lb_ref[...]
    f = lb + (1.0 - lb) * _sigmoid(f_ref[0] + fb_ref[...])
    g_all = _cumsum_mm(causal.astype(F32).astype(MXU_DT), jnp.log(f))

    def row_of_each_sub(r):
        return jnp.concatenate([jnp.broadcast_to(g_all[j * sub + r:j * sub + r + 1], (sub, g_all.shape[1]))
                                for j in range(nsub)], axis=0)

    g_mid, g_last = row_of_each_sub(mid), row_of_each_sub(last)
    k_all = 1.0 - f
    q_all = _silu(q_ref[0])
    q1 = _mx(q_all * jnp.exp(g_all - g_mid))
    k1 = _mx(k_all * jnp.exp(g_mid - g_all))
    qg_f = q_all * jnp.exp(g_all)
    kl_f = k_all * jnp.exp(g_last - g_all)
    v_all = _mx(v_ref[0])
    assert nsub % 2 == 0
    is_first = lambda j: (j % 2 == 1) == rev
    dec = [jnp.exp(g_all[j * sub + last:j * sub + last + 1]) for j in range(nsub)]
    one_row = jnp.ones_like(dec[0])
    per_sub = lambda vals: jnp.concatenate([jnp.broadcast_to(x, (sub, x.shape[1])) for x in vals], axis=0)
    qg, kl = _mx(qg_f), _mx(kl_f)
    qg2 = _mx(qg_f * per_sub([one_row if is_first(j) else dec[j ^ 1] for j in range(nsub)]))
    kl2 = _mx(kl_f * per_sub([dec[j ^ 1] if is_first(j) else one_row for j in range(nsub)]))
    sub_r, sub_c = ri >> sh, ci >> sh
    second_sees_first = ((sub_r ^ 1) == sub_c) & ((sub_r & 1) == (0 if rev else 1))
    lanes = [slice(h * hd, (h + 1) * hd) for h in range(heads)]
    zero = jnp.zeros((), F32)
    att = [_mx(jnp.where(causal, _nt(q1[:, sl], k1[:, sl]), zero)
               + jnp.where(second_sees_first, _nt(qg[:, sl], kl[:, sl]), zero)) for sl in lanes]
    intra = [_nn(att[h], v_all[:, lanes[h]]) for h in range(heads)]
    pairs = [nsub // 2 - 1 - s if rev else s for s in range(nsub // 2)]
    rows = [slice(2 * p * sub, 2 * (p + 1) * sub) for p in range(nsub // 2)]
    upd = {(p, h): _tn(v_all[rows[p], lanes[h]], kl2[rows[p], lanes[h]]) for p in pairs for h in range(heads)}
    s_mem = [s_ref[0, h] for h in range(heads)]
    for p in pairs:
        dec_pair = dec[2 * p] * dec[2 * p + 1]
        outs = []
        for h in range(heads):
            outs.append(intra[h][rows[p]] + _nt(qg2[rows[p], lanes[h]], _mx(s_mem[h])))
            s_mem[h] = s_mem[h] * dec_pair[:, lanes[h]] + upd[(p, h)]
        o = jnp.concatenate(outs, axis=1)
        if has_acc:
            o = o + acc_ref[0, rows[p], :]
        o_ref[0, rows[p], :] = o
    for h in range(heads):
        s_ref[0, h] = s_mem[h]


def _hgrn_scan(pq, pf, pi, f_b, lb, state, rev, acc=None):
    b, l, _ = pq[0].shape
    w = f_b.shape[1]
    heads = HG_HEADS
    hd = w // heads
    tt = min(HG_TILE, l)
    nc = l // tt
    d = 1 if rev else 0
    ch = (lambda s: nc - 1 - s) if rev else (lambda s: s)
    col = lambda x, extra=0: pl.BlockSpec((1, tt, w), lambda bi, s: (bi, ch(s), x[1] // w + extra))
    seq = pl.BlockSpec((1, tt, w), lambda bi, s: (bi, ch(s), 0))
    vec = pl.BlockSpec((1, w), lambda bi, s: (0, 0))
    st = pl.BlockSpec((1,) + state.shape[1:], lambda bi, s: (bi, 0, 0, 0))
    args = [pq[0], pf[0], pi[0], f_b, lb, state] + ([acc] if acc is not None else [])
    o, s = pl.pallas_call(
        functools.partial(_hgrn_kernel, rev=rev, heads=heads, hd=hd, has_acc=acc is not None),
        grid=(b, nc),
        in_specs=[col(pq), col(pf, d), col(pi), vec, vec, st] + ([seq] if acc is not None else []),
        out_specs=[seq, st],
        out_shape=[jax.ShapeDtypeStruct((b, l, w), F32), jax.ShapeDtypeStruct(state.shape, F32)],
        compiler_params=_params(("parallel", "arbitrary")),
        name="hgrn_rev" if rev else "hgrn_fwd",
    )(*args)
    return o, s


_EV_ROWS = ('ml_norm_w', 'rk_ln_w', 'rk_ln_b', 'hg_norm_w', 'gate_b0', 'gate_b1', 'gate_b2', 'final_w', 'next_w')


def _epilogue_kernel(hm_ref, mo_ref, mz_ref, yr_ref, bon_ref, rz_ref, oh_ref, hz_ref,
                     g0_ref, g1_ref, g2_ref, xs_ref, gx_ref, ev_ref,
                     wpm_ref, wpr_ref, wph_ref, wout_ref, *rest, final, has_next):
    o_ref = rest[2 * has_next]
    ev = lambda name: ev_ref[_EV_ROWS.index(name):_EV_ROWS.index(name) + 1, :]
    d = hm_ref.shape[2]

    def seg_mean(x, seg):
        if seg % 128:
            return _segsum(x, _seg_ones(seg)) * (1.0 / seg)
        parts = [jnp.broadcast_to(jnp.mean(x[:, i:i + seg], axis=-1, keepdims=True), (x.shape[0], seg))
                 for i in range(0, d, seg)]
        return jnp.concatenate(parts, axis=1)

    seg = d // ML_HEADS
    y = hm_ref[0]
    y = y - seg_mean(y, seg)
    y = y * lax.rsqrt(seg_mean(y * y, seg) + NORM_EPS) * ev('ml_norm_w')
    u_m = _sigmoid(mo_ref[0].astype(F32)) * y * _silu(mz_ref[0].astype(F32))
    y = yr_ref[0]
    y = y - seg_mean(y, RK_HD)
    y = y * lax.rsqrt(seg_mean(y * y, RK_HD) + RK_GN_EPS) * ev('rk_ln_w') + ev('rk_ln_b') + bon_ref[0]
    u_r = y * _silu(rz_ref[0].astype(F32))
    seg = d // HG_HEADS
    o = oh_ref[0]
    o = o * lax.rsqrt(seg_mean(o * o, seg) + NORM_EPS) * ev('hg_norm_w')
    u_h = o * _silu(hz_ref[0].astype(F32))

    merged = (_sigmoid(g0_ref[0].astype(F32) + ev('gate_b0')) * _nn(_mx(u_m), wpm_ref[...])
              + _sigmoid(g1_ref[0].astype(F32) + ev('gate_b1')) * _nn(_mx(u_r), wpr_ref[...])
              + _sigmoid(g2_ref[0].astype(F32) + ev('gate_b2')) * _nn(_mx(u_h), wph_ref[...]))
    xs = xs_ref[0] + gx_ref[0] * _nn(_mx(merged), wout_ref[...])
    if final:
        xs = xs * lax.rsqrt(jnp.mean(xs * xs, axis=-1, keepdims=True) + NORM_EPS) * ev('final_w')
    o_ref[0] = xs
    if has_next:
        normed = xs * lax.rsqrt(jnp.mean(xs * xs, axis=-1, keepdims=True) + NORM_EPS) * ev('next_w')
        rest[3][0] = _mx(normed * rest[0][0] + rest[1][0])


def _epilogue(hm, mo, mz, yr, bon, rz, oh, hz, gate, xs, gx, p, final_w, nxt):
    b, l, d = xs.shape
    tt = min(EP_TILE, l)
    seq = pl.BlockSpec((1, tt, d), lambda bi, i: (bi, i, 0))
    col = lambda x, extra=0: pl.BlockSpec((1, tt, d), lambda bi, i: (bi, i, x[1] // d + extra))
    ev = jnp.stack([p['ml_norm_w'], p['rk_ln_w'], p['rk_ln_b'], p['hg_norm_w'],
                    p['gate_b'][0], p['gate_b'][1], p['gate_b'][2],
                    final_w if final_w is not None else jnp.ones((d,), F32),
                    nxt[0] if nxt is not None else jnp.ones((d,), F32)], axis=0)
    wspec = pl.BlockSpec((d, d), lambda bi, i: (0, 0))
    ws = [_mx(p[n]) for n in ('w_pm', 'w_pr', 'w_ph', 'w_out')]
    per_batch = pl.BlockSpec((1, 1, d), lambda bi, i: (bi, 0, 0))
    has_next = nxt is not None
    out = pl.pallas_call(
        functools.partial(_epilogue_kernel, final=final_w is not None, has_next=has_next),
        grid=(b, l // tt),
        in_specs=[seq, col(mo), col(mz), seq, seq, col(rz), seq, col(hz), col(gate), col(gate, 1), col(gate, 2), seq,
                  per_batch, pl.BlockSpec(ev.shape, lambda bi, i: (0, 0))] + [wspec] * 4 + [per_batch] * (2 * has_next),
        out_specs=[seq] * (1 + has_next),
        out_shape=[jax.ShapeDtypeStruct((b, l, d), F32)] + [jax.ShapeDtypeStruct((b, l, d), MXU_DT)] * has_next,
        compiler_params=_params(("parallel", "parallel")),
        name="epilogue",
    )(hm, mo[0], mz[0], yr, bon, rz[0], oh, hz[0], gate[0], gate[0], gate[0], xs, gx, ev, *ws,
      *(nxt[1:] if has_next else ()))
    return out[0], (out[1] if has_next else None)


def _rms_norm(x, w):
    return x * lax.rsqrt(jnp.mean(x * x, axis=-1, keepdims=True) + NORM_EPS) * w


def _bi_shift_seq(u):
    half = u.shape[-1] // 2
    g = jnp.pad(u, ((0, 0), (1, 1), (0, 0)))
    return jnp.concatenate((g[:, :-2, :half], g[:, 2:, half:]), axis=-1)


def _in_layout(d):
    rd = d // 16
    return (('m_q', d), ('m_k', d), ('m_v', d), ('m_o', d), ('m_z', d), ('m_if', 4 * ML_HEADS),
            ('r_r', d), ('r_k', d), ('r_v', d), ('r_z', d), ('r_wa', 4 * rd),
            ('h_q', d), ('h_f', 2 * d), ('h_i', d), ('h_z', d), ('gate', 3 * d))


_COL_GROUPS = ((False, ('m_q', 'm_k', 'r_r', 'r_k', 'r_v', 'h_q', 'h_f')),
               (False, ('r_wa', 'm_if')),
               (True, ('m_v', 'h_i', 'm_o', 'm_z', 'r_z', 'h_z', 'gate')))
_COL_ALIGN = 512


def _pack_w_in(w_in):
    d = w_in.shape[0]
    src, off = {}, 0
    for name, width in _in_layout(d):
        src[name] = (off, width)
        off += width

    def pack(names):
        cols, first, o = [], {}, 0
        for n in names:
            cols.append(w_in[:, src[n][0]:src[n][0] + src[n][1]])
            first[n] = o
            o += src[n][1]
        if o % _COL_ALIGN:
            cols.append(jnp.zeros((d, -o % _COL_ALIGN), w_in.dtype))
        return jnp.concatenate(cols, axis=1).astype(MXU_DT), first

    return [pack(names) for _, names in _COL_GROUPS]


def _take(col, width):
    return col[0][..., col[1]:col[1] + width]


def _zero_states(b, d):
    ml_hd = d // ML_HEADS
    ml = (jnp.zeros((b, ML_HEADS, ml_hd, ml_hd), F32), jnp.zeros((b, ML_HEADS, 1, ml_hd), F32),
          jnp.zeros((b, ML_HEADS, 8, 128), F32))
    rk = jnp.zeros((b, d // RK_GROUP, RK_GROUP, RK_GROUP), F32)
    hg_hd = d // HG_HEADS
    hg = jnp.zeros((b, HG_HEADS, hg_hd, hg_hd), F32)
    return ((ml, ml), (rk, rk), (hg, hg))


def _mlstm_branch(proj, p, init, b, l):
    gl = _take(proj('m_if'), 4 * ML_HEADS) + p['ml_if_b'].reshape(-1)
    gl = gl.transpose(0, 2, 1).reshape(b, 2, 2, ML_HEADS, l)
    acc, qk, states = None, None, []
    for dr in (0, 1):
        log_i = gl[:, dr, 0]
        log_f = jax.nn.log_sigmoid(gl[:, dr, 1])
        acc, st, qk = _mlstm_scan(proj('m_q'), proj('m_k'), proj('m_v'), p['ml_conv'], log_i, log_f,
                                  init[dr], rev=bool(dr), acc=acc, qk=qk)
        states.append(st)
    return acc, tuple(states)


def _rwkv7_seq_branch(proj, h2, p, init, v_first, b, l, d):
    nh, n = d // RK_HD, RK_HD
    rd = d // 16
    mu = p['rk_mu']

    def lerp_shift(u, m):
        return u + m * (_bi_shift_seq(u) - u)

    r = lerp_shift(_take(proj('r_r'), d), mu[0:d])
    k = lerp_shift(_take(proj('r_k'), d), mu[d:2 * d])
    v = lerp_shift(_take(proj('r_v'), d), mu[2 * d:3 * d])
    pwa = _take(proj('r_wa'), 4 * rd)
    xwa = jnp.concatenate([lerp_shift(pwa[..., i * rd:(i + 1) * rd], mu[3 * d + i * rd:3 * d + (i + 1) * rd])
                           for i in range(4)], axis=-1)
    if p['rk_v0'] is None:
        v_first = v
    else:
        lora = _mm(_mm(h2, p['rk_v1']), p['rk_v2']).reshape(b, l, d)
        v = v + (v_first - v) * jax.nn.sigmoid(p['rk_v0'] + lora)
    kk = (k * p['rk_kk']).reshape(b, l, nh, n)
    kk = kk / jnp.maximum(jnp.sqrt(jnp.sum(kk * kk, axis=-1, keepdims=True)), 1e-12)
    kk = kk.reshape(b, l, d)
    acc, states, kd_sum = None, [], 0.0
    for dr in (0, 1):
        xw = xwa[..., dr * rd:(dr + 1) * rd]
        xa = xwa[..., (2 + dr) * rd:(3 + dr) * rd]
        log_w = -jax.nn.softplus(-(p['rk_w0'][dr] + _mm(jnp.tanh(xw).reshape(b * l, rd), p['rk_w2'][dr]).reshape(b, l, d))) - 0.5
        lw = -jnp.exp(log_w)
        a = jax.nn.sigmoid(p['rk_a0'][dr] + _mm(xa.reshape(b * l, rd), p['rk_a2'][dr]).reshape(b, l, d))
        kd = k * (1.0 + (a - 1.0) * p['rk_ka'])
        acc, s = _rwkv_scan_seq(r, lw, kd, v, kk, a, init[dr], rev=bool(dr), acc=acc)
        states.append(s)
        kd_sum = kd_sum + kd
    bonus = jnp.sum((r * kd_sum * p['rk_rk']).reshape(b, l, nh, n), axis=-1, keepdims=True) * v.reshape(b, l, nh, n)
    return acc, bonus.reshape(b, l, d), tuple(states), v_first


def _rwkv7_grid_branch(proj, h2, p, init, v_first, b, l, d):
    mu = p['rk_mu']
    zeros = jnp.zeros((d,), F32)
    has_vf = p['rk_v0'] is not None
    lora = _mm(_mm(h2, p['rk_v1']), p['rk_v2']).reshape(b, l, d) if has_vf else None
    acc, states, vf_out = None, [], v_first
    for dr in (0, 1):
        rows = {'mu_r': mu[0:d], 'mu_k': mu[d:2 * d], 'mu_v': mu[2 * d:3 * d], 'w0': p['rk_w0'][dr],
                'a0': p['rk_a0'][dr], 'kk': p['rk_kk'], 'ka': p['rk_ka'], 'rk': p['rk_rk'],
                'v0': p['rk_v0'] if has_vf else zeros}
        pvec = jnp.stack([rows[n] for n in _PV_ROWS], axis=0)
        emit_vf = (not has_vf) and dr == 0
        out = _rwkv_scan_grid(proj('r_r'), proj('r_k'), proj('r_v'), proj('r_wa'), pvec, mu[3 * d:][None],
                              _mx(p['rk_w2'][dr]), _mx(p['rk_a2'][dr]), init[dr], rev=bool(dr),
                              vf=v_first if has_vf else None, lora=lora, acc=acc, emit_vf=emit_vf)
        acc = (out[0], out[1])
        states.append(out[2])
        if emit_vf:
            vf_out = out[3]
    return acc[0], acc[1], tuple(states), vf_out


def _hgrn2_branch(proj, p, init):
    acc, states = None, []
    for dr in (0, 1):
        acc, s = _hgrn_scan(proj('h_q'), proj('h_f'), proj('h_i'), p['hg_f_b'][dr][None], p['hg_lb'][dr][None],
                            init[dr], rev=bool(dr), acc=acc)
        states.append(s)
    return acc, tuple(states)


def _mixer(h, p, init, on_grid, v_first, need_out, xs, gx, final_w, nxt):
    b, l, d = h.shape
    h2 = h.reshape(b * l, d)
    cols = {}
    for (w, first), (store_mx, _) in zip(p['w_in'], _COL_GROUPS):
        out = _mm(h2, w, MXU_DT if store_mx else F32).reshape(b, l, w.shape[1])
        cols.update({name: (out, off) for name, off in first.items()})

    def proj(name):
        return cols[name]

    hm, st_m = _mlstm_branch(proj, p, init[0], b, l)
    if on_grid:
        yr, bon, st_r, v_first = _rwkv7_grid_branch(proj, h2, p, init[1], v_first, b, l, d)
    else:
        yr, bon, st_r, v_first = _rwkv7_seq_branch(proj, h2, p, init[1], v_first, b, l, d)
    oh, st_h = _hgrn2_branch(proj, p, init[2])
    states = (st_m, st_r, st_h)
    if not need_out:
        return None, None, states, v_first
    out, h_next = _epilogue(hm, proj('m_o'), proj('m_z'), yr, bon, proj('r_z'), oh, proj('h_z'), proj('gate'),
                            xs, gx, p, final_w, nxt)
    return out, h_next, states, v_first


def kernel(x, c, ctx, c_ctx, norm_w, ada_w, ada_b, w_in, gate_b, ml_conv, ml_if_b, ml_norm_w, rk_mu, rk_w0, rk_w2, rk_a0, rk_a2, rk_kk, rk_ka, rk_rk, rk_v0, rk_v1, rk_v2, rk_ln_w, rk_ln_b, hg_f_b, hg_lb, hg_norm_w, w_pm, w_pr, w_ph, w_out, final_norm_w):
    batch, _, d = x.shape
    depth = w_in.shape[0]
    lb_p = jax.nn.softmax(hg_lb.astype(F32), axis=1)
    lower_bounds = jnp.cumsum(lb_p, axis=1) - lb_p[:, :1]
    xs, cs = x, ctx
    vf_x, vf_c = None, None
    cond = jnp.concatenate([jax.nn.silu(c), jax.nn.silu(c_ctx)[None]], axis=0)
    mods = []
    for l in range(depth):
        mod = _mm(cond, ada_w[l]) + ada_b[l]
        sh_x, sc_x, g_x = jnp.split(mod[:batch, None, :], 3, axis=-1)
        sh_c, sc_c, g_c = (jnp.broadcast_to(t[None, None, :], (batch, 1, d)) for t in jnp.split(mod[batch], 3, axis=-1))
        mods.append(((sh_x, 1.0 + sc_x, g_x), (sh_c, 1.0 + sc_c, g_c)))
    hx = _mx(_rms_norm(xs, norm_w[0]) * mods[0][0][1] + mods[0][0][0])
    hc = _mx(_rms_norm(cs, norm_w[0]) * mods[0][1][1] + mods[0][1][0])
    for l in range(depth):
        last = l == depth - 1
        p = {'w_in': _pack_w_in(w_in[l]), 'gate_b': gate_b[l], 'ml_conv': ml_conv[l], 'ml_if_b': ml_if_b[l],
             'ml_norm_w': ml_norm_w[l], 'rk_mu': rk_mu[l], 'rk_w0': rk_w0[l], 'rk_w2': rk_w2[l],
             'rk_a0': rk_a0[l], 'rk_a2': rk_a2[l], 'rk_kk': rk_kk[l], 'rk_ka': rk_ka[l], 'rk_rk': rk_rk[l],
             'rk_v0': rk_v0[l - 1] if l > 0 else None, 'rk_v1': rk_v1[l - 1] if l > 0 else None,
             'rk_v2': rk_v2[l - 1] if l > 0 else None, 'rk_ln_w': rk_ln_w[l], 'rk_ln_b': rk_ln_b[l],
             'hg_f_b': hg_f_b[l], 'hg_lb': lower_bounds[:, l], 'hg_norm_w': hg_norm_w[l],
             'w_pm': w_pm[l], 'w_pr': w_pr[l], 'w_ph': w_ph[l], 'w_out': w_out[l]}
        (_, _, gate_x), (_, _, gate_c) = mods[l]
        nxt_x = None if last else (norm_w[l + 1], mods[l + 1][0][1], mods[l + 1][0][0])
        nxt_c = None if last else (norm_w[l + 1], mods[l + 1][1][1], mods[l + 1][1][0])
        cs, hc, st_c, vf_c = _mixer(hc, p, _zero_states(batch, d), False, vf_c, not last, cs, gate_c, None, nxt_c)
        xs, hx, _, vf_x = _mixer(hx, p, st_c, True, vf_x, True, xs, gate_x, final_norm_w if last else None, nxt_x)
    return xs
```

```python
import functools

import jax
import jax.numpy as jnp
from jax import lax
from jax.experimental import pallas as pl
from jax.experimental.pallas import tpu as pltpu

F32 = jnp.float32
MXU_DT = jnp.bfloat16

NORM_EPS = 1e-6
GRID_W = 64
ML_HEADS = 4
ML_CHUNK = 256
RK_HD = 64
RK_CHUNK = 64
RK_GROUP = 256
RK_BATCH = 4
RK_GN_EPS = 64e-5
HG_HEADS = 8
HG_SUB = 16
HG_TILE = 128
EP_TILE = 256
SEG_W = 256
VMEM_LIMIT = 56 * 1024 * 1024


def _nt(a, b):
    return lax.dot_general(a, b, (((1,), (1,)), ((), ())), preferred_element_type=F32)


def _tn(a, b):
    return lax.dot_general(a, b, (((0,), (0,)), ((), ())), preferred_element_type=F32)


def _nn(a, b):
    return jnp.dot(a, b, preferred_element_type=F32)


def _mx(a):
    return a.astype(MXU_DT)


def _cumsum_mm(tri, x):
    hi = x.astype(MXU_DT)
    r1 = x - hi.astype(F32)
    mid = r1.astype(MXU_DT)
    lo = (r1 - mid.astype(F32)).astype(MXU_DT)
    one = lambda t: _nn(t, hi) + _nn(t, mid) + _nn(t, lo)
    return [one(t) for t in tri] if isinstance(tri, (list, tuple)) else one(tri)


def _seg_ones(seg):
    sh = seg.bit_length() - 1
    ri = lax.broadcasted_iota(jnp.int32, (SEG_W, SEG_W), 0)
    ci = lax.broadcasted_iota(jnp.int32, (SEG_W, SEG_W), 1)
    return ((ri >> sh) == (ci >> sh)).astype(F32).astype(MXU_DT)


def _segsum(x, ones):
    outs = [_nn(_mx(x[:, g * SEG_W:(g + 1) * SEG_W]), ones) for g in range(x.shape[1] // SEG_W)]
    return outs[0] if len(outs) == 1 else jnp.concatenate(outs, axis=1)


def _sigmoid(x):
    return jax.nn.sigmoid(x)


def _silu(x):
    return x * jax.nn.sigmoid(x)


def _params(sem):
    return pltpu.CompilerParams(dimension_semantics=sem, vmem_limit_bytes=VMEM_LIMIT)


def _mm_kernel(x_ref, w_ref, o_ref):
    o_ref[...] = _nn(_mx(x_ref[...]), w_ref[...]).astype(o_ref.dtype)


def _mm(x, w, out_dtype=F32):
    m, k = x.shape
    n = w.shape[1]
    mp = -(-m // 8) * 8
    npad = -(-n // 128) * 128
    if mp != m:
        x = jnp.pad(x, ((0, mp - m), (0, 0)))
    wb = w.astype(MXU_DT)
    if npad != n:
        wb = jnp.pad(wb, ((0, 0), (0, npad - n)))
    tm = next(t for t in (2048, 1024, 512, 256, 128, 64, 32, 16, 8) if mp % t == 0)
    tn = next(t for t in (1024, 512, 256, 128) if npad % t == 0)
    out = pl.pallas_call(
        _mm_kernel,
        grid=(mp // tm, npad // tn),
        in_specs=[pl.BlockSpec((tm, k), lambda i, j: (i, 0)),
                  pl.BlockSpec((k, tn), lambda i, j: (0, j))],
        out_specs=pl.BlockSpec((tm, tn), lambda i, j: (i, j)),
        out_shape=jax.ShapeDtypeStruct((mp, npad), out_dtype),
        compiler_params=_params(("parallel", "parallel")),
        name="mm",
    )(x, wb)
    if mp != m or npad != n:
        out = out[:m, :n]
    return out


def _mlstm_kernel(*refs, rev, heads, hd, nc, has_acc, conv):
    it = iter(refs)
    if conv:
        q_ref, qp_ref, qn_ref, k_ref, kp_ref, kn_ref, v_ref, cw_ref = (next(it) for _ in range(8))
    else:
        q_ref, k_ref, v_ref = (next(it) for _ in range(3))
    gcol_ref, grow_ref, c0_ref, n0_ref, m0_ref = (next(it) for _ in range(5))
    acc_ref = next(it) if has_acc else None
    h_ref, c_ref, n_ref, m_ref = (next(it) for _ in range(4))
    step = pl.program_id(1)

    @pl.when(step == 0)
    def _():
        c_ref[...] = c0_ref[...]
        n_ref[...] = n0_ref[...]
        m_ref[...] = m0_ref[...]

    t = q_ref.shape[1]
    ci = nc - 1 - step if rev else step
    has_prev = jnp.where(ci > 0, 1.0, 0.0)
    has_next = jnp.where(ci < nc - 1, 1.0, 0.0)
    trow = lax.broadcasted_iota(jnp.int32, (t, 1), 0)

    def conv_silu(u, prev_blk, next_blk, w3):
        up = jnp.where(trow == 0, prev_blk[7:8, :] * has_prev, pltpu.roll(u, 1, 0))
        dn = jnp.where(trow == t - 1, next_blk[0:1, :] * has_next, pltpu.roll(u, t - 1, 0))
        return _silu(up * w3[0:1, :] + u * w3[1:2, :] + dn * w3[2:3, :])

    if conv:
        q_all = conv_silu(q_ref[0], qp_ref[0], qn_ref[0], cw_ref[0:3, :]) * (hd ** -0.5)
        k_all = conv_silu(k_ref[0], kp_ref[0], kn_ref[0], cw_ref[3:6, :])
        qo_ref, ko_ref = next(it), next(it)
        qo_ref[0] = q_all
        ko_ref[0] = k_all
    else:
        q_all, k_all = q_ref[0], k_ref[0]

    row = lax.broadcasted_iota(jnp.int32, (t, t), 0)
    col = lax.broadcasted_iota(jnp.int32, (t, t), 1)
    mask = (col >= row) if rev else (col <= row)
    last = 0 if rev else t - 1
    hs = range(heads)
    lanes = [slice(h * hd, (h + 1) * hd) for h in hs]
    q32 = [q_all[:, sl] for sl in lanes]
    qc = [_mx(x) for x in q32]
    kc = [k_all[:, sl] for sl in lanes]
    vc = [_mx(v_ref[0, :, sl]) for sl in lanes]
    bc_col = [gcol_ref[0, :, h:h + 1] for h in hs]
    i_col = [gcol_ref[0, :, heads + h:heads + h + 1] for h in hs]
    m_prev = [m_ref[0, h, 0:1, 0:1] for h in hs]
    c_mem = [c_ref[0, h] for h in hs]
    n_mem = [n_ref[0, h] for h in hs]
    dmat = [jnp.where(mask, bc_col[h] - grow_ref[0, 0, h:h + 1, :] + grow_ref[0, 0, heads + h:heads + h + 1, :],
                      -jnp.inf) for h in hs]
    inter = [bc_col[h] + m_prev[h] for h in hs]
    m_t = [jnp.maximum(jnp.max(dmat[h], axis=-1, keepdims=True), inter[h]) for h in hs]
    qk = [_nt(qc[h], _mx(kc[h])) for h in hs]
    q_c = [_nn(qc[h], _mx(c_mem[h])) for h in hs]
    s = [qk[h] * jnp.exp(dmat[h] - m_t[h]) for h in hs]
    w_inter = [jnp.exp(inter[h] - m_t[h]) for h in hs]
    num = [_nn(_mx(s[h]), vc[h]) + w_inter[h] * q_c[h] for h in hs]
    for h in hs:
        qn = jnp.sum(q32[h] * n_mem[h], axis=-1, keepdims=True)
        den = jnp.sum(s[h], axis=-1, keepdims=True) + w_inter[h] * qn
        h_out = num[h] * (1.0 / jnp.maximum(jnp.abs(den), jnp.exp(-m_t[h])))
        if has_acc:
            h_out = h_out + acc_ref[0, :, lanes[h]]
        h_ref[0, :, lanes[h]] = h_out
    for h in hs:
        total = bc_col[h][last:last + 1, :]
        g_col = total - bc_col[h] + i_col[h]
        m_new = jnp.maximum(total + m_prev[h], jnp.max(g_col, axis=0, keepdims=True))
        wk = jnp.exp(g_col - m_new)
        dec = jnp.exp(total + m_prev[h] - m_new)
        kw = kc[h] * wk
        c_ref[0, h] = dec * c_mem[h] + _tn(_mx(kw), vc[h])
        n_ref[0, h] = dec * n_mem[h] + jnp.sum(kw, axis=0, keepdims=True)
        m_ref[0, h] = jnp.broadcast_to(m_new, m_ref.shape[2:])


def _mlstm_scan(pq, pk, pv, conv_w, log_i, log_f, state, rev, acc=None, qk=None):
    b, l, _ = pq[0].shape
    w = conv_w.shape[-1]
    heads = ML_HEADS
    hd = w // heads
    t = min(ML_CHUNK, l)
    assert l % t == 0
    nc = l // t
    nb8 = l // 8
    lf = log_f.reshape(b, heads, nc, t)
    bcum = jnp.flip(jnp.cumsum(jnp.flip(lf, 3), axis=3), 3) if rev else jnp.cumsum(lf, axis=3)
    rows = jnp.concatenate([bcum, log_i.reshape(b, heads, nc, t)], axis=1)
    grow = rows.transpose(0, 2, 1, 3)
    gcol = rows.reshape(b, 2 * heads, l).transpose(0, 2, 1)
    ch = (lambda c: nc - 1 - c) if rev else (lambda c: c)
    cidx = lambda bi, c: (bi, ch(c), 0)
    ridx = lambda bi, c: (bi, ch(c), 0, 0)
    sidx = lambda bi, c: (bi, 0, 0, 0)
    c0, n0, m0 = state
    seq = pl.BlockSpec((1, t, w), cidx)
    col = lambda x: pl.BlockSpec((1, t, w), lambda bi, c: (bi, ch(c), x[1] // w))
    prv = lambda x: pl.BlockSpec((1, 8, w), lambda bi, c: (bi, jnp.maximum(ch(c) * (t // 8) - 1, 0), x[1] // w))
    nxt = lambda x: pl.BlockSpec((1, 8, w), lambda bi, c: (bi, jnp.minimum((ch(c) + 1) * (t // 8), nb8 - 1), x[1] // w))
    st_specs = [pl.BlockSpec((1, heads, hd, hd), sidx), pl.BlockSpec((1, heads, 1, hd), sidx),
                pl.BlockSpec((1, heads, 8, 128), sidx)]
    conv = qk is None
    if conv:
        in_specs = [col(pq), prv(pq), nxt(pq), col(pk), prv(pk), nxt(pk), col(pv),
                    pl.BlockSpec((6, w), lambda bi, c: (0, 0))]
        args = [pq[0], pq[0], pq[0], pk[0], pk[0], pk[0], pv[0], conv_w.reshape(6, w)]
    else:
        in_specs = [seq, seq, col(pv)]
        args = [qk[0], qk[1], pv[0]]
    in_specs += [pl.BlockSpec((1, t, 2 * heads), cidx), pl.BlockSpec((1, 1, 2 * heads, t), ridx)] + st_specs
    args += [gcol, grow, c0, n0, m0]
    if acc is not None:
        in_specs.append(seq)
        args.append(acc)
    seq_out = jax.ShapeDtypeStruct((b, l, w), F32)
    out = pl.pallas_call(
        functools.partial(_mlstm_kernel, rev=rev, heads=heads, hd=hd, nc=nc, has_acc=acc is not None, conv=conv),
        grid=(b, nc),
        in_specs=in_specs,
        out_specs=[seq] + st_specs + ([seq, seq] if conv else []),
        out_shape=[seq_out, jax.ShapeDtypeStruct(c0.shape, F32), jax.ShapeDtypeStruct(n0.shape, F32),
                   jax.ShapeDtypeStruct(m0.shape, F32)] + ([seq_out, seq_out] if conv else []),
        compiler_params=_params(("parallel", "arbitrary")),
        name="mlstm_rev" if rev else "mlstm_fwd",
    )(*args)
    return out[0], (out[1], out[2], out[3]), ((out[4], out[5]) if conv else qk)


def _rwkv_core(rows, s_ref, rev):
    c, w = rows[0][0].shape
    g_w = RK_GROUP
    nh = g_w // RK_HD
    assert c == RK_HD and c & (c - 1) == 0
    sh = c.bit_length() - 1
    rj = lax.broadcasted_iota(jnp.int32, (g_w, g_w), 0)
    cj = lax.broadcasted_iota(jnp.int32, (g_w, g_w), 1)
    blk = (rj >> sh) == (cj >> sh)
    tt, ss = rj & (c - 1), cj & (c - 1)
    incl = blk & ((ss >= tt) if rev else (ss <= tt))
    strict = blk & ((ss > tt) if rev else (ss < tt))
    eye = (rj == cj).astype(F32)
    tri = incl[:c, :c].astype(F32).astype(MXU_DT)
    lo_half = lax.broadcasted_iota(jnp.int32, (2 * g_w, 2 * c), 1) < c
    last = 0 if rev else c - 1
    zero = jnp.zeros((), F32)
    chains = [(bi, g) for bi in range(len(rows)) for g in range(w // g_w)]

    def bd(x):
        return _mx(jnp.where(blk, jnp.concatenate([x] * nh, axis=0), zero))

    def rsum(z):
        out = z[0:c]
        for i in range(1, nh):
            out = out + z[i * c:(i + 1) * c]
        return out

    a_bd, r_bd, v_bd, bk_t, r_t, v_g, bk_hat, dec = {}, {}, {}, {}, {}, {}, {}, {}
    for bi, (r, lw, kd, v, kk, a) in enumerate(rows):
        ka = kk * a
        gi = _cumsum_mm(tri, lw)
        tot = gi[last:last + 1, :]
        e_neg = jnp.exp(-gi)
        e_tail = jnp.exp(tot - gi)
        full = {'a': -kk * jnp.exp(gi - lw), 'b': ka * e_neg, 'k': kd * e_neg, 'r': r * jnp.exp(gi),
                'bh': ka * e_tail, 'kh': kd * e_tail, 'dec': jnp.exp(tot), 'v': v}
        for g in range(w // g_w):
            x = {n: t[:, g * g_w:(g + 1) * g_w] for n, t in full.items()}
            ch = (bi, g)
            a_bd[ch], r_bd[ch], v_bd[ch] = bd(x['a']), bd(x['r']), bd(x['v'])
            bk_t[ch] = _mx(jnp.concatenate([x['b'], x['k']], axis=0))
            bk_hat[ch] = _mx(jnp.concatenate([x['bh'], x['kh']], axis=0))
            r_t[ch], v_g[ch], dec[ch] = _mx(x['r']), x['v'], x['dec']

    l_ab, a_ak, a_rb, a_rk = {}, {}, {}, {}
    for ch in chains:
        prod = _nt(jnp.concatenate([a_bd[ch], r_bd[ch]], axis=0), bk_t[ch])
        swp = pltpu.roll(prod, c, 1)
        xb = jnp.where(lo_half, prod, swp)
        xk = jnp.where(lo_half, swp, prod)
        xb = jnp.concatenate([xb, xb], axis=1)
        xk = jnp.concatenate([xk, xk], axis=1)
        l_ab[ch] = jnp.where(strict, xb[:g_w], zero)
        a_ak[ch] = _mx(jnp.where(strict, xk[:g_w], zero))
        a_rb[ch] = _mx(rsum(jnp.where(incl, xb[g_w:], zero)))
        a_rk[ch] = _mx(rsum(jnp.where(incl, xk[g_w:], zero)))

    def live(z, k):
        return z if k == 0 else jnp.concatenate(
            [z[h * c + (0 if rev else k):h * c + (c - k if rev else c)] for h in range(nh)], axis=0)

    def spread(z, k):
        if k == 0:
            return z
        zeros = jnp.zeros((k, z.shape[1]), z.dtype)
        parts = [z[h * (c - k):(h + 1) * (c - k)] for h in range(nh)]
        return jnp.concatenate([x for p_ in parts for x in ((p_, zeros) if rev else (zeros, p_))], axis=0)

    xs = {ch: _mx(l_ab[ch]) for ch in chains}
    ps = {ch: eye + l_ab[ch] for ch in chains}
    for i in range(max(1, (c - 1).bit_length() - 1)):
        k = (2 ** (i + 1)) // 16 * 16
        xs = {ch: _mx(spread(_nn(live(xs[ch], k), xs[ch]), k)) for ch in chains}
        ps = {ch: ps[ch] + spread(_nn(live(_mx(ps[ch]), k), xs[ch]), k) for ch in chains}

    t_cat = {ch: _mx(rsum(ps[ch])) for ch in chains}
    w_cat = {ch: _mx(_nn(t_cat[ch], a_bd[ch])) for ch in chains}
    ta = {ch: _mx(_nn(t_cat[ch], a_ak[ch])) for ch in chains}
    u0 = {ch: _nn(ta[ch], v_bd[ch]) for ch in chains}
    y_k = {ch: _nn(a_rk[ch], v_bd[ch]) for ch in chains}
    s_mem = {ch: s_ref[ch[0], ch[1]] for ch in chains}
    s_b = {ch: _mx(s_mem[ch]) for ch in chains}
    u = {ch: _nt(w_cat[ch], s_b[ch]) + u0[ch] for ch in chains}
    y = {ch: _nt(r_t[ch], s_b[ch]) + _nn(a_rb[ch], bd(u[ch])) + y_k[ch] for ch in chains}
    for ch in chains:
        upd = _tn(_mx(jnp.concatenate([u[ch], v_g[ch]], axis=0)), bk_hat[ch])
        s_ref[ch[0], ch[1]] = s_mem[ch] * dec[ch] + jnp.where(blk, upd, zero)
    return [jnp.concatenate([y[(bi, g)] for g in range(w // g_w)], axis=1) for bi in range(len(rows))]


def _rwkv_seq_kernel(*refs, rev, has_acc):
    r_ref, lw_ref, kd_ref, v_ref, kk_ref, a_ref, s0_ref = refs[:7]
    acc_ref = refs[7] if has_acc else None
    y_ref, s_ref = refs[7 + has_acc:]

    @pl.when(pl.program_id(1) == 0)
    def _():
        s_ref[...] = s0_ref[...]

    y, = _rwkv_core([(r_ref[0], lw_ref[0], kd_ref[0], v_ref[0], kk_ref[0], a_ref[0])], s_ref, rev)
    if has_acc:
        y = y + acc_ref[0]
    y_ref[0] = y


def _rwkv_scan_seq(r, lw, kd, v, kk, a, state, rev, acc=None):
    b, l, w = r.shape
    c = RK_CHUNK
    nc = l // c
    cidx = (lambda bi, ch: (bi, nc - 1 - ch, 0)) if rev else (lambda bi, ch: (bi, ch, 0))
    seq = pl.BlockSpec((1, c, w), cidx)
    st = pl.BlockSpec((1,) + state.shape[1:], lambda bi, ch: (bi, 0, 0, 0))
    args = [r, lw, kd, v, kk, a, state] + ([acc] if acc is not None else [])
    y, s = pl.pallas_call(
        functools.partial(_rwkv_seq_kernel, rev=rev, has_acc=acc is not None),
        grid=(b, nc),
        in_specs=[seq] * 6 + [st] + ([seq] if acc is not None else []),
        out_specs=[seq, st],
        out_shape=[jax.ShapeDtypeStruct((b, l, w), F32), jax.ShapeDtypeStruct(state.shape, F32)],
        compiler_params=_params(("parallel", "arbitrary")),
        name="rwkv_seq_rev" if rev else "rwkv_seq_fwd",
    )(*args)
    return y, s


_PV_ROWS = ('mu_r', 'mu_k', 'mu_v', 'w0', 'a0', 'kk', 'ka', 'rk', 'v0')


def _rwkv_grid_kernel(*refs, rev, nc, has_vf, has_acc, emit_vf):
    it = iter(refs)
    pr, pr_u, pr_d, pk, pk_u, pk_d, pv, pv_u, pv_d, wa, wa_u, wa_d = (next(it) for _ in range(12))
    vf_ref, lora_ref = (next(it), next(it)) if has_vf else (None, None)
    pvec, mu_wa, w2_ref, a2_ref, s0_ref = (next(it) for _ in range(5))
    accy_ref, accb_ref = (next(it), next(it)) if has_acc else (None, None)
    y_ref, bon_ref, s_ref = (next(it) for _ in range(3))
    vfo_ref = next(it) if emit_vf else None
    step = pl.program_id(1)

    @pl.when(step == 0)
    def _():
        s_ref[...] = s0_ref[...]

    c = pr.shape[1]
    w = pr.shape[2]
    rd = w // 16
    ci = nc - 1 - step if rev else step
    has_up = jnp.where(ci > 0, 1.0, 0.0)
    has_dn = jnp.where(ci < nc - 1, 1.0, 0.0)
    trow = lax.broadcasted_iota(jnp.int32, (c, 1), 0)
    prow = lambda name: pvec[_PV_ROWS.index(name):_PV_ROWS.index(name) + 1, :]

    def left(x):
        return jnp.where(trow == 0, 0.0, pltpu.roll(x, 1, 0))

    def right(x):
        return jnp.where(trow == c - 1, 0.0, pltpu.roll(x, c - 1, 0))

    ones = _seg_ones(RK_HD)
    d = 1 if rev else 0
    rows, bons = [], []
    for bi in range(pr.shape[0]):
        def lerp_wide(cur_ref, up_ref, dn_ref, mu):
            cur = cur_ref[bi]
            q = w // 4
            sh = jnp.concatenate([left(cur[:, :q]), right(cur[:, q:2 * q]),
                                  up_ref[bi] * has_up, dn_ref[bi] * has_dn], axis=1)
            return cur + mu * (sh - cur)

        r = lerp_wide(pr, pr_u, pr_d, prow('mu_r'))
        k = lerp_wide(pk, pk_u, pk_d, prow('mu_k'))
        v = lerp_wide(pv, pv_u, pv_d, prow('mu_v'))

        cur = wa[bi]
        qd = (lax.broadcasted_iota(jnp.int32, cur.shape, 1) & (rd - 1)) >> ((rd // 4).bit_length() - 1)
        sh = jnp.where(qd == 0, left(cur), jnp.where(qd == 1, right(cur),
                                                     jnp.where(qd == 2, wa_u[bi] * has_up, wa_d[bi] * has_dn)))
        xwa = cur + mu_wa[...] * (sh - cur)
        xw = xwa[:, d * rd:(d + 1) * rd]
        xa = xwa[:, (2 + d) * rd:(3 + d) * rd]
        z = -(prow('w0') + _nn(_mx(jnp.tanh(xw)), w2_ref[...]))
        softplus = jnp.maximum(z, 0.0) + jnp.log(1.0 + jnp.exp(-jnp.abs(z)))
        lw = -jnp.exp(-softplus - 0.5)
        a = _sigmoid(prow('a0') + _nn(_mx(xa), a2_ref[...]))
        if has_vf:
            v = v + (vf_ref[bi] - v) * _sigmoid(prow('v0') + lora_ref[bi])
        if emit_vf:
            vfo_ref[bi] = v
        kkn = k * prow('kk')
        kk = kkn * lax.rsqrt(jnp.maximum(_segsum(kkn * kkn, ones), 1e-24))
        kd = k * (1.0 + (a - 1.0) * prow('ka'))
        bons.append(_segsum(r * kd * prow('rk'), ones) * v)
        rows.append((r, lw, kd, v, kk, a))
    ys = _rwkv_core(rows, s_ref, rev)
    for bi, (y, bon) in enumerate(zip(ys, bons)):
        if has_acc:
            y = y + accy_ref[bi]
            bon = bon + accb_ref[bi]
        y_ref[bi] = y
        bon_ref[bi] = bon


def _rwkv_scan_grid(pr, pk, pv, wa, pvec, mu_wa, w2, a2, state, rev, vf=None, lora=None,
                    acc=None, emit_vf=False):
    b, l, _ = pr[0].shape
    w = pvec.shape[1]
    c = RK_CHUNK
    assert c == GRID_W
    nc = l // c
    wq = w // 4
    ch = (lambda s: nc - 1 - s) if rev else (lambda s: s)
    cur = lambda lane: (lambda bi, s: (bi, ch(s), lane))
    upi = lambda lane: (lambda bi, s: (bi, jnp.maximum(ch(s) - 1, 0), lane))
    dni = lambda lane: (lambda bi, s: (bi, jnp.minimum(ch(s) + 1, nc - 1), lane))
    bb = RK_BATCH if b % RK_BATCH == 0 else 1
    seq = pl.BlockSpec((bb, c, w), cur(0))
    wide = lambda x: [pl.BlockSpec((bb, c, w), cur(x[1] // w)), pl.BlockSpec((bb, c, wq), upi(x[1] // wq + 2)),
                      pl.BlockSpec((bb, c, wq), dni(x[1] // wq + 3))]
    wa_w = mu_wa.shape[1]
    small = [pl.BlockSpec((bb, c, wa_w), cur(wa[1] // wa_w)), pl.BlockSpec((bb, c, wa_w), upi(wa[1] // wa_w)),
             pl.BlockSpec((bb, c, wa_w), dni(wa[1] // wa_w))]
    full = lambda x: pl.BlockSpec(x.shape, lambda bi, s: (0,) * x.ndim)
    st = pl.BlockSpec((bb,) + state.shape[1:], lambda bi, s: (bi, 0, 0, 0))
    in_specs = wide(pr) + wide(pk) + wide(pv) + small
    args = [pr[0]] * 3 + [pk[0]] * 3 + [pv[0]] * 3 + [wa[0]] * 3
    if vf is not None:
        in_specs += [seq, seq]
        args += [vf, lora]
    in_specs += [full(pvec), full(mu_wa), full(w2), full(a2), st]
    args += [pvec, mu_wa, w2, a2, state]
    if acc is not None:
        in_specs += [seq, seq]
        args += list(acc)
    out_specs = [seq, seq, st]
    out_shape = [jax.ShapeDtypeStruct((b, l, w), F32), jax.ShapeDtypeStruct((b, l, w), F32),
                 jax.ShapeDtypeStruct(state.shape, F32)]
    if emit_vf:
        out_specs.append(seq)
        out_shape.append(jax.ShapeDtypeStruct((b, l, w), F32))
    return pl.pallas_call(
        functools.partial(_rwkv_grid_kernel, rev=rev, nc=nc, has_vf=vf is not None,
                          has_acc=acc is not None, emit_vf=emit_vf),
        grid=(b // bb, nc),
        in_specs=in_specs,
        out_specs=out_specs,
        out_shape=out_shape,
        compiler_params=_params(("parallel", "arbitrary")),
        name="rwkv_grid_rev" if rev else "rwkv_grid_fwd",
    )(*args)


def _hgrn_kernel(*refs, rev, heads, hd, has_acc):
    q_ref, f_ref, v_ref, fb_ref, lb_ref, s0_ref = refs[:6]
    acc_ref = refs[6] if has_acc else None
    o_ref, s_ref = refs[6 + has_acc:]

    @pl.when(pl.program_id(1) == 0)
    def _():
        s_ref[...] = s0_ref[...]

    tt = q_ref.shape[1]
    sub = HG_SUB
    sh = sub.bit_length() - 1
    nsub = tt // sub
    mid = sub // 2 if rev else sub // 2 - 1
    last = 0 if rev else sub - 1
    ri = lax.broadcasted_iota(jnp.int32, (tt, tt), 0)
    ci = lax.broadcasted_iota(jnp.int32, (tt, tt), 1)
    causal = ((ri >> sh) == (ci >> sh)) & ((ci >= ri) if rev else (ci <= ri))
    lb = lb_ref[...]
    f = lb + (1.0 - lb) * _sigmoid(f_ref[0] + fb_ref[...])
    g_all = _cumsum_mm(causal.astype(F32).astype(MXU_DT), jnp.log(f))

    def row_of_each_sub(r):
        return jnp.concatenate([jnp.broadcast_to(g_all[j * sub + r:j * sub + r + 1], (sub, g_all.shape[1]))
                                for j in range(nsub)], axis=0)

    g_mid, g_last = row_of_each_sub(mid), row_of_each_sub(last)
    k_all = 1.0 - f
    q_all = _silu(q_ref[0])
    q1 = _mx(q_all * jnp.exp(g_all - g_mid))
    k1 = _mx(k_all * jnp.exp(g_mid - g_all))
    qg_f = q_all * jnp.exp(g_all)
    kl_f = k_all * jnp.exp(g_last - g_all)
    v_all = _mx(v_ref[0])
    assert nsub % 2 == 0
    is_first = lambda j: (j % 2 == 1) == rev
    dec = [jnp.exp(g_all[j * sub + last:j * sub + last + 1]) for j in range(nsub)]
    one_row = jnp.ones_like(dec[0])
    per_sub = lambda vals: jnp.concatenate([jnp.broadcast_to(x, (sub, x.shape[1])) for x in vals], axis=0)
    qg, kl = _mx(qg_f), _mx(kl_f)
    qg2 = _mx(qg_f * per_sub([one_row if is_first(j) else dec[j ^ 1] for j in range(nsub)]))
    kl2 = _mx(kl_f * per_sub([dec[j ^ 1] if is_first(j) else one_row for j in range(nsub)]))
    sub_r, sub_c = ri >> sh, ci >> sh
    second_sees_first = ((sub_r ^ 1) == sub_c) & ((sub_r & 1) == (0 if rev else 1))
    lanes = [slice(h * hd, (h + 1) * hd) for h in range(heads)]
    zero = jnp.zeros((), F32)
    att = [_mx(jnp.where(causal, _nt(q1[:, sl], k1[:, sl]), zero)
               + jnp.where(second_sees_first, _nt(qg[:, sl], kl[:, sl]), zero)) for sl in lanes]
    intra = [_nn(att[h], v_all[:, lanes[h]]) for h in range(heads)]
    pairs = [nsub // 2 - 1 - s if rev else s for s in range(nsub // 2)]
    rows = [slice(2 * p * sub, 2 * (p + 1) * sub) for p in range(nsub // 2)]
    upd = {(p, h): _tn(v_all[rows[p], lanes[h]], kl2[rows[p], lanes[h]]) for p in pairs for h in range(heads)}
    s_mem = [s_ref[0, h] for h in range(heads)]
    for p in pairs:
        dec_pair = dec[2 * p] * dec[2 * p + 1]
        outs = []
        for h in range(heads):
            outs.append(intra[h][rows[p]] + _nt(qg2[rows[p], lanes[h]], _mx(s_mem[h])))
            s_mem[h] = s_mem[h] * dec_pair[:, lanes[h]] + upd[(p, h)]
        o = jnp.concatenate(outs, axis=1)
        if has_acc:
            o = o + acc_ref[0, rows[p], :]
        o_ref[0, rows[p], :] = o
    for h in range(heads):
        s_ref[0, h] = s_mem[h]


def _hgrn_scan(pq, pf, pi, f_b, lb, state, rev, acc=None):
    b, l, _ = pq[0].shape
    w = f_b.shape[1]
    heads = HG_HEADS
    hd = w // heads
    tt = min(HG_TILE, l)
    nc = l // tt
    d = 1 if rev else 0
    ch = (lambda s: nc - 1 - s) if rev else (lambda s: s)
    col = lambda x, extra=0: pl.BlockSpec((1, tt, w), lambda bi, s: (bi, ch(s), x[1] // w + extra))
    seq = pl.BlockSpec((1, tt, w), lambda bi, s: (bi, ch(s), 0))
    vec = pl.BlockSpec((1, w), lambda bi, s: (0, 0))
    st = pl.BlockSpec((1,) + state.shape[1:], lambda bi, s: (bi, 0, 0, 0))
    args = [pq[0], pf[0], pi[0], f_b, lb, state] + ([acc] if acc is not None else [])
    o, s = pl.pallas_call(
        functools.partial(_hgrn_kernel, rev=rev, heads=heads, hd=hd, has_acc=acc is not None),
        grid=(b, nc),
        in_specs=[col(pq), col(pf, d), col(pi), vec, vec, st] + ([seq] if acc is not None else []),
        out_specs=[seq, st],
        out_shape=[jax.ShapeDtypeStruct((b, l, w), F32), jax.ShapeDtypeStruct(state.shape, F32)],
        compiler_params=_params(("parallel", "arbitrary")),
        name="hgrn_rev" if rev else "hgrn_fwd",
    )(*args)
    return o, s


_EV_ROWS = ('ml_norm_w', 'rk_ln_w', 'rk_ln_b', 'hg_norm_w', 'gate_b0', 'gate_b1', 'gate_b2', 'final_w', 'next_w')


def _epilogue_kernel(hm_ref, mo_ref, mz_ref, yr_ref, bon_ref, rz_ref, oh_ref, hz_ref,
                     g0_ref, g1_ref, g2_ref, xs_ref, gx_ref, ev_ref,
                     wpm_ref, wpr_ref, wph_ref, wout_ref, *rest, final, has_next):
    o_ref = rest[2 * has_next]
    ev = lambda name: ev_ref[_EV_ROWS.index(name):_EV_ROWS.index(name) + 1, :]
    d = hm_ref.shape[2]

    def seg_mean(x, seg):
        if seg % 128:
            return _segsum(x, _seg_ones(seg)) * (1.0 / seg)
        parts = [jnp.broadcast_to(jnp.mean(x[:, i:i + seg], axis=-1, keepdims=True), (x.shape[0], seg))
                 for i in range(0, d, seg)]
        return jnp.concatenate(parts, axis=1)

    seg = d // ML_HEADS
    y = hm_ref[0]
    y = y - seg_mean(y, seg)
    y = y * lax.rsqrt(seg_mean(y * y, seg) + NORM_EPS) * ev('ml_norm_w')
    u_m = _sigmoid(mo_ref[0].astype(F32)) * y * _silu(mz_ref[0].astype(F32))
    y = yr_ref[0]
    y = y - seg_mean(y, RK_HD)
    y = y * lax.rsqrt(seg_mean(y * y, RK_HD) + RK_GN_EPS) * ev('rk_ln_w') + ev('rk_ln_b') + bon_ref[0]
    u_r = y * _silu(rz_ref[0].astype(F32))
    seg = d // HG_HEADS
    o = oh_ref[0]
    o = o * lax.rsqrt(seg_mean(o * o, seg) + NORM_EPS) * ev('hg_norm_w')
    u_h = o * _silu(hz_ref[0].astype(F32))

    merged = (_sigmoid(g0_ref[0].astype(F32) + ev('gate_b0')) * _nn(_mx(u_m), wpm_ref[...])
              + _sigmoid(g1_ref[0].astype(F32) + ev('gate_b1')) * _nn(_mx(u_r), wpr_ref[...])
              + _sigmoid(g2_ref[0].astype(F32) + ev('gate_b2')) * _nn(_mx(u_h), wph_ref[...]))
    xs = xs_ref[0] + gx_ref[0] * _nn(_mx(merged), wout_ref[...])
    if final:
        xs = xs * lax.rsqrt(jnp.mean(xs * xs, axis=-1, keepdims=True) + NORM_EPS) * ev('final_w')
    o_ref[0] = xs
    if has_next:
        normed = xs * lax.rsqrt(jnp.mean(xs * xs, axis=-1, keepdims=True) + NORM_EPS) * ev('next_w')
        rest[3][0] = _mx(normed * rest[0][0] + rest[1][0])


def _epilogue(hm, mo, mz, yr, bon, rz, oh, hz, gate, xs, gx, p, final_w, nxt):
    b, l, d = xs.shape
    tt = min(EP_TILE, l)
    seq = pl.BlockSpec((1, tt, d), lambda bi, i: (bi, i, 0))
    col = lambda x, extra=0: pl.BlockSpec((1, tt, d), lambda bi, i: (bi, i, x[1] // d + extra))
    ev = jnp.stack([p['ml_norm_w'], p['rk_ln_w'], p['rk_ln_b'], p['hg_norm_w'],
                    p['gate_b'][0], p['gate_b'][1], p['gate_b'][2],
                    final_w if final_w is not None else jnp.ones((d,), F32),
                    nxt[0] if nxt is not None else jnp.ones((d,), F32)], axis=0)
    wspec = pl.BlockSpec((d, d), lambda bi, i: (0, 0))
    ws = [_mx(p[n]) for n in ('w_pm', 'w_pr', 'w_ph', 'w_out')]
    per_batch = pl.BlockSpec((1, 1, d), lambda bi, i: (bi, 0, 0))
    has_next = nxt is not None
    out = pl.pallas_call(
        functools.partial(_epilogue_kernel, final=final_w is not None, has_next=has_next),
        grid=(b, l // tt),
        in_specs=[seq, col(mo), col(mz), seq, seq, col(rz), seq, col(hz), col(gate), col(gate, 1), col(gate, 2), seq,
                  per_batch, pl.BlockSpec(ev.shape, lambda bi, i: (0, 0))] + [wspec] * 4 + [per_batch] * (2 * has_next),
        out_specs=[seq] * (1 + has_next),
        out_shape=[jax.ShapeDtypeStruct((b, l, d), F32)] + [jax.ShapeDtypeStruct((b, l, d), MXU_DT)] * has_next,
        compiler_params=_params(("parallel", "parallel")),
        name="epilogue",
    )(hm, mo[0], mz[0], yr, bon, rz[0], oh, hz[0], gate[0], gate[0], gate[0], xs, gx, ev, *ws,
      *(nxt[1:] if has_next else ()))
    return out[0], (out[1] if has_next else None)


def _rms_norm(x, w):
    return x * lax.rsqrt(jnp.mean(x * x, axis=-1, keepdims=True) + NORM_EPS) * w


def _bi_shift_seq(u):
    half = u.shape[-1] // 2
    g = jnp.pad(u, ((0, 0), (1, 1), (0, 0)))
    return jnp.concatenate((g[:, :-2, :half], g[:, 2:, half:]), axis=-1)


def _in_layout(d):
    rd = d // 16
    return (('m_q', d), ('m_k', d), ('m_v', d), ('m_o', d), ('m_z', d), ('m_if', 4 * ML_HEADS),
            ('r_r', d), ('r_k', d), ('r_v', d), ('r_z', d), ('r_wa', 4 * rd),
            ('h_q', d), ('h_f', 2 * d), ('h_i', d), ('h_z', d), ('gate', 3 * d))


_COL_GROUPS = ((False, ('m_q', 'm_k', 'r_r', 'r_k', 'r_v', 'h_q', 'h_f')),
               (False, ('r_wa', 'm_if')),
               (True, ('m_v', 'h_i', 'm_o', 'm_z', 'r_z', 'h_z', 'gate')))
_COL_ALIGN = 512


def _pack_w_in(w_in):
    d = w_in.shape[0]
    src, off = {}, 0
    for name, width in _in_layout(d):
        src[name] = (off, width)
        off += width

    def pack(names):
        cols, first, o = [], {}, 0
        for n in names:
            cols.append(w_in[:, src[n][0]:src[n][0] + src[n][1]])
            first[n] = o
            o += src[n][1]
        if o % _COL_ALIGN:
            cols.append(jnp.zeros((d, -o % _COL_ALIGN), w_in.dtype))
        return jnp.concatenate(cols, axis=1).astype(MXU_DT), first

    return [pack(names) for _, names in _COL_GROUPS]


def _take(col, width):
    return col[0][..., col[1]:col[1] + width]


def _zero_states(b, d):
    ml_hd = d // ML_HEADS
    ml = (jnp.zeros((b, ML_HEADS, ml_hd, ml_hd), F32), jnp.zeros((b, ML_HEADS, 1, ml_hd), F32),
          jnp.zeros((b, ML_HEADS, 8, 128), F32))
    rk = jnp.zeros((b, d // RK_GROUP, RK_GROUP, RK_GROUP), F32)
    hg_hd = d // HG_HEADS
    hg = jnp.zeros((b, HG_HEADS, hg_hd, hg_hd), F32)
    return ((ml, ml), (rk, rk), (hg, hg))


def _mlstm_branch(proj, p, init, b, l):
    gl = _take(proj('m_if'), 4 * ML_HEADS) + p['ml_if_b'].reshape(-1)
    gl = gl.transpose(0, 2, 1).reshape(b, 2, 2, ML_HEADS, l)
    acc, qk, states = None, None, []
    for dr in (0, 1):
        log_i = gl[:, dr, 0]
        log_f = jax.nn.log_sigmoid(gl[:, dr, 1])
        acc, st, qk = _mlstm_scan(proj('m_q'), proj('m_k'), proj('m_v'), p['ml_conv'], log_i, log_f,
                                  init[dr], rev=bool(dr), acc=acc, qk=qk)
        states.append(st)
    return acc, tuple(states)


def _rwkv7_seq_branch(proj, h2, p, init, v_first, b, l, d):
    nh, n = d // RK_HD, RK_HD
    rd = d // 16
    mu = p['rk_mu']

    def lerp_shift(u, m):
        return u + m * (_bi_shift_seq(u) - u)

    r = lerp_shift(_take(proj('r_r'), d), mu[0:d])
    k = lerp_shift(_take(proj('r_k'), d), mu[d:2 * d])
    v = lerp_shift(_take(proj('r_v'), d), mu[2 * d:3 * d])
    pwa = _take(proj('r_wa'), 4 * rd)
    xwa = jnp.concatenate([lerp_shift(pwa[..., i * rd:(i + 1) * rd], mu[3 * d + i * rd:3 * d + (i + 1) * rd])
                           for i in range(4)], axis=-1)
    if p['rk_v0'] is None:
        v_first = v
    else:
        lora = _mm(_mm(h2, p['rk_v1']), p['rk_v2']).reshape(b, l, d)
        v = v + (v_first - v) * jax.nn.sigmoid(p['rk_v0'] + lora)
    kk = (k * p['rk_kk']).reshape(b, l, nh, n)
    kk = kk / jnp.maximum(jnp.sqrt(jnp.sum(kk * kk, axis=-1, keepdims=True)), 1e-12)
    kk = kk.reshape(b, l, d)
    acc, states, kd_sum = None, [], 0.0
    for dr in (0, 1):
        xw = xwa[..., dr * rd:(dr + 1) * rd]
        xa = xwa[..., (2 + dr) * rd:(3 + dr) * rd]
        log_w = -jax.nn.softplus(-(p['rk_w0'][dr] + _mm(jnp.tanh(xw).reshape(b * l, rd), p['rk_w2'][dr]).reshape(b, l, d))) - 0.5
        lw = -jnp.exp(log_w)
        a = jax.nn.sigmoid(p['rk_a0'][dr] + _mm(xa.reshape(b * l, rd), p['rk_a2'][dr]).reshape(b, l, d))
        kd = k * (1.0 + (a - 1.0) * p['rk_ka'])
        acc, s = _rwkv_scan_seq(r, lw, kd, v, kk, a, init[dr], rev=bool(dr), acc=acc)
        states.append(s)
        kd_sum = kd_sum + kd
    bonus = jnp.sum((r * kd_sum * p['rk_rk']).reshape(b, l, nh, n), axis=-1, keepdims=True) * v.reshape(b, l, nh, n)
    return acc, bonus.reshape(b, l, d), tuple(states), v_first


def _rwkv7_grid_branch(proj, h2, p, init, v_first, b, l, d):
    mu = p['rk_mu']
    zeros = jnp.zeros((d,), F32)
    has_vf = p['rk_v0'] is not None
    lora = _mm(_mm(h2, p['rk_v1']), p['rk_v2']).reshape(b, l, d) if has_vf else None
    acc, states, vf_out = None, [], v_first
    for dr in (0, 1):
        rows = {'mu_r': mu[0:d], 'mu_k': mu[d:2 * d], 'mu_v': mu[2 * d:3 * d], 'w0': p['rk_w0'][dr],
                'a0': p['rk_a0'][dr], 'kk': p['rk_kk'], 'ka': p['rk_ka'], 'rk': p['rk_rk'],
                'v0': p['rk_v0'] if has_vf else zeros}
        pvec = jnp.stack([rows[n] for n in _PV_ROWS], axis=0)
        emit_vf = (not has_vf) and dr == 0
        out = _rwkv_scan_grid(proj('r_r'), proj('r_k'), proj('r_v'), proj('r_wa'), pvec, mu[3 * d:][None],
                              _mx(p['rk_w2'][dr]), _mx(p['rk_a2'][dr]), init[dr], rev=bool(dr),
                              vf=v_first if has_vf else None, lora=lora, acc=acc, emit_vf=emit_vf)
        acc = (out[0], out[1])
        states.append(out[2])
        if emit_vf:
            vf_out = out[3]
    return acc[0], acc[1], tuple(states), vf_out


def _hgrn2_branch(proj, p, init):
    acc, states = None, []
    for dr in (0, 1):
        acc, s = _hgrn_scan(proj('h_q'), proj('h_f'), proj('h_i'), p['hg_f_b'][dr][None], p['hg_lb'][dr][None],
                            init[dr], rev=bool(dr), acc=acc)
        states.append(s)
    return acc, tuple(states)


def _mixer(h, p, init, on_grid, v_first, need_out, xs, gx, final_w, nxt):
    b, l, d = h.shape
    h2 = h.reshape(b * l, d)
    cols = {}
    for (w, first), (store_mx, _) in zip(p['w_in'], _COL_GROUPS):
        out = _mm(h2, w, MXU_DT if store_mx else F32).reshape(b, l, w.shape[1])
        cols.update({name: (out, off) for name, off in first.items()})

    def proj(name):
        return cols[name]

    hm, st_m = _mlstm_branch(proj, p, init[0], b, l)
    if on_grid:
        yr, bon, st_r, v_first = _rwkv7_grid_branch(proj, h2, p, init[1], v_first, b, l, d)
    else:
        yr, bon, st_r, v_first = _rwkv7_seq_branch(proj, h2, p, init[1], v_first, b, l, d)
    oh, st_h = _hgrn2_branch(proj, p, init[2])
    states = (st_m, st_r, st_h)
    if not need_out:
        return None, None, states, v_first
    out, h_next = _epilogue(hm, proj('m_o'), proj('m_z'), yr, bon, proj('r_z'), oh, proj('h_z'), proj('gate'),
                            xs, gx, p, final_w, nxt)
    return out, h_next, states, v_first


def kernel(x, c, ctx, c_ctx, norm_w, ada_w, ada_b, w_in, gate_b, ml_conv, ml_if_b, ml_norm_w, rk_mu, rk_w0, rk_w2, rk_a0, rk_a2, rk_kk, rk_ka, rk_rk, rk_v0, rk_v1, rk_v2, rk_ln_w, rk_ln_b, hg_f_b, hg_lb, hg_norm_w, w_pm, w_pr, w_ph, w_out, final_norm_w):
    batch, _, d = x.shape
    depth = w_in.shape[0]
    lb_p = jax.nn.softmax(hg_lb.astype(F32), axis=1)
    lower_bounds = jnp.cumsum(lb_p, axis=1) - lb_p[:, :1]
    xs, cs = x, ctx
    vf_x, vf_c = None, None
    cond = jnp.concatenate([jax.nn.silu(c), jax.nn.silu(c_ctx)[None]], axis=0)
    mods = []
    for l in range(depth):
        mod = _mm(cond, ada_w[l]) + ada_b[l]
        sh_x, sc_x, g_x = jnp.split(mod[:batch, None, :], 3, axis=-1)
        sh_c, sc_c, g_c = (jnp.broadcast_to(t[None, None, :], (batch, 1, d)) for t in jnp.split(mod[batch], 3, axis=-1))
        mods.append(((sh_x, 1.0 + sc_x, g_x), (sh_c, 1.0 + sc_c, g_c)))
    hx = _mx(_rms_norm(xs, norm_w[0]) * mods[0][0][1] + mods[0][0][0])
    hc = _mx(_rms_norm(cs, norm_w[0]) * mods[0][1][1] + mods[0][1][0])
    for l in range(depth):
        last = l == depth - 1
        p = {'w_in': _pack_w_in(w_in[l]), 'gate_b': gate_b[l], 'ml_conv': ml_conv[l], 'ml_if_b': ml_if_b[l],
             'ml_norm_w': ml_norm_w[l], 'rk_mu': rk_mu[l], 'rk_w0': rk_w0[l], 'rk_w2': rk_w2[l],
             'rk_a0': rk_a0[l], 'rk_a2': rk_a2[l], 'rk_kk': rk_kk[l], 'rk_ka': rk_ka[l], 'rk_rk': rk_rk[l],
             'rk_v0': rk_v0[l - 1] if l > 0 else None, 'rk_v1': rk_v1[l - 1] if l > 0 else None,
             'rk_v2': rk_v2[l - 1] if l > 0 else None, 'rk_ln_w': rk_ln_w[l], 'rk_ln_b': rk_ln_b[l],
             'hg_f_b': hg_f_b[l], 'hg_lb': lower_bounds[:, l], 'hg_norm_w': hg_norm_w[l],
             'w_pm': w_pm[l], 'w_pr': w_pr[l], 'w_ph': w_ph[l], 'w_out': w_out[l]}
        (_, _, gate_x), (_, _, gate_c) = mods[l]
        nxt_x = None if last else (norm_w[l + 1], mods[l + 1][0][1], mods[l + 1][0][0])
        nxt_c = None if last else (norm_w[l + 1], mods[l + 1][1][1], mods[l + 1][1][0])
        cs, hc, st_c, vf_c = _mixer(hc, p, _zero_states(batch, d), False, vf_c, not last, cs, gate_c, None, nxt_c)
        xs, hx, _, vf_x = _mixer(hx, p, st_c, True, vf_x, True, xs, gate_x, final_norm_w if last else None, nxt_x)
    return xs
```

```python
import functools

import jax
import jax.numpy as jnp
from jax import lax
from jax.experimental import pallas as pl
from jax.experimental.pallas import tpu as pltpu

F32 = jnp.float32
MXU_DT = jnp.bfloat16

NORM_EPS = 1e-6
GRID_W = 64
ML_HEADS = 4
ML_CHUNK = 512
RK_HD = 64
RK_CHUNK = 64
RK_GROUP = 256
RK_BATCH = 4
RK_GN_EPS = 64e-5
HG_HEADS = 8
HG_SUB = 16
HG_TILE = 128
EP_TILE = 256
SEG_W = 256
VMEM_LIMIT = 56 * 1024 * 1024


def _nt(a, b):
    return lax.dot_general(a, b, (((1,), (1,)), ((), ())), preferred_element_type=F32)


def _tn(a, b):
    return lax.dot_general(a, b, (((0,), (0,)), ((), ())), preferred_element_type=F32)


def _nn(a, b):
    return jnp.dot(a, b, preferred_element_type=F32)


def _mx(a):
    return a.astype(MXU_DT)


def _cumsum_mm(tri, x):
    hi = x.astype(MXU_DT)
    r1 = x - hi.astype(F32)
    mid = r1.astype(MXU_DT)
    lo = (r1 - mid.astype(F32)).astype(MXU_DT)
    one = lambda t: _nn(t, hi) + _nn(t, mid) + _nn(t, lo)
    return [one(t) for t in tri] if isinstance(tri, (list, tuple)) else one(tri)


def _seg_ones(seg):
    sh = seg.bit_length() - 1
    ri = lax.broadcasted_iota(jnp.int32, (SEG_W, SEG_W), 0)
    ci = lax.broadcasted_iota(jnp.int32, (SEG_W, SEG_W), 1)
    return ((ri >> sh) == (ci >> sh)).astype(F32).astype(MXU_DT)


def _segsum(x, ones):
    outs = [_nn(_mx(x[:, g * SEG_W:(g + 1) * SEG_W]), ones) for g in range(x.shape[1] // SEG_W)]
    return outs[0] if len(outs) == 1 else jnp.concatenate(outs, axis=1)


def _sigmoid(x):
    return jax.nn.sigmoid(x)


def _silu(x):
    return x * jax.nn.sigmoid(x)


def _params(sem):
    return pltpu.CompilerParams(dimension_semantics=sem, vmem_limit_bytes=VMEM_LIMIT)


def _mm_kernel(x_ref, w_ref, o_ref):
    o_ref[...] = _nn(_mx(x_ref[...]), w_ref[...]).astype(o_ref.dtype)


def _mm(x, w, out_dtype=F32):
    m, k = x.shape
    n = w.shape[1]
    mp = -(-m // 8) * 8
    npad = -(-n // 128) * 128
    if mp != m:
        x = jnp.pad(x, ((0, mp - m), (0, 0)))
    wb = w.astype(MXU_DT)
    if npad != n:
        wb = jnp.pad(wb, ((0, 0), (0, npad - n)))
    tm = next(t for t in (2048, 1024, 512, 256, 128, 64, 32, 16, 8) if mp % t == 0)
    tn = next(t for t in (1024, 512, 256, 128) if npad % t == 0)
    out = pl.pallas_call(
        _mm_kernel,
        grid=(mp // tm, npad // tn),
        in_specs=[pl.BlockSpec((tm, k), lambda i, j: (i, 0)),
                  pl.BlockSpec((k, tn), lambda i, j: (0, j))],
        out_specs=pl.BlockSpec((tm, tn), lambda i, j: (i, j)),
        out_shape=jax.ShapeDtypeStruct((mp, npad), out_dtype),
        compiler_params=_params(("parallel", "parallel")),
        name="mm",
    )(x, wb)
    if mp != m or npad != n:
        out = out[:m, :n]
    return out


def _mlstm_kernel(*refs, rev, heads, hd, nc, has_acc, conv):
    it = iter(refs)
    if conv:
        q_ref, qp_ref, qn_ref, k_ref, kp_ref, kn_ref, v_ref, cw_ref = (next(it) for _ in range(8))
    else:
        q_ref, k_ref, v_ref = (next(it) for _ in range(3))
    gcol_ref, grow_ref, c0_ref, n0_ref, m0_ref = (next(it) for _ in range(5))
    acc_ref = next(it) if has_acc else None
    h_ref, c_ref, n_ref, m_ref = (next(it) for _ in range(4))
    step = pl.program_id(1)

    @pl.when(step == 0)
    def _():
        c_ref[...] = c0_ref[...]
        n_ref[...] = n0_ref[...]
        m_ref[...] = m0_ref[...]

    t = q_ref.shape[1]
    ci = nc - 1 - step if rev else step
    has_prev = jnp.where(ci > 0, 1.0, 0.0)
    has_next = jnp.where(ci < nc - 1, 1.0, 0.0)
    trow = lax.broadcasted_iota(jnp.int32, (t, 1), 0)

    def conv_silu(u, prev_blk, next_blk, w3):
        up = jnp.where(trow == 0, prev_blk[7:8, :] * has_prev, pltpu.roll(u, 1, 0))
        dn = jnp.where(trow == t - 1, next_blk[0:1, :] * has_next, pltpu.roll(u, t - 1, 0))
        return _silu(up * w3[0:1, :] + u * w3[1:2, :] + dn * w3[2:3, :])

    if conv:
        q_all = conv_silu(q_ref[0], qp_ref[0], qn_ref[0], cw_ref[0:3, :]) * (hd ** -0.5)
        k_all = conv_silu(k_ref[0], kp_ref[0], kn_ref[0], cw_ref[3:6, :])
        qo_ref, ko_ref = next(it), next(it)
        qo_ref[0] = q_all
        ko_ref[0] = k_all
    else:
        q_all, k_all = q_ref[0], k_ref[0]

    row = lax.broadcasted_iota(jnp.int32, (t, t), 0)
    col = lax.broadcasted_iota(jnp.int32, (t, t), 1)
    mask = (col >= row) if rev else (col <= row)
    last = 0 if rev else t - 1
    hs = range(heads)
    lanes = [slice(h * hd, (h + 1) * hd) for h in hs]
    q32 = [q_all[:, sl] for sl in lanes]
    qc = [_mx(x) for x in q32]
    kc = [k_all[:, sl] for sl in lanes]
    vc = [_mx(v_ref[0, :, sl]) for sl in lanes]
    bc_col = [gcol_ref[0, :, h:h + 1] for h in hs]
    i_col = [gcol_ref[0, :, heads + h:heads + h + 1] for h in hs]
    m_prev = [m_ref[0, h, 0:1, 0:1] for h in hs]
    c_mem = [c_ref[0, h] for h in hs]
    n_mem = [n_ref[0, h] for h in hs]
    dmat = [jnp.where(mask, bc_col[h] - grow_ref[0, 0, h:h + 1, :] + grow_ref[0, 0, heads + h:heads + h + 1, :],
                      -jnp.inf) for h in hs]
    inter = [bc_col[h] + m_prev[h] for h in hs]
    m_t = [jnp.maximum(jnp.max(dmat[h], axis=-1, keepdims=True), inter[h]) for h in hs]
    qk = [_nt(qc[h], _mx(kc[h])) for h in hs]
    q_c = [_nn(qc[h], _mx(c_mem[h])) for h in hs]
    s = [qk[h] * jnp.exp(dmat[h] - m_t[h]) for h in hs]
    w_inter = [jnp.exp(inter[h] - m_t[h]) for h in hs]
    num = [_nn(_mx(s[h]), vc[h]) + w_inter[h] * q_c[h] for h in hs]
    for h in hs:
        qn = jnp.sum(q32[h] * n_mem[h], axis=-1, keepdims=True)
        den = jnp.sum(s[h], axis=-1, keepdims=True) + w_inter[h] * qn
        h_out = num[h] * (1.0 / jnp.maximum(jnp.abs(den), jnp.exp(-m_t[h])))
        if has_acc:
            h_out = h_out + acc_ref[0, :, lanes[h]]
        h_ref[0, :, lanes[h]] = h_out
    for h in hs:
        total = bc_col[h][last:last + 1, :]
        g_col = total - bc_col[h] + i_col[h]
        m_new = jnp.maximum(total + m_prev[h], jnp.max(g_col, axis=0, keepdims=True))
        wk = jnp.exp(g_col - m_new)
        dec = jnp.exp(total + m_prev[h] - m_new)
        kw = kc[h] * wk
        c_ref[0, h] = dec * c_mem[h] + _tn(_mx(kw), vc[h])
        n_ref[0, h] = dec * n_mem[h] + jnp.sum(kw, axis=0, keepdims=True)
        m_ref[0, h] = jnp.broadcast_to(m_new, m_ref.shape[2:])


def _mlstm_scan(pq, pk, pv, conv_w, log_i, log_f, state, rev, acc=None, qk=None):
    b, l, _ = pq[0].shape
    w = conv_w.shape[-1]
    heads = ML_HEADS
    hd = w // heads
    t = min(ML_CHUNK, l)
    assert l % t == 0
    nc = l // t
    nb8 = l // 8
    lf = log_f.reshape(b, heads, nc, t)
    bcum = jnp.flip(jnp.cumsum(jnp.flip(lf, 3), axis=3), 3) if rev else jnp.cumsum(lf, axis=3)
    rows = jnp.concatenate([bcum, log_i.reshape(b, heads, nc, t)], axis=1)
    grow = rows.transpose(0, 2, 1, 3)
    gcol = rows.reshape(b, 2 * heads, l).transpose(0, 2, 1)
    ch = (lambda c: nc - 1 - c) if rev else (lambda c: c)
    cidx = lambda bi, c: (bi, ch(c), 0)
    ridx = lambda bi, c: (bi, ch(c), 0, 0)
    sidx = lambda bi, c: (bi, 0, 0, 0)
    c0, n0, m0 = state
    seq = pl.BlockSpec((1, t, w), cidx)
    col = lambda x: pl.BlockSpec((1, t, w), lambda bi, c: (bi, ch(c), x[1] // w))
    prv = lambda x: pl.BlockSpec((1, 8, w), lambda bi, c: (bi, jnp.maximum(ch(c) * (t // 8) - 1, 0), x[1] // w))
    nxt = lambda x: pl.BlockSpec((1, 8, w), lambda bi, c: (bi, jnp.minimum((ch(c) + 1) * (t // 8), nb8 - 1), x[1] // w))
    st_specs = [pl.BlockSpec((1, heads, hd, hd), sidx), pl.BlockSpec((1, heads, 1, hd), sidx),
                pl.BlockSpec((1, heads, 8, 128), sidx)]
    conv = qk is None
    if conv:
        in_specs = [col(pq), prv(pq), nxt(pq), col(pk), prv(pk), nxt(pk), col(pv),
                    pl.BlockSpec((6, w), lambda bi, c: (0, 0))]
        args = [pq[0], pq[0], pq[0], pk[0], pk[0], pk[0], pv[0], conv_w.reshape(6, w)]
    else:
        in_specs = [seq, seq, col(pv)]
        args = [qk[0], qk[1], pv[0]]
    in_specs += [pl.BlockSpec((1, t, 2 * heads), cidx), pl.BlockSpec((1, 1, 2 * heads, t), ridx)] + st_specs
    args += [gcol, grow, c0, n0, m0]
    if acc is not None:
        in_specs.append(seq)
        args.append(acc)
    seq_out = jax.ShapeDtypeStruct((b, l, w), F32)
    out = pl.pallas_call(
        functools.partial(_mlstm_kernel, rev=rev, heads=heads, hd=hd, nc=nc, has_acc=acc is not None, conv=conv),
        grid=(b, nc),
        in_specs=in_specs,
        out_specs=[seq] + st_specs + ([seq, seq] if conv else []),
        out_shape=[seq_out, jax.ShapeDtypeStruct(c0.shape, F32), jax.ShapeDtypeStruct(n0.shape, F32),
                   jax.ShapeDtypeStruct(m0.shape, F32)] + ([seq_out, seq_out] if conv else []),
        compiler_params=_params(("parallel", "arbitrary")),
        name="mlstm_rev" if rev else "mlstm_fwd",
    )(*args)
    return out[0], (out[1], out[2], out[3]), ((out[4], out[5]) if conv else qk)


def _rwkv_core(rows, s_ref, rev):
    c, w = rows[0][0].shape
    g_w = RK_GROUP
    nh = g_w // RK_HD
    assert c == RK_HD and c & (c - 1) == 0
    sh = c.bit_length() - 1
    rj = lax.broadcasted_iota(jnp.int32, (g_w, g_w), 0)
    cj = lax.broadcasted_iota(jnp.int32, (g_w, g_w), 1)
    blk = (rj >> sh) == (cj >> sh)
    tt, ss = rj & (c - 1), cj & (c - 1)
    incl = blk & ((ss >= tt) if rev else (ss <= tt))
    strict = blk & ((ss > tt) if rev else (ss < tt))
    eye = (rj == cj).astype(F32)
    tri = incl[:c, :c].astype(F32).astype(MXU_DT)
    lo_half = lax.broadcasted_iota(jnp.int32, (2 * g_w, 2 * c), 1) < c
    last = 0 if rev else c - 1
    zero = jnp.zeros((), F32)
    chains = [(bi, g) for bi in range(len(rows)) for g in range(w // g_w)]

    def bd(x):
        return _mx(jnp.where(blk, jnp.concatenate([x] * nh, axis=0), zero))

    def rsum(z):
        out = z[0:c]
        for i in range(1, nh):
            out = out + z[i * c:(i + 1) * c]
        return out

    a_bd, r_bd, v_bd, bk_t, r_t, v_g, bk_hat, dec = {}, {}, {}, {}, {}, {}, {}, {}
    for bi, (r, lw, kd, v, kk, a) in enumerate(rows):
        ka = kk * a
        gi = _cumsum_mm(tri, lw)
        tot = gi[last:last + 1, :]
        e_neg = jnp.exp(-gi)
        e_tail = jnp.exp(tot - gi)
        full = {'a': -kk * jnp.exp(gi - lw), 'b': ka * e_neg, 'k': kd * e_neg, 'r': r * jnp.exp(gi),
                'bh': ka * e_tail, 'kh': kd * e_tail, 'dec': jnp.exp(tot), 'v': v}
        for g in range(w // g_w):
            x = {n: t[:, g * g_w:(g + 1) * g_w] for n, t in full.items()}
            ch = (bi, g)
            a_bd[ch], r_bd[ch], v_bd[ch] = bd(x['a']), bd(x['r']), bd(x['v'])
            bk_t[ch] = _mx(jnp.concatenate([x['b'], x['k']], axis=0))
            bk_hat[ch] = _mx(jnp.concatenate([x['bh'], x['kh']], axis=0))
            r_t[ch], v_g[ch], dec[ch] = _mx(x['r']), x['v'], x['dec']

    l_ab, a_ak, a_rb, a_rk = {}, {}, {}, {}
    for ch in chains:
        prod = _nt(jnp.concatenate([a_bd[ch], r_bd[ch]], axis=0), bk_t[ch])
        swp = pltpu.roll(prod, c, 1)
        xb = jnp.where(lo_half, prod, swp)
        xk = jnp.where(lo_half, swp, prod)
        xb = jnp.concatenate([xb, xb], axis=1)
        xk = jnp.concatenate([xk, xk], axis=1)
        l_ab[ch] = jnp.where(strict, xb[:g_w], zero)
        a_ak[ch] = _mx(jnp.where(strict, xk[:g_w], zero))
        a_rb[ch] = _mx(rsum(jnp.where(incl, xb[g_w:], zero)))
        a_rk[ch] = _mx(rsum(jnp.where(incl, xk[g_w:], zero)))

    def live(z, k):
        return z if k == 0 else jnp.concatenate(
            [z[h * c + (0 if rev else k):h * c + (c - k if rev else c)] for h in range(nh)], axis=0)

    def spread(z, k):
        if k == 0:
            return z
        zeros = jnp.zeros((k, z.shape[1]), z.dtype)
        parts = [z[h * (c - k):(h + 1) * (c - k)] for h in range(nh)]
        return jnp.concatenate([x for p_ in parts for x in ((p_, zeros) if rev else (zeros, p_))], axis=0)

    xs = {ch: _mx(l_ab[ch]) for ch in chains}
    ps = {ch: eye + l_ab[ch] for ch in chains}
    for i in range(max(1, (c - 1).bit_length() - 1)):
        k = (2 ** (i + 1)) // 16 * 16
        xs = {ch: _mx(spread(_nn(live(xs[ch], k), xs[ch]), k)) for ch in chains}
        ps = {ch: ps[ch] + spread(_nn(live(_mx(ps[ch]), k), xs[ch]), k) for ch in chains}

    t_cat = {ch: _mx(rsum(ps[ch])) for ch in chains}
    w_cat = {ch: _mx(_nn(t_cat[ch], a_bd[ch])) for ch in chains}
    ta = {ch: _mx(_nn(t_cat[ch], a_ak[ch])) for ch in chains}
    u0 = {ch: _nn(ta[ch], v_bd[ch]) for ch in chains}
    y_k = {ch: _nn(a_rk[ch], v_bd[ch]) for ch in chains}
    s_mem = {ch: s_ref[ch[0], ch[1]] for ch in chains}
    s_b = {ch: _mx(s_mem[ch]) for ch in chains}
    u = {ch: _nt(w_cat[ch], s_b[ch]) + u0[ch] for ch in chains}
    y = {ch: _nt(r_t[ch], s_b[ch]) + _nn(a_rb[ch], bd(u[ch])) + y_k[ch] for ch in chains}
    for ch in chains:
        upd = _tn(_mx(jnp.concatenate([u[ch], v_g[ch]], axis=0)), bk_hat[ch])
        s_ref[ch[0], ch[1]] = s_mem[ch] * dec[ch] + jnp.where(blk, upd, zero)
    return [jnp.concatenate([y[(bi, g)] for g in range(w // g_w)], axis=1) for bi in range(len(rows))]


def _rwkv_seq_kernel(*refs, rev, has_acc):
    r_ref, lw_ref, kd_ref, v_ref, kk_ref, a_ref, s0_ref = refs[:7]
    acc_ref = refs[7] if has_acc else None
    y_ref, s_ref = refs[7 + has_acc:]

    @pl.when(pl.program_id(1) == 0)
    def _():
        s_ref[...] = s0_ref[...]

    y, = _rwkv_core([(r_ref[0], lw_ref[0], kd_ref[0], v_ref[0], kk_ref[0], a_ref[0])], s_ref, rev)
    if has_acc:
        y = y + acc_ref[0]
    y_ref[0] = y


def _rwkv_scan_seq(r, lw, kd, v, kk, a, state, rev, acc=None):
    b, l, w = r.shape
    c = RK_CHUNK
    nc = l // c
    cidx = (lambda bi, ch: (bi, nc - 1 - ch, 0)) if rev else (lambda bi, ch: (bi, ch, 0))
    seq = pl.BlockSpec((1, c, w), cidx)
    st = pl.BlockSpec((1,) + state.shape[1:], lambda bi, ch: (bi, 0, 0, 0))
    args = [r, lw, kd, v, kk, a, state] + ([acc] if acc is not None else [])
    y, s = pl.pallas_call(
        functools.partial(_rwkv_seq_kernel, rev=rev, has_acc=acc is not None),
        grid=(b, nc),
        in_specs=[seq] * 6 + [st] + ([seq] if acc is not None else []),
        out_specs=[seq, st],
        out_shape=[jax.ShapeDtypeStruct((b, l, w), F32), jax.ShapeDtypeStruct(state.shape, F32)],
        compiler_params=_params(("parallel", "arbitrary")),
        name="rwkv_seq_rev" if rev else "rwkv_seq_fwd",
    )(*args)
    return y, s


_PV_ROWS = ('mu_r', 'mu_k', 'mu_v', 'w0', 'a0', 'kk', 'ka', 'rk', 'v0')


def _rwkv_grid_kernel(*refs, rev, nc, has_vf, has_acc, emit_vf):
    it = iter(refs)
    pr, pr_u, pr_d, pk, pk_u, pk_d, pv, pv_u, pv_d, wa, wa_u, wa_d = (next(it) for _ in range(12))
    vf_ref, lora_ref = (next(it), next(it)) if has_vf else (None, None)
    pvec, mu_wa, w2_ref, a2_ref, s0_ref = (next(it) for _ in range(5))
    accy_ref, accb_ref = (next(it), next(it)) if has_acc else (None, None)
    y_ref, bon_ref, s_ref = (next(it) for _ in range(3))
    vfo_ref = next(it) if emit_vf else None
    step = pl.program_id(1)

    @pl.when(step == 0)
    def _():
        s_ref[...] = s0_ref[...]

    c = pr.shape[1]
    w = pr.shape[2]
    rd = w // 16
    ci = nc - 1 - step if rev else step
    has_up = jnp.where(ci > 0, 1.0, 0.0)
    has_dn = jnp.where(ci < nc - 1, 1.0, 0.0)
    trow = lax.broadcasted_iota(jnp.int32, (c, 1), 0)
    prow = lambda name: pvec[_PV_ROWS.index(name):_PV_ROWS.index(name) + 1, :]

    def left(x):
        return jnp.where(trow == 0, 0.0, pltpu.roll(x, 1, 0))

    def right(x):
        return jnp.where(trow == c - 1, 0.0, pltpu.roll(x, c - 1, 0))

    ones = _seg_ones(RK_HD)
    d = 1 if rev else 0
    rows, bons = [], []
    for bi in range(pr.shape[0]):
        def lerp_wide(cur_ref, up_ref, dn_ref, mu):
            cur = cur_ref[bi]
            q = w // 4
            sh = jnp.concatenate([left(cur[:, :q]), right(cur[:, q:2 * q]),
                                  up_ref[bi] * has_up, dn_ref[bi] * has_dn], axis=1)
            return cur + mu * (sh - cur)

        r = lerp_wide(pr, pr_u, pr_d, prow('mu_r'))
        k = lerp_wide(pk, pk_u, pk_d, prow('mu_k'))
        v = lerp_wide(pv, pv_u, pv_d, prow('mu_v'))

        cur = wa[bi]
        qd = (lax.broadcasted_iota(jnp.int32, cur.shape, 1) & (rd - 1)) >> ((rd // 4).bit_length() - 1)
        sh = jnp.where(qd == 0, left(cur), jnp.where(qd == 1, right(cur),
                                                     jnp.where(qd == 2, wa_u[bi] * has_up, wa_d[bi] * has_dn)))
        xwa = cur + mu_wa[...] * (sh - cur)
        xw = xwa[:, d * rd:(d + 1) * rd]
        xa = xwa[:, (2 + d) * rd:(3 + d) * rd]
        z = -(prow('w0') + _nn(_mx(jnp.tanh(xw)), w2_ref[...]))
        softplus = jnp.maximum(z, 0.0) + jnp.log(1.0 + jnp.exp(-jnp.abs(z)))
        lw = -jnp.exp(-softplus - 0.5)
        a = _sigmoid(prow('a0') + _nn(_mx(xa), a2_ref[...]))
        if has_vf:
            v = v + (vf_ref[bi] - v) * _sigmoid(prow('v0') + lora_ref[bi])
        if emit_vf:
            vfo_ref[bi] = v
        kkn = k * prow('kk')
        kk = kkn * lax.rsqrt(jnp.maximum(_segsum(kkn * kkn, ones), 1e-24))
        kd = k * (1.0 + (a - 1.0) * prow('ka'))
        bons.append(_segsum(r * kd * prow('rk'), ones) * v)
        rows.append((r, lw, kd, v, kk, a))
    ys = _rwkv_core(rows, s_ref, rev)
    for bi, (y, bon) in enumerate(zip(ys, bons)):
        if has_acc:
            y = y + accy_ref[bi]
            bon = bon + accb_ref[bi]
        y_ref[bi] = y
        bon_ref[bi] = bon


def _rwkv_scan_grid(pr, pk, pv, wa, pvec, mu_wa, w2, a2, state, rev, vf=None, lora=None,
                    acc=None, emit_vf=False):
    b, l, _ = pr[0].shape
    w = pvec.shape[1]
    c = RK_CHUNK
    assert c == GRID_W
    nc = l // c
    wq = w // 4
    ch = (lambda s: nc - 1 - s) if rev else (lambda s: s)
    cur = lambda lane: (lambda bi, s: (bi, ch(s), lane))
    upi = lambda lane: (lambda bi, s: (bi, jnp.maximum(ch(s) - 1, 0), lane))
    dni = lambda lane: (lambda bi, s: (bi, jnp.minimum(ch(s) + 1, nc - 1), lane))
    bb = RK_BATCH if b % RK_BATCH == 0 else 1
    seq = pl.BlockSpec((bb, c, w), cur(0))
    wide = lambda x: [pl.BlockSpec((bb, c, w), cur(x[1] // w)), pl.BlockSpec((bb, c, wq), upi(x[1] // wq + 2)),
                      pl.BlockSpec((bb, c, wq), dni(x[1] // wq + 3))]
    wa_w = mu_wa.shape[1]
    small = [pl.BlockSpec((bb, c, wa_w), cur(wa[1] // wa_w)), pl.BlockSpec((bb, c, wa_w), upi(wa[1] // wa_w)),
             pl.BlockSpec((bb, c, wa_w), dni(wa[1] // wa_w))]
    full = lambda x: pl.BlockSpec(x.shape, lambda bi, s: (0,) * x.ndim)
    st = pl.BlockSpec((bb,) + state.shape[1:], lambda bi, s: (bi, 0, 0, 0))
    in_specs = wide(pr) + wide(pk) + wide(pv) + small
    args = [pr[0]] * 3 + [pk[0]] * 3 + [pv[0]] * 3 + [wa[0]] * 3
    if vf is not None:
        in_specs += [seq, seq]
        args += [vf, lora]
    in_specs += [full(pvec), full(mu_wa), full(w2), full(a2), st]
    args += [pvec, mu_wa, w2, a2, state]
    if acc is not None:
        in_specs += [seq, seq]
        args += list(acc)
    out_specs = [seq, seq, st]
    out_shape = [jax.ShapeDtypeStruct((b, l, w), F32), jax.ShapeDtypeStruct((b, l, w), F32),
                 jax.ShapeDtypeStruct(state.shape, F32)]
    if emit_vf:
        out_specs.append(seq)
        out_shape.append(jax.ShapeDtypeStruct((b, l, w), F32))
    return pl.pallas_call(
        functools.partial(_rwkv_grid_kernel, rev=rev, nc=nc, has_vf=vf is not None,
                          has_acc=acc is not None, emit_vf=emit_vf),
        grid=(b // bb, nc),
        in_specs=in_specs,
        out_specs=out_specs,
        out_shape=out_shape,
        compiler_params=_params(("parallel", "arbitrary")),
        name="rwkv_grid_rev" if rev else "rwkv_grid_fwd",
    )(*args)


def _hgrn_kernel(*refs, rev, heads, hd, has_acc):
    q_ref, f_ref, v_ref, fb_ref, lb_ref, s0_ref = refs[:6]
    acc_ref = refs[6] if has_acc else None
    o_ref, s_ref = refs[6 + has_acc:]

    @pl.when(pl.program_id(1) == 0)
    def _():
        s_ref[...] = s0_ref[...]

    tt = q_ref.shape[1]
    sub = HG_SUB
    sh = sub.bit_length() - 1
    nsub = tt // sub
    mid = sub // 2 if rev else sub // 2 - 1
    last = 0 if rev else sub - 1
    ri = lax.broadcasted_iota(jnp.int32, (tt, tt), 0)
    ci = lax.broadcasted_iota(jnp.int32, (tt, tt), 1)
    causal = ((ri >> sh) == (ci >> sh)) & ((ci >= ri) if rev else (ci <= ri))
    lb = lb_ref[...]
    f = lb + (1.0 - lb) * _sigmoid(f_ref[0] + fb_ref[...])
    g_all = _cumsum_mm(causal.astype(F32).astype(MXU_DT), jnp.log(f))

    def row_of_each_sub(r):
        return jnp.concatenate([jnp.broadcast_to(g_all[j * sub + r:j * sub + r + 1], (sub, g_all.shape[1]))
                                for j in range(nsub)], axis=0)

    g_mid, g_last = row_of_each_sub(mid), row_of_each_sub(last)
    k_all = 1.0 - f
    q_all = _silu(q_ref[0])
    q1 = _mx(q_all * jnp.exp(g_all - g_mid))
    k1 = _mx(k_all * jnp.exp(g_mid - g_all))
    qg_f = q_all * jnp.exp(g_all)
    kl_f = k_all * jnp.exp(g_last - g_all)
    v_all = _mx(v_ref[0])
    assert nsub % 2 == 0
    is_first = lambda j: (j % 2 == 1) == rev
    dec = [jnp.exp(g_all[j * sub + last:j * sub + last + 1]) for j in range(nsub)]
    one_row = jnp.ones_like(dec[0])
    per_sub = lambda vals: jnp.concatenate([jnp.broadcast_to(x, (sub, x.shape[1])) for x in vals], axis=0)
    qg, kl = _mx(qg_f), _mx(kl_f)
    qg2 = _mx(qg_f * per_sub([one_row if is_first(j) else dec[j ^ 1] for j in range(nsub)]))
    kl2 = _mx(kl_f * per_sub([dec[j ^ 1] if is_first(j) else one_row for j in range(nsub)]))
    sub_r, sub_c = ri >> sh, ci >> sh
    second_sees_first = ((sub_r ^ 1) == sub_c) & ((sub_r & 1) == (0 if rev else 1))
    lanes = [slice(h * hd, (h + 1) * hd) for h in range(heads)]
    zero = jnp.zeros((), F32)
    att = [_mx(jnp.where(causal, _nt(q1[:, sl], k1[:, sl]), zero)
               + jnp.where(second_sees_first, _nt(qg[:, sl], kl[:, sl]), zero)) for sl in lanes]
    intra = [_nn(att[h], v_all[:, lanes[h]]) for h in range(heads)]
    pairs = [nsub // 2 - 1 - s if rev else s for s in range(nsub // 2)]
    rows = [slice(2 * p * sub, 2 * (p + 1) * sub) for p in range(nsub // 2)]
    upd = {(p, h): _tn(v_all[rows[p], lanes[h]], kl2[rows[p], lanes[h]]) for p in pairs for h in range(heads)}
    s_mem = [s_ref[0, h] for h in range(heads)]
    for p in pairs:
        dec_pair = dec[2 * p] * dec[2 * p + 1]
        outs = []
        for h in range(heads):
            outs.append(intra[h][rows[p]] + _nt(qg2[rows[p], lanes[h]], _mx(s_mem[h])))
            s_mem[h] = s_mem[h] * dec_pair[:, lanes[h]] + upd[(p, h)]
        o = jnp.concatenate(outs, axis=1)
        if has_acc:
            o = o + acc_ref[0, rows[p], :]
        o_ref[0, rows[p], :] = o
    for h in range(heads):
        s_ref[0, h] = s_mem[h]


def _hgrn_scan(pq, pf, pi, f_b, lb, state, rev, acc=None):
    b, l, _ = pq[0].shape
    w = f_b.shape[1]
    heads = HG_HEADS
    hd = w // heads
    tt = min(HG_TILE, l)
    nc = l // tt
    d = 1 if rev else 0
    ch = (lambda s: nc - 1 - s) if rev else (lambda s: s)
    col = lambda x, extra=0: pl.BlockSpec((1, tt, w), lambda bi, s: (bi, ch(s), x[1] // w + extra))
    seq = pl.BlockSpec((1, tt, w), lambda bi, s: (bi, ch(s), 0))
    vec = pl.BlockSpec((1, w), lambda bi, s: (0, 0))
    st = pl.BlockSpec((1,) + state.shape[1:], lambda bi, s: (bi, 0, 0, 0))
    args = [pq[0], pf[0], pi[0], f_b, lb, state] + ([acc] if acc is not None else [])
    o, s = pl.pallas_call(
        functools.partial(_hgrn_kernel, rev=rev, heads=heads, hd=hd, has_acc=acc is not None),
        grid=(b, nc),
        in_specs=[col(pq), col(pf, d), col(pi), vec, vec, st] + ([seq] if acc is not None else []),
        out_specs=[seq, st],
        out_shape=[jax.ShapeDtypeStruct((b, l, w), F32), jax.ShapeDtypeStruct(state.shape, F32)],
        compiler_params=_params(("parallel", "arbitrary")),
        name="hgrn_rev" if rev else "hgrn_fwd",
    )(*args)
    return o, s


_EV_ROWS = ('ml_norm_w', 'rk_ln_w', 'rk_ln_b', 'hg_norm_w', 'gate_b0', 'gate_b1', 'gate_b2', 'final_w', 'next_w')


def _epilogue_kernel(hm_ref, mo_ref, mz_ref, yr_ref, bon_ref, rz_ref, oh_ref, hz_ref,
                     g0_ref, g1_ref, g2_ref, xs_ref, gx_ref, ev_ref,
                     wpm_ref, wpr_ref, wph_ref, wout_ref, *rest, final, has_next):
    o_ref = rest[2 * has_next]
    ev = lambda name: ev_ref[_EV_ROWS.index(name):_EV_ROWS.index(name) + 1, :]
    d = hm_ref.shape[2]

    def seg_mean(x, seg):
        if seg % 128:
            return _segsum(x, _seg_ones(seg)) * (1.0 / seg)
        parts = [jnp.broadcast_to(jnp.mean(x[:, i:i + seg], axis=-1, keepdims=True), (x.shape[0], seg))
                 for i in range(0, d, seg)]
        return jnp.concatenate(parts, axis=1)

    seg = d // ML_HEADS
    y = hm_ref[0]
    y = y - seg_mean(y, seg)
    y = y * lax.rsqrt(seg_mean(y * y, seg) + NORM_EPS) * ev('ml_norm_w')
    u_m = _sigmoid(mo_ref[0].astype(F32)) * y * _silu(mz_ref[0].astype(F32))
    y = yr_ref[0]
    y = y - seg_mean(y, RK_HD)
    y = y * lax.rsqrt(seg_mean(y * y, RK_HD) + RK_GN_EPS) * ev('rk_ln_w') + ev('rk_ln_b') + bon_ref[0]
    u_r = y * _silu(rz_ref[0].astype(F32))
    seg = d // HG_HEADS
    o = oh_ref[0]
    o = o * lax.rsqrt(seg_mean(o * o, seg) + NORM_EPS) * ev('hg_norm_w')
    u_h = o * _silu(hz_ref[0].astype(F32))

    merged = (_sigmoid(g0_ref[0].astype(F32) + ev('gate_b0')) * _nn(_mx(u_m), wpm_ref[...])
              + _sigmoid(g1_ref[0].astype(F32) + ev('gate_b1')) * _nn(_mx(u_r), wpr_ref[...])
              + _sigmoid(g2_ref[0].astype(F32) + ev('gate_b2')) * _nn(_mx(u_h), wph_ref[...]))
    xs = xs_ref[0] + gx_ref[0] * _nn(_mx(merged), wout_ref[...])
    if final:
        xs = xs * lax.rsqrt(jnp.mean(xs * xs, axis=-1, keepdims=True) + NORM_EPS) * ev('final_w')
    o_ref[0] = xs
    if has_next:
        normed = xs * lax.rsqrt(jnp.mean(xs * xs, axis=-1, keepdims=True) + NORM_EPS) * ev('next_w')
        rest[3][0] = _mx(normed * rest[0][0] + rest[1][0])


def _epilogue(hm, mo, mz, yr, bon, rz, oh, hz, gate, xs, gx, p, final_w, nxt):
    b, l, d = xs.shape
    tt = min(EP_TILE, l)
    seq = pl.BlockSpec((1, tt, d), lambda bi, i: (bi, i, 0))
    col = lambda x, extra=0: pl.BlockSpec((1, tt, d), lambda bi, i: (bi, i, x[1] // d + extra))
    ev = jnp.stack([p['ml_norm_w'], p['rk_ln_w'], p['rk_ln_b'], p['hg_norm_w'],
                    p['gate_b'][0], p['gate_b'][1], p['gate_b'][2],
                    final_w if final_w is not None else jnp.ones((d,), F32),
                    nxt[0] if nxt is not None else jnp.ones((d,), F32)], axis=0)
    wspec = pl.BlockSpec((d, d), lambda bi, i: (0, 0))
    ws = [_mx(p[n]) for n in ('w_pm', 'w_pr', 'w_ph', 'w_out')]
    per_batch = pl.BlockSpec((1, 1, d), lambda bi, i: (bi, 0, 0))
    has_next = nxt is not None
    out = pl.pallas_call(
        functools.partial(_epilogue_kernel, final=final_w is not None, has_next=has_next),
        grid=(b, l // tt),
        in_specs=[seq, col(mo), col(mz), seq, seq, col(rz), seq, col(hz), col(gate), col(gate, 1), col(gate, 2), seq,
                  per_batch, pl.BlockSpec(ev.shape, lambda bi, i: (0, 0))] + [wspec] * 4 + [per_batch] * (2 * has_next),
        out_specs=[seq] * (1 + has_next),
        out_shape=[jax.ShapeDtypeStruct((b, l, d), F32)] + [jax.ShapeDtypeStruct((b, l, d), MXU_DT)] * has_next,
        compiler_params=_params(("parallel", "parallel")),
        name="epilogue",
    )(hm, mo[0], mz[0], yr, bon, rz[0], oh, hz[0], gate[0], gate[0], gate[0], xs, gx, ev, *ws,
      *(nxt[1:] if has_next else ()))
    return out[0], (out[1] if has_next else None)


def _rms_norm(x, w):
    return x * lax.rsqrt(jnp.mean(x * x, axis=-1, keepdims=True) + NORM_EPS) * w


def _bi_shift_seq(u):
    half = u.shape[-1] // 2
    g = jnp.pad(u, ((0, 0), (1, 1), (0, 0)))
    return jnp.concatenate((g[:, :-2, :half], g[:, 2:, half:]), axis=-1)


def _in_layout(d):
    rd = d // 16
    return (('m_q', d), ('m_k', d), ('m_v', d), ('m_o', d), ('m_z', d), ('m_if', 4 * ML_HEADS),
            ('r_r', d), ('r_k', d), ('r_v', d), ('r_z', d), ('r_wa', 4 * rd),
            ('h_q', d), ('h_f', 2 * d), ('h_i', d), ('h_z', d), ('gate', 3 * d))


_COL_GROUPS = ((False, ('m_q', 'm_k', 'r_r', 'r_k', 'r_v', 'h_q', 'h_f')),
               (False, ('r_wa', 'm_if')),
               (True, ('m_v', 'h_i', 'm_o', 'm_z', 'r_z', 'h_z', 'gate')))
_COL_ALIGN = 512


def _pack_w_in(w_in):
    d = w_in.shape[0]
    src, off = {}, 0
    for name, width in _in_layout(d):
        src[name] = (off, width)
        off += width

    def pack(names):
        cols, first, o = [], {}, 0
        for n in names:
            cols.append(w_in[:, src[n][0]:src[n][0] + src[n][1]])
            first[n] = o
            o += src[n][1]
        if o % _COL_ALIGN:
            cols.append(jnp.zeros((d, -o % _COL_ALIGN), w_in.dtype))
        return jnp.concatenate(cols, axis=1).astype(MXU_DT), first

    return [pack(names) for _, names in _COL_GROUPS]


def _take(col, width):
    return col[0][..., col[1]:col[1] + width]


def _zero_states(b, d):
    ml_hd = d // ML_HEADS
    ml = (jnp.zeros((b, ML_HEADS, ml_hd, ml_hd), F32), jnp.zeros((b, ML_HEADS, 1, ml_hd), F32),
          jnp.zeros((b, ML_HEADS, 8, 128), F32))
    rk = jnp.zeros((b, d // RK_GROUP, RK_GROUP, RK_GROUP), F32)
    hg_hd = d // HG_HEADS
    hg = jnp.zeros((b, HG_HEADS, hg_hd, hg_hd), F32)
    return ((ml, ml), (rk, rk), (hg, hg))


def _mlstm_branch(proj, p, init, b, l):
    gl = _take(proj('m_if'), 4 * ML_HEADS) + p['ml_if_b'].reshape(-1)
    gl = gl.transpose(0, 2, 1).reshape(b, 2, 2, ML_HEADS, l)
    acc, qk, states = None, None, []
    for dr in (0, 1):
        log_i = gl[:, dr, 0]
        log_f = jax.nn.log_sigmoid(gl[:, dr, 1])
        acc, st, qk = _mlstm_scan(proj('m_q'), proj('m_k'), proj('m_v'), p['ml_conv'], log_i, log_f,
                                  init[dr], rev=bool(dr), acc=acc, qk=qk)
        states.append(st)
    return acc, tuple(states)


def _rwkv7_seq_branch(proj, h2, p, init, v_first, b, l, d):
    nh, n = d // RK_HD, RK_HD
    rd = d // 16
    mu = p['rk_mu']

    def lerp_shift(u, m):
        return u + m * (_bi_shift_seq(u) - u)

    r = lerp_shift(_take(proj('r_r'), d), mu[0:d])
    k = lerp_shift(_take(proj('r_k'), d), mu[d:2 * d])
    v = lerp_shift(_take(proj('r_v'), d), mu[2 * d:3 * d])
    pwa = _take(proj('r_wa'), 4 * rd)
    xwa = jnp.concatenate([lerp_shift(pwa[..., i * rd:(i + 1) * rd], mu[3 * d + i * rd:3 * d + (i + 1) * rd])
                           for i in range(4)], axis=-1)
    if p['rk_v0'] is None:
        v_first = v
    else:
        lora = _mm(_mm(h2, p['rk_v1']), p['rk_v2']).reshape(b, l, d)
        v = v + (v_first - v) * jax.nn.sigmoid(p['rk_v0'] + lora)
    kk = (k * p['rk_kk']).reshape(b, l, nh, n)
    kk = kk / jnp.maximum(jnp.sqrt(jnp.sum(kk * kk, axis=-1, keepdims=True)), 1e-12)
    kk = kk.reshape(b, l, d)
    acc, states, kd_sum = None, [], 0.0
    for dr in (0, 1):
        xw = xwa[..., dr * rd:(dr + 1) * rd]
        xa = xwa[..., (2 + dr) * rd:(3 + dr) * rd]
        log_w = -jax.nn.softplus(-(p['rk_w0'][dr] + _mm(jnp.tanh(xw).reshape(b * l, rd), p['rk_w2'][dr]).reshape(b, l, d))) - 0.5
        lw = -jnp.exp(log_w)
        a = jax.nn.sigmoid(p['rk_a0'][dr] + _mm(xa.reshape(b * l, rd), p['rk_a2'][dr]).reshape(b, l, d))
        kd = k * (1.0 + (a - 1.0) * p['rk_ka'])
        acc, s = _rwkv_scan_seq(r, lw, kd, v, kk, a, init[dr], rev=bool(dr), acc=acc)
        states.append(s)
        kd_sum = kd_sum + kd
    bonus = jnp.sum((r * kd_sum * p['rk_rk']).reshape(b, l, nh, n), axis=-1, keepdims=True) * v.reshape(b, l, nh, n)
    return acc, bonus.reshape(b, l, d), tuple(states), v_first


def _rwkv7_grid_branch(proj, h2, p, init, v_first, b, l, d):
    mu = p['rk_mu']
    zeros = jnp.zeros((d,), F32)
    has_vf = p['rk_v0'] is not None
    lora = _mm(_mm(h2, p['rk_v1']), p['rk_v2']).reshape(b, l, d) if has_vf else None
    acc, states, vf_out = None, [], v_first
    for dr in (0, 1):
        rows = {'mu_r': mu[0:d], 'mu_k': mu[d:2 * d], 'mu_v': mu[2 * d:3 * d], 'w0': p['rk_w0'][dr],
                'a0': p['rk_a0'][dr], 'kk': p['rk_kk'], 'ka': p['rk_ka'], 'rk': p['rk_rk'],
                'v0': p['rk_v0'] if has_vf else zeros}
        pvec = jnp.stack([rows[n] for n in _PV_ROWS], axis=0)
        emit_vf = (not has_vf) and dr == 0
        out = _rwkv_scan_grid(proj('r_r'), proj('r_k'), proj('r_v'), proj('r_wa'), pvec, mu[3 * d:][None],
                              _mx(p['rk_w2'][dr]), _mx(p['rk_a2'][dr]), init[dr], rev=bool(dr),
                              vf=v_first if has_vf else None, lora=lora, acc=acc, emit_vf=emit_vf)
        acc = (out[0], out[1])
        states.append(out[2])
        if emit_vf:
            vf_out = out[3]
    return acc[0], acc[1], tuple(states), vf_out


def _hgrn2_branch(proj, p, init):
    acc, states = None, []
    for dr in (0, 1):
        acc, s = _hgrn_scan(proj('h_q'), proj('h_f'), proj('h_i'), p['hg_f_b'][dr][None], p['hg_lb'][dr][None],
                            init[dr], rev=bool(dr), acc=acc)
        states.append(s)
    return acc, tuple(states)


def _mixer(h, p, init, on_grid, v_first, need_out, xs, gx, final_w, nxt):
    b, l, d = h.shape
    h2 = h.reshape(b * l, d)
    cols = {}
    for (w, first), (store_mx, _) in zip(p['w_in'], _COL_GROUPS):
        out = _mm(h2, w, MXU_DT if store_mx else F32).reshape(b, l, w.shape[1])
        cols.update({name: (out, off) for name, off in first.items()})

    def proj(name):
        return cols[name]

    hm, st_m = _mlstm_branch(proj, p, init[0], b, l)
    if on_grid:
        yr, bon, st_r, v_first = _rwkv7_grid_branch(proj, h2, p, init[1], v_first, b, l, d)
    else:
        yr, bon, st_r, v_first = _rwkv7_seq_branch(proj, h2, p, init[1], v_first, b, l, d)
    oh, st_h = _hgrn2_branch(proj, p, init[2])
    states = (st_m, st_r, st_h)
    if not need_out:
        return None, None, states, v_first
    out, h_next = _epilogue(hm, proj('m_o'), proj('m_z'), yr, bon, proj('r_z'), oh, proj('h_z'), proj('gate'),
                            xs, gx, p, final_w, nxt)
    return out, h_next, states, v_first


def kernel(x, c, ctx, c_ctx, norm_w, ada_w, ada_b, w_in, gate_b, ml_conv, ml_if_b, ml_norm_w, rk_mu, rk_w0, rk_w2, rk_a0, rk_a2, rk_kk, rk_ka, rk_rk, rk_v0, rk_v1, rk_v2, rk_ln_w, rk_ln_b, hg_f_b, hg_lb, hg_norm_w, w_pm, w_pr, w_ph, w_out, final_norm_w):
    batch, _, d = x.shape
    depth = w_in.shape[0]
    lb_p = jax.nn.softmax(hg_lb.astype(F32), axis=1)
    lower_bounds = jnp.cumsum(lb_p, axis=1) - lb_p[:, :1]
    xs, cs = x, ctx
    vf_x, vf_c = None, None
    cond = jnp.concatenate([jax.nn.silu(c), jax.nn.silu(c_ctx)[None]], axis=0)
    mods = []
    for l in range(depth):
        mod = _mm(cond, ada_w[l]) + ada_b[l]
        sh_x, sc_x, g_x = jnp.split(mod[:batch, None, :], 3, axis=-1)
        sh_c, sc_c, g_c = (jnp.broadcast_to(t[None, None, :], (batch, 1, d)) for t in jnp.split(mod[batch], 3, axis=-1))
        mods.append(((sh_x, 1.0 + sc_x, g_x), (sh_c, 1.0 + sc_c, g_c)))
    hx = _mx(_rms_norm(xs, norm_w[0]) * mods[0][0][1] + mods[0][0][0])
    hc = _mx(_rms_norm(cs, norm_w[0]) * mods[0][1][1] + mods[0][1][0])
    for l in range(depth):
        last = l == depth - 1
        p = {'w_in': _pack_w_in(w_in[l]), 'gate_b': gate_b[l], 'ml_conv': ml_conv[l], 'ml_if_b': ml_if_b[l],
             'ml_norm_w': ml_norm_w[l], 'rk_mu': rk_mu[l], 'rk_w0': rk_w0[l], 'rk_w2': rk_w2[l],
             'rk_a0': rk_a0[l], 'rk_a2': rk_a2[l], 'rk_kk': rk_kk[l], 'rk_ka': rk_ka[l], 'rk_rk': rk_rk[l],
             'rk_v0': rk_v0[l - 1] if l > 0 else None, 'rk_v1': rk_v1[l - 1] if l > 0 else None,
             'rk_v2': rk_v2[l - 1] if l > 0 else None, 'rk_ln_w': rk_ln_w[l], 'rk_ln_b': rk_ln_b[l],
             'hg_f_b': hg_f_b[l], 'hg_lb': lower_bounds[:, l], 'hg_norm_w': hg_norm_w[l],
             'w_pm': w_pm[l], 'w_pr': w_pr[l], 'w_ph': w_ph[l], 'w_out': w_out[l]}
        (_, _, gate_x), (_, _, gate_c) = mods[l]
        nxt_x = None if last else (norm_w[l + 1], mods[l + 1][0][1], mods[l + 1][0][0])
        nxt_c = None if last else (norm_w[l + 1], mods[l + 1][1][1], mods[l + 1][1][0])
        cs, hc, st_c, vf_c = _mixer(hc, p, _zero_states(batch, d), False, vf_c, not last, cs, gate_c, None, nxt_c)
        xs, hx, _, vf_x = _mixer(hx, p, st_c, True, vf_x, True, xs, gate_x, final_norm_w if last else None, nxt_x)
    return xs
```

```python
import functools

import jax
import jax.numpy as jnp
from jax import lax
from jax.experimental import pallas as pl
from jax.experimental.pallas import tpu as pltpu

F32 = jnp.float32
MXU_DT = jnp.bfloat16

NORM_EPS = 1e-6
GRID_W = 64
ML_HEADS = 4
ML_CHUNK = 512
RK_HD = 64
RK_CHUNK = 64
RK_GROUP = 256
RK_BATCH = 4
RK_GN_EPS = 64e-5
HG_HEADS = 8
HG_SUB = 16
HG_TILE = 128
EP_TILE = 512
EP_IN_BUFFERS = 2
SEG_W = 256
VMEM_LIMIT = 60 * 1024 * 1024


def _nt(a, b):
    return lax.dot_general(a, b, (((1,), (1,)), ((), ())), preferred_element_type=F32)


def _tn(a, b):
    return lax.dot_general(a, b, (((0,), (0,)), ((), ())), preferred_element_type=F32)


def _nn(a, b):
    return jnp.dot(a, b, preferred_element_type=F32)


def _mx(a):
    return a.astype(MXU_DT)


def _cumsum_mm(tri, x):
    hi = x.astype(MXU_DT)
    r1 = x - hi.astype(F32)
    mid = r1.astype(MXU_DT)
    lo = (r1 - mid.astype(F32)).astype(MXU_DT)
    one = lambda t: _nn(t, hi) + _nn(t, mid) + _nn(t, lo)
    return [one(t) for t in tri] if isinstance(tri, (list, tuple)) else one(tri)


def _seg_ones(seg):
    sh = seg.bit_length() - 1
    ri = lax.broadcasted_iota(jnp.int32, (SEG_W, SEG_W), 0)
    ci = lax.broadcasted_iota(jnp.int32, (SEG_W, SEG_W), 1)
    return ((ri >> sh) == (ci >> sh)).astype(F32).astype(MXU_DT)


def _segsum(x, ones):
    outs = [_nn(_mx(x[:, g * SEG_W:(g + 1) * SEG_W]), ones) for g in range(x.shape[1] // SEG_W)]
    return outs[0] if len(outs) == 1 else jnp.concatenate(outs, axis=1)


def _sigmoid(x):
    return jax.nn.sigmoid(x)


def _silu(x):
    return x * jax.nn.sigmoid(x)


def _params(sem):
    return pltpu.CompilerParams(dimension_semantics=sem, vmem_limit_bytes=VMEM_LIMIT)


def _mm_kernel(x_ref, w_ref, o_ref):
    o_ref[...] = _nn(_mx(x_ref[...]), w_ref[...]).astype(o_ref.dtype)


def _mm(x, w, out_dtype=F32):
    m, k = x.shape
    n = w.shape[1]
    mp = -(-m // 8) * 8
    npad = -(-n // 128) * 128
    if mp != m:
        x = jnp.pad(x, ((0, mp - m), (0, 0)))
    wb = w.astype(MXU_DT)
    if npad != n:
        wb = jnp.pad(wb, ((0, 0), (0, npad - n)))
    tm = next(t for t in (2048, 1024, 512, 256, 128, 64, 32, 16, 8) if mp % t == 0)
    tn = next(t for t in (1024, 512, 256, 128) if npad % t == 0)
    out = pl.pallas_call(
        _mm_kernel,
        grid=(mp // tm, npad // tn),
        in_specs=[pl.BlockSpec((tm, k), lambda i, j: (i, 0)),
                  pl.BlockSpec((k, tn), lambda i, j: (0, j))],
        out_specs=pl.BlockSpec((tm, tn), lambda i, j: (i, j)),
        out_shape=jax.ShapeDtypeStruct((mp, npad), out_dtype),
        compiler_params=_params(("parallel", "parallel")),
        name="mm",
    )(x, wb)
    if mp != m or npad != n:
        out = out[:m, :n]
    return out


def _mlstm_kernel(*refs, rev, heads, hd, nc, has_acc, conv):
    it = iter(refs)
    if conv:
        q_ref, qp_ref, qn_ref, k_ref, kp_ref, kn_ref, v_ref, cw_ref = (next(it) for _ in range(8))
    else:
        q_ref, k_ref, v_ref = (next(it) for _ in range(3))
    gcol_ref, grow_ref, c0_ref, n0_ref, m0_ref = (next(it) for _ in range(5))
    acc_ref = next(it) if has_acc else None
    h_ref, c_ref, n_ref, m_ref = (next(it) for _ in range(4))
    step = pl.program_id(1)

    @pl.when(step == 0)
    def _():
        c_ref[...] = c0_ref[...]
        n_ref[...] = n0_ref[...]
        m_ref[...] = m0_ref[...]

    t = q_ref.shape[1]
    ci = nc - 1 - step if rev else step
    has_prev = jnp.where(ci > 0, 1.0, 0.0)
    has_next = jnp.where(ci < nc - 1, 1.0, 0.0)
    trow = lax.broadcasted_iota(jnp.int32, (t, 1), 0)

    def conv_silu(u, prev_blk, next_blk, w3):
        up = jnp.where(trow == 0, prev_blk[7:8, :] * has_prev, pltpu.roll(u, 1, 0))
        dn = jnp.where(trow == t - 1, next_blk[0:1, :] * has_next, pltpu.roll(u, t - 1, 0))
        return _silu(up * w3[0:1, :] + u * w3[1:2, :] + dn * w3[2:3, :])

    if conv:
        q_all = conv_silu(q_ref[0], qp_ref[0], qn_ref[0], cw_ref[0:3, :]) * (hd ** -0.5)
        k_all = conv_silu(k_ref[0], kp_ref[0], kn_ref[0], cw_ref[3:6, :])
        qo_ref, ko_ref = next(it), next(it)
        qo_ref[0] = q_all
        ko_ref[0] = k_all
    else:
        q_all, k_all = q_ref[0], k_ref[0]

    row = lax.broadcasted_iota(jnp.int32, (t, t), 0)
    col = lax.broadcasted_iota(jnp.int32, (t, t), 1)
    mask = (col >= row) if rev else (col <= row)
    last = 0 if rev else t - 1
    hs = range(heads)
    lanes = [slice(h * hd, (h + 1) * hd) for h in hs]
    q32 = [q_all[:, sl] for sl in lanes]
    qc = [_mx(x) for x in q32]
    kc = [k_all[:, sl] for sl in lanes]
    vc = [_mx(v_ref[0, :, sl]) for sl in lanes]
    bc_col = [gcol_ref[0, :, h:h + 1] for h in hs]
    i_col = [gcol_ref[0, :, heads + h:heads + h + 1] for h in hs]
    m_prev = [m_ref[0, h, 0:1, 0:1] for h in hs]
    c_mem = [c_ref[0, h] for h in hs]
    n_mem = [n_ref[0, h] for h in hs]
    dmat = [jnp.where(mask, bc_col[h] - grow_ref[0, 0, h:h + 1, :] + grow_ref[0, 0, heads + h:heads + h + 1, :],
                      -jnp.inf) for h in hs]
    inter = [bc_col[h] + m_prev[h] for h in hs]
    m_t = [jnp.maximum(jnp.max(dmat[h], axis=-1, keepdims=True), inter[h]) for h in hs]
    qk = [_nt(qc[h], _mx(kc[h])) for h in hs]
    q_c = [_nn(qc[h], _mx(c_mem[h])) for h in hs]
    s = [qk[h] * jnp.exp(dmat[h] - m_t[h]) for h in hs]
    w_inter = [jnp.exp(inter[h] - m_t[h]) for h in hs]
    num = [_nn(_mx(s[h]), vc[h]) + w_inter[h] * q_c[h] for h in hs]
    for h in hs:
        qn = jnp.sum(q32[h] * n_mem[h], axis=-1, keepdims=True)
        den = jnp.sum(s[h], axis=-1, keepdims=True) + w_inter[h] * qn
        h_out = num[h] * (1.0 / jnp.maximum(jnp.abs(den), jnp.exp(-m_t[h])))
        if has_acc:
            h_out = h_out + acc_ref[0, :, lanes[h]]
        h_ref[0, :, lanes[h]] = h_out
    for h in hs:
        total = bc_col[h][last:last + 1, :]
        g_col = total - bc_col[h] + i_col[h]
        m_new = jnp.maximum(total + m_prev[h], jnp.max(g_col, axis=0, keepdims=True))
        wk = jnp.exp(g_col - m_new)
        dec = jnp.exp(total + m_prev[h] - m_new)
        kw = kc[h] * wk
        c_ref[0, h] = dec * c_mem[h] + _tn(_mx(kw), vc[h])
        n_ref[0, h] = dec * n_mem[h] + jnp.sum(kw, axis=0, keepdims=True)
        m_ref[0, h] = jnp.broadcast_to(m_new, m_ref.shape[2:])


def _mlstm_scan(pq, pk, pv, conv_w, log_i, log_f, state, rev, acc=None, qk=None):
    b, l, _ = pq[0].shape
    w = conv_w.shape[-1]
    heads = ML_HEADS
    hd = w // heads
    t = min(ML_CHUNK, l)
    assert l % t == 0
    nc = l // t
    nb8 = l // 8
    lf = log_f.reshape(b, heads, nc, t)
    bcum = jnp.flip(jnp.cumsum(jnp.flip(lf, 3), axis=3), 3) if rev else jnp.cumsum(lf, axis=3)
    rows = jnp.concatenate([bcum, log_i.reshape(b, heads, nc, t)], axis=1)
    grow = rows.transpose(0, 2, 1, 3)
    gcol = rows.reshape(b, 2 * heads, l).transpose(0, 2, 1)
    ch = (lambda c: nc - 1 - c) if rev else (lambda c: c)
    cidx = lambda bi, c: (bi, ch(c), 0)
    ridx = lambda bi, c: (bi, ch(c), 0, 0)
    sidx = lambda bi, c: (bi, 0, 0, 0)
    c0, n0, m0 = state
    seq = pl.BlockSpec((1, t, w), cidx)
    col = lambda x: pl.BlockSpec((1, t, w), lambda bi, c: (bi, ch(c), x[1] // w))
    prv = lambda x: pl.BlockSpec((1, 8, w), lambda bi, c: (bi, jnp.maximum(ch(c) * (t // 8) - 1, 0), x[1] // w))
    nxt = lambda x: pl.BlockSpec((1, 8, w), lambda bi, c: (bi, jnp.minimum((ch(c) + 1) * (t // 8), nb8 - 1), x[1] // w))
    st_specs = [pl.BlockSpec((1, heads, hd, hd), sidx), pl.BlockSpec((1, heads, 1, hd), sidx),
                pl.BlockSpec((1, heads, 8, 128), sidx)]
    conv = qk is None
    if conv:
        in_specs = [col(pq), prv(pq), nxt(pq), col(pk), prv(pk), nxt(pk), col(pv),
                    pl.BlockSpec((6, w), lambda bi, c: (0, 0))]
        args = [pq[0], pq[0], pq[0], pk[0], pk[0], pk[0], pv[0], conv_w.reshape(6, w)]
    else:
        in_specs = [seq, seq, col(pv)]
        args = [qk[0], qk[1], pv[0]]
    in_specs += [pl.BlockSpec((1, t, 2 * heads), cidx), pl.BlockSpec((1, 1, 2 * heads, t), ridx)] + st_specs
    args += [gcol, grow, c0, n0, m0]
    if acc is not None:
        in_specs.append(seq)
        args.append(acc)
    seq_out = jax.ShapeDtypeStruct((b, l, w), F32)
    out = pl.pallas_call(
        functools.partial(_mlstm_kernel, rev=rev, heads=heads, hd=hd, nc=nc, has_acc=acc is not None, conv=conv),
        grid=(b, nc),
        in_specs=in_specs,
        out_specs=[seq] + st_specs + ([seq, seq] if conv else []),
        out_shape=[seq_out, jax.ShapeDtypeStruct(c0.shape, F32), jax.ShapeDtypeStruct(n0.shape, F32),
                   jax.ShapeDtypeStruct(m0.shape, F32)] + ([seq_out, seq_out] if conv else []),
        compiler_params=_params(("parallel", "arbitrary")),
        name="mlstm_rev" if rev else "mlstm_fwd",
    )(*args)
    return out[0], (out[1], out[2], out[3]), ((out[4], out[5]) if conv else qk)


def _rwkv_core(rows, s_ref, rev):
    c, w = rows[0][0].shape
    g_w = RK_GROUP
    nh = g_w // RK_HD
    assert c == RK_HD and c & (c - 1) == 0
    sh = c.bit_length() - 1
    rj = lax.broadcasted_iota(jnp.int32, (g_w, g_w), 0)
    cj = lax.broadcasted_iota(jnp.int32, (g_w, g_w), 1)
    blk = (rj >> sh) == (cj >> sh)
    tt, ss = rj & (c - 1), cj & (c - 1)
    incl = blk & ((ss >= tt) if rev else (ss <= tt))
    strict = blk & ((ss > tt) if rev else (ss < tt))
    eye = (rj == cj).astype(F32)
    tri = incl[:c, :c].astype(F32).astype(MXU_DT)
    lo_half = lax.broadcasted_iota(jnp.int32, (2 * g_w, 2 * c), 1) < c
    last = 0 if rev else c - 1
    zero = jnp.zeros((), F32)
    chains = [(bi, g) for bi in range(len(rows)) for g in range(w // g_w)]

    def bd(x):
        return _mx(jnp.where(blk, jnp.concatenate([x] * nh, axis=0), zero))

    def rsum(z):
        out = z[0:c]
        for i in range(1, nh):
            out = out + z[i * c:(i + 1) * c]
        return out

    a_bd, r_bd, v_bd, bk_t, r_t, v_g, bk_hat, dec = {}, {}, {}, {}, {}, {}, {}, {}
    for bi, (r, lw, kd, v, kk, a) in enumerate(rows):
        ka = kk * a
        gi = _cumsum_mm(tri, lw)
        tot = gi[last:last + 1, :]
        e_neg = jnp.exp(-gi)
        e_tail = jnp.exp(tot - gi)
        full = {'a': -kk * jnp.exp(gi - lw), 'b': ka * e_neg, 'k': kd * e_neg, 'r': r * jnp.exp(gi),
                'bh': ka * e_tail, 'kh': kd * e_tail, 'dec': jnp.exp(tot), 'v': v}
        for g in range(w // g_w):
            x = {n: t[:, g * g_w:(g + 1) * g_w] for n, t in full.items()}
            ch = (bi, g)
            a_bd[ch], r_bd[ch], v_bd[ch] = bd(x['a']), bd(x['r']), bd(x['v'])
            bk_t[ch] = _mx(jnp.concatenate([x['b'], x['k']], axis=0))
            bk_hat[ch] = _mx(jnp.concatenate([x['bh'], x['kh']], axis=0))
            r_t[ch], v_g[ch], dec[ch] = _mx(x['r']), x['v'], x['dec']

    l_ab, a_ak, a_rb, a_rk = {}, {}, {}, {}
    for ch in chains:
        prod = _nt(jnp.concatenate([a_bd[ch], r_bd[ch]], axis=0), bk_t[ch])
        swp = pltpu.roll(prod, c, 1)
        xb = jnp.where(lo_half, prod, swp)
        xk = jnp.where(lo_half, swp, prod)
        xb = jnp.concatenate([xb, xb], axis=1)
        xk = jnp.concatenate([xk, xk], axis=1)
        l_ab[ch] = jnp.where(strict, xb[:g_w], zero)
        a_ak[ch] = _mx(jnp.where(strict, xk[:g_w], zero))
        a_rb[ch] = _mx(rsum(jnp.where(incl, xb[g_w:], zero)))
        a_rk[ch] = _mx(rsum(jnp.where(incl, xk[g_w:], zero)))

    def live(z, k):
        return z if k == 0 else jnp.concatenate(
            [z[h * c + (0 if rev else k):h * c + (c - k if rev else c)] for h in range(nh)], axis=0)

    def spread(z, k):
        if k == 0:
            return z
        zeros = jnp.zeros((k, z.shape[1]), z.dtype)
        parts = [z[h * (c - k):(h + 1) * (c - k)] for h in range(nh)]
        return jnp.concatenate([x for p_ in parts for x in ((p_, zeros) if rev else (zeros, p_))], axis=0)

    xs = {ch: _mx(l_ab[ch]) for ch in chains}
    ps = {ch: eye + l_ab[ch] for ch in chains}
    for i in range(max(1, (c - 1).bit_length() - 1)):
        k = (2 ** (i + 1)) // 16 * 16
        xs = {ch: _mx(spread(_nn(live(xs[ch], k), xs[ch]), k)) for ch in chains}
        ps = {ch: ps[ch] + spread(_nn(live(_mx(ps[ch]), k), xs[ch]), k) for ch in chains}

    t_cat = {ch: _mx(rsum(ps[ch])) for ch in chains}
    w_cat = {ch: _mx(_nn(t_cat[ch], a_bd[ch])) for ch in chains}
    ta = {ch: _mx(_nn(t_cat[ch], a_ak[ch])) for ch in chains}
    u0 = {ch: _nn(ta[ch], v_bd[ch]) for ch in chains}
    y_k = {ch: _nn(a_rk[ch], v_bd[ch]) for ch in chains}
    s_mem = {ch: s_ref[ch[0], ch[1]] for ch in chains}
    s_b = {ch: _mx(s_mem[ch]) for ch in chains}
    u = {ch: _nt(w_cat[ch], s_b[ch]) + u0[ch] for ch in chains}
    y = {ch: _nt(r_t[ch], s_b[ch]) + _nn(a_rb[ch], bd(u[ch])) + y_k[ch] for ch in chains}
    for ch in chains:
        upd = _tn(_mx(jnp.concatenate([u[ch], v_g[ch]], axis=0)), bk_hat[ch])
        s_ref[ch[0], ch[1]] = s_mem[ch] * dec[ch] + jnp.where(blk, upd, zero)
    return [jnp.concatenate([y[(bi, g)] for g in range(w // g_w)], axis=1) for bi in range(len(rows))]


def _rwkv_seq_kernel(*refs, rev, has_acc):
    r_ref, lw_ref, kd_ref, v_ref, kk_ref, a_ref, s0_ref = refs[:7]
    acc_ref = refs[7] if has_acc else None
    y_ref, s_ref = refs[7 + has_acc:]

    @pl.when(pl.program_id(1) == 0)
    def _():
        s_ref[...] = s0_ref[...]

    y, = _rwkv_core([(r_ref[0], lw_ref[0], kd_ref[0], v_ref[0], kk_ref[0], a_ref[0])], s_ref, rev)
    if has_acc:
        y = y + acc_ref[0]
    y_ref[0] = y


def _rwkv_scan_seq(r, lw, kd, v, kk, a, state, rev, acc=None):
    b, l, w = r.shape
    c = RK_CHUNK
    nc = l // c
    cidx = (lambda bi, ch: (bi, nc - 1 - ch, 0)) if rev else (lambda bi, ch: (bi, ch, 0))
    seq = pl.BlockSpec((1, c, w), cidx)
    st = pl.BlockSpec((1,) + state.shape[1:], lambda bi, ch: (bi, 0, 0, 0))
    args = [r, lw, kd, v, kk, a, state] + ([acc] if acc is not None else [])
    y, s = pl.pallas_call(
        functools.partial(_rwkv_seq_kernel, rev=rev, has_acc=acc is not None),
        grid=(b, nc),
        in_specs=[seq] * 6 + [st] + ([seq] if acc is not None else []),
        out_specs=[seq, st],
        out_shape=[jax.ShapeDtypeStruct((b, l, w), F32), jax.ShapeDtypeStruct(state.shape, F32)],
        compiler_params=_params(("parallel", "arbitrary")),
        name="rwkv_seq_rev" if rev else "rwkv_seq_fwd",
    )(*args)
    return y, s


_PV_ROWS = ('mu_r', 'mu_k', 'mu_v', 'w0', 'a0', 'kk', 'ka', 'rk', 'v0')


def _rwkv_grid_kernel(*refs, rev, nc, has_vf, has_acc, emit_vf):
    it = iter(refs)
    pr, pr_u, pr_d, pk, pk_u, pk_d, pv, pv_u, pv_d, wa, wa_u, wa_d = (next(it) for _ in range(12))
    vf_ref, lora_ref = (next(it), next(it)) if has_vf else (None, None)
    pvec, mu_wa, w2_ref, a2_ref, s0_ref = (next(it) for _ in range(5))
    accy_ref, accb_ref = (next(it), next(it)) if has_acc else (None, None)
    y_ref, bon_ref, s_ref = (next(it) for _ in range(3))
    vfo_ref = next(it) if emit_vf else None
    step = pl.program_id(1)

    @pl.when(step == 0)
    def _():
        s_ref[...] = s0_ref[...]

    c = pr.shape[1]
    w = pr.shape[2]
    rd = w // 16
    ci = nc - 1 - step if rev else step
    has_up = jnp.where(ci > 0, 1.0, 0.0)
    has_dn = jnp.where(ci < nc - 1, 1.0, 0.0)
    trow = lax.broadcasted_iota(jnp.int32, (c, 1), 0)
    prow = lambda name: pvec[_PV_ROWS.index(name):_PV_ROWS.index(name) + 1, :]

    def left(x):
        return jnp.where(trow == 0, 0.0, pltpu.roll(x, 1, 0))

    def right(x):
        return jnp.where(trow == c - 1, 0.0, pltpu.roll(x, c - 1, 0))

    ones = _seg_ones(RK_HD)
    d = 1 if rev else 0
    rows, bons = [], []
    for bi in range(pr.shape[0]):
        def lerp_wide(cur_ref, up_ref, dn_ref, mu):
            cur = cur_ref[bi]
            q = w // 4
            sh = jnp.concatenate([left(cur[:, :q]), right(cur[:, q:2 * q]),
                                  up_ref[bi] * has_up, dn_ref[bi] * has_dn], axis=1)
            return cur + mu * (sh - cur)

        r = lerp_wide(pr, pr_u, pr_d, prow('mu_r'))
        k = lerp_wide(pk, pk_u, pk_d, prow('mu_k'))
        v = lerp_wide(pv, pv_u, pv_d, prow('mu_v'))

        cur = wa[bi]
        qd = (lax.broadcasted_iota(jnp.int32, cur.shape, 1) & (rd - 1)) >> ((rd // 4).bit_length() - 1)
        sh = jnp.where(qd == 0, left(cur), jnp.where(qd == 1, right(cur),
                                                     jnp.where(qd == 2, wa_u[bi] * has_up, wa_d[bi] * has_dn)))
        xwa = cur + mu_wa[...] * (sh - cur)
        xw = xwa[:, d * rd:(d + 1) * rd]
        xa = xwa[:, (2 + d) * rd:(3 + d) * rd]
        z = -(prow('w0') + _nn(_mx(jnp.tanh(xw)), w2_ref[...]))
        softplus = jnp.maximum(z, 0.0) + jnp.log(1.0 + jnp.exp(-jnp.abs(z)))
        lw = -jnp.exp(-softplus - 0.5)
        a = _sigmoid(prow('a0') + _nn(_mx(xa), a2_ref[...]))
        if has_vf:
            v = v + (vf_ref[bi] - v) * _sigmoid(prow('v0') + lora_ref[bi])
        if emit_vf:
            vfo_ref[bi] = v
        kkn = k * prow('kk')
        kk = kkn * lax.rsqrt(jnp.maximum(_segsum(kkn * kkn, ones), 1e-24))
        kd = k * (1.0 + (a - 1.0) * prow('ka'))
        bons.append(_segsum(r * kd * prow('rk'), ones) * v)
        rows.append((r, lw, kd, v, kk, a))
    ys = _rwkv_core(rows, s_ref, rev)
    for bi, (y, bon) in enumerate(zip(ys, bons)):
        if has_acc:
            y = y + accy_ref[bi]
            bon = bon + accb_ref[bi]
        y_ref[bi] = y
        bon_ref[bi] = bon


def _rwkv_scan_grid(pr, pk, pv, wa, pvec, mu_wa, w2, a2, state, rev, vf=None, lora=None,
                    acc=None, emit_vf=False):
    b, l, _ = pr[0].shape
    w = pvec.shape[1]
    c = RK_CHUNK
    assert c == GRID_W
    nc = l // c
    wq = w // 4
    ch = (lambda s: nc - 1 - s) if rev else (lambda s: s)
    cur = lambda lane: (lambda bi, s: (bi, ch(s), lane))
    upi = lambda lane: (lambda bi, s: (bi, jnp.maximum(ch(s) - 1, 0), lane))
    dni = lambda lane: (lambda bi, s: (bi, jnp.minimum(ch(s) + 1, nc - 1), lane))
    bb = RK_BATCH if b % RK_BATCH == 0 else 1
    seq = pl.BlockSpec((bb, c, w), cur(0))
    wide = lambda x: [pl.BlockSpec((bb, c, w), cur(x[1] // w)), pl.BlockSpec((bb, c, wq), upi(x[1] // wq + 2)),
                      pl.BlockSpec((bb, c, wq), dni(x[1] // wq + 3))]
    wa_w = mu_wa.shape[1]
    small = [pl.BlockSpec((bb, c, wa_w), cur(wa[1] // wa_w)), pl.BlockSpec((bb, c, wa_w), upi(wa[1] // wa_w)),
             pl.BlockSpec((bb, c, wa_w), dni(wa[1] // wa_w))]
    full = lambda x: pl.BlockSpec(x.shape, lambda bi, s: (0,) * x.ndim)
    st = pl.BlockSpec((bb,) + state.shape[1:], lambda bi, s: (bi, 0, 0, 0))
    in_specs = wide(pr) + wide(pk) + wide(pv) + small
    args = [pr[0]] * 3 + [pk[0]] * 3 + [pv[0]] * 3 + [wa[0]] * 3
    if vf is not None:
        in_specs += [seq, seq]
        args += [vf, lora]
    in_specs += [full(pvec), full(mu_wa), full(w2), full(a2), st]
    args += [pvec, mu_wa, w2, a2, state]
    if acc is not None:
        in_specs += [seq, seq]
        args += list(acc)
    out_specs = [seq, seq, st]
    out_shape = [jax.ShapeDtypeStruct((b, l, w), F32), jax.ShapeDtypeStruct((b, l, w), F32),
                 jax.ShapeDtypeStruct(state.shape, F32)]
    if emit_vf:
        out_specs.append(seq)
        out_shape.append(jax.ShapeDtypeStruct((b, l, w), F32))
    return pl.pallas_call(
        functools.partial(_rwkv_grid_kernel, rev=rev, nc=nc, has_vf=vf is not None,
                          has_acc=acc is not None, emit_vf=emit_vf),
        grid=(b // bb, nc),
        in_specs=in_specs,
        out_specs=out_specs,
        out_shape=out_shape,
        compiler_params=_params(("parallel", "arbitrary")),
        name="rwkv_grid_rev" if rev else "rwkv_grid_fwd",
    )(*args)


def _hgrn_kernel(*refs, rev, heads, hd, has_acc):
    q_ref, f_ref, v_ref, fb_ref, lb_ref, s0_ref = refs[:6]
    acc_ref = refs[6] if has_acc else None
    o_ref, s_ref = refs[6 + has_acc:]

    @pl.when(pl.program_id(1) == 0)
    def _():
        s_ref[...] = s0_ref[...]

    tt = q_ref.shape[1]
    sub = HG_SUB
    sh = sub.bit_length() - 1
    nsub = tt // sub
    mid = sub // 2 if rev else sub // 2 - 1
    last = 0 if rev else sub - 1
    ri = lax.broadcasted_iota(jnp.int32, (tt, tt), 0)
    ci = lax.broadcasted_iota(jnp.int32, (tt, tt), 1)
    causal = ((ri >> sh) == (ci >> sh)) & ((ci >= ri) if rev else (ci <= ri))
    lb = lb_ref[...]
    f = lb + (1.0 - lb) * _sigmoid(f_ref[0] + fb_ref[...])
    g_all = _cumsum_mm(causal.astype(F32).astype(MXU_DT), jnp.log(f))

    def row_of_each_sub(r):
        return jnp.concatenate([jnp.broadcast_to(g_all[j * sub + r:j * sub + r + 1], (sub, g_all.shape[1]))
                                for j in range(nsub)], axis=0)

    g_mid, g_last = row_of_each_sub(mid), row_of_each_sub(last)
    k_all = 1.0 - f
    q_all = _silu(q_ref[0])
    q1 = _mx(q_all * jnp.exp(g_all - g_mid))
    k1 = _mx(k_all * jnp.exp(g_mid - g_all))
    qg_f = q_all * jnp.exp(g_all)
    kl_f = k_all * jnp.exp(g_last - g_all)
    v_all = _mx(v_ref[0])
    assert nsub % 2 == 0
    is_first = lambda j: (j % 2 == 1) == rev
    dec = [jnp.exp(g_all[j * sub + last:j * sub + last + 1]) for j in range(nsub)]
    one_row = jnp.ones_like(dec[0])
    per_sub = lambda vals: jnp.concatenate([jnp.broadcast_to(x, (sub, x.shape[1])) for x in vals], axis=0)
    qg, kl = _mx(qg_f), _mx(kl_f)
    qg2 = _mx(qg_f * per_sub([one_row if is_first(j) else dec[j ^ 1] for j in range(nsub)]))
    kl2 = _mx(kl_f * per_sub([dec[j ^ 1] if is_first(j) else one_row for j in range(nsub)]))
    sub_r, sub_c = ri >> sh, ci >> sh
    second_sees_first = ((sub_r ^ 1) == sub_c) & ((sub_r & 1) == (0 if rev else 1))
    lanes = [slice(h * hd, (h + 1) * hd) for h in range(heads)]
    zero = jnp.zeros((), F32)
    att = [_mx(jnp.where(causal, _nt(q1[:, sl], k1[:, sl]), zero)
               + jnp.where(second_sees_first, _nt(qg[:, sl], kl[:, sl]), zero)) for sl in lanes]
    intra = [_nn(att[h], v_all[:, lanes[h]]) for h in range(heads)]
    pairs = [nsub // 2 - 1 - s if rev else s for s in range(nsub // 2)]
    rows = [slice(2 * p * sub, 2 * (p + 1) * sub) for p in range(nsub // 2)]
    upd = {(p, h): _tn(v_all[rows[p], lanes[h]], kl2[rows[p], lanes[h]]) for p in pairs for h in range(heads)}
    s_mem = [s_ref[0, h] for h in range(heads)]
    for p in pairs:
        dec_pair = dec[2 * p] * dec[2 * p + 1]
        outs = []
        for h in range(heads):
            outs.append(intra[h][rows[p]] + _nt(qg2[rows[p], lanes[h]], _mx(s_mem[h])))
            s_mem[h] = s_mem[h] * dec_pair[:, lanes[h]] + upd[(p, h)]
        o = jnp.concatenate(outs, axis=1)
        if has_acc:
            o = o + acc_ref[0, rows[p], :]
        o_ref[0, rows[p], :] = o
    for h in range(heads):
        s_ref[0, h] = s_mem[h]


def _hgrn_scan(pq, pf, pi, f_b, lb, state, rev, acc=None):
    b, l, _ = pq[0].shape
    w = f_b.shape[1]
    heads = HG_HEADS
    hd = w // heads
    tt = min(HG_TILE, l)
    nc = l // tt
    d = 1 if rev else 0
    ch = (lambda s: nc - 1 - s) if rev else (lambda s: s)
    col = lambda x, extra=0: pl.BlockSpec((1, tt, w), lambda bi, s: (bi, ch(s), x[1] // w + extra))
    seq = pl.BlockSpec((1, tt, w), lambda bi, s: (bi, ch(s), 0))
    vec = pl.BlockSpec((1, w), lambda bi, s: (0, 0))
    st = pl.BlockSpec((1,) + state.shape[1:], lambda bi, s: (bi, 0, 0, 0))
    args = [pq[0], pf[0], pi[0], f_b, lb, state] + ([acc] if acc is not None else [])
    o, s = pl.pallas_call(
        functools.partial(_hgrn_kernel, rev=rev, heads=heads, hd=hd, has_acc=acc is not None),
        grid=(b, nc),
        in_specs=[col(pq), col(pf, d), col(pi), vec, vec, st] + ([seq] if acc is not None else []),
        out_specs=[seq, st],
        out_shape=[jax.ShapeDtypeStruct((b, l, w), F32), jax.ShapeDtypeStruct(state.shape, F32)],
        compiler_params=_params(("parallel", "arbitrary")),
        name="hgrn_rev" if rev else "hgrn_fwd",
    )(*args)
    return o, s


_EV_ROWS = ('ml_norm_w', 'rk_ln_w', 'rk_ln_b', 'hg_norm_w', 'gate_b0', 'gate_b1', 'gate_b2', 'final_w', 'next_w')


def _epilogue_kernel(hm_ref, mo_ref, mz_ref, yr_ref, bon_ref, rz_ref, oh_ref, hz_ref,
                     g0_ref, g1_ref, g2_ref, xs_ref, gx_ref, ev_ref,
                     wpm_ref, wpr_ref, wph_ref, wout_ref, *rest, final, has_next):
    o_ref = rest[2 * has_next]
    ev = lambda name: ev_ref[_EV_ROWS.index(name):_EV_ROWS.index(name) + 1, :]
    d = hm_ref.shape[2]

    def seg_mean(x, seg):
        if seg % 128:
            return _segsum(x, _seg_ones(seg)) * (1.0 / seg)
        parts = [jnp.broadcast_to(jnp.mean(x[:, i:i + seg], axis=-1, keepdims=True), (x.shape[0], seg))
                 for i in range(0, d, seg)]
        return jnp.concatenate(parts, axis=1)

    seg = d // ML_HEADS
    y = hm_ref[0]
    y = y - seg_mean(y, seg)
    y = y * lax.rsqrt(seg_mean(y * y, seg) + NORM_EPS) * ev('ml_norm_w')
    u_m = _sigmoid(mo_ref[0].astype(F32)) * y * _silu(mz_ref[0].astype(F32))
    y = yr_ref[0]
    y = y - seg_mean(y, RK_HD)
    y = y * lax.rsqrt(seg_mean(y * y, RK_HD) + RK_GN_EPS) * ev('rk_ln_w') + ev('rk_ln_b') + bon_ref[0]
    u_r = y * _silu(rz_ref[0].astype(F32))
    seg = d // HG_HEADS
    o = oh_ref[0]
    o = o * lax.rsqrt(seg_mean(o * o, seg) + NORM_EPS) * ev('hg_norm_w')
    u_h = o * _silu(hz_ref[0].astype(F32))

    merged = (_sigmoid(g0_ref[0].astype(F32) + ev('gate_b0')) * _nn(_mx(u_m), wpm_ref[...])
              + _sigmoid(g1_ref[0].astype(F32) + ev('gate_b1')) * _nn(_mx(u_r), wpr_ref[...])
              + _sigmoid(g2_ref[0].astype(F32) + ev('gate_b2')) * _nn(_mx(u_h), wph_ref[...]))
    xs = xs_ref[0] + gx_ref[0] * _nn(_mx(merged), wout_ref[...])
    if final:
        xs = xs * lax.rsqrt(jnp.mean(xs * xs, axis=-1, keepdims=True) + NORM_EPS) * ev('final_w')
    o_ref[0] = xs
    if has_next:
        normed = xs * lax.rsqrt(jnp.mean(xs * xs, axis=-1, keepdims=True) + NORM_EPS) * ev('next_w')
        rest[3][0] = _mx(normed * rest[0][0] + rest[1][0])


def _epilogue(hm, mo, mz, yr, bon, rz, oh, hz, gate, xs, gx, p, final_w, nxt):
    b, l, d = xs.shape
    tt = min(EP_TILE, l)
    seq_out = pl.BlockSpec((1, tt, d), lambda bi, i: (bi, i, 0))
    seq = pl.BlockSpec((1, tt, d), lambda bi, i: (bi, i, 0), pipeline_mode=pl.Buffered(EP_IN_BUFFERS))
    col = lambda x, extra=0: pl.BlockSpec((1, tt, d), lambda bi, i: (bi, i, x[1] // d + extra),
                                          pipeline_mode=pl.Buffered(EP_IN_BUFFERS))
    ev = jnp.stack([p['ml_norm_w'], p['rk_ln_w'], p['rk_ln_b'], p['hg_norm_w'],
                    p['gate_b'][0], p['gate_b'][1], p['gate_b'][2],
                    final_w if final_w is not None else jnp.ones((d,), F32),
                    nxt[0] if nxt is not None else jnp.ones((d,), F32)], axis=0)
    wspec = pl.BlockSpec((d, d), lambda bi, i: (0, 0), pipeline_mode=pl.Buffered(1))
    ws = [_mx(p[n]) for n in ('w_pm', 'w_pr', 'w_ph', 'w_out')]
    per_batch = pl.BlockSpec((1, 1, d), lambda bi, i: (bi, 0, 0))
    has_next = nxt is not None
    out = pl.pallas_call(
        functools.partial(_epilogue_kernel, final=final_w is not None, has_next=has_next),
        grid=(b, l // tt),
        in_specs=[seq, col(mo), col(mz), seq, seq, col(rz), seq, col(hz), col(gate), col(gate, 1), col(gate, 2), seq,
                  per_batch, pl.BlockSpec(ev.shape, lambda bi, i: (0, 0))] + [wspec] * 4 + [per_batch] * (2 * has_next),
        out_specs=[seq_out] * (1 + has_next),
        out_shape=[jax.ShapeDtypeStruct((b, l, d), F32)] + [jax.ShapeDtypeStruct((b, l, d), MXU_DT)] * has_next,
        compiler_params=_params(("parallel", "parallel")),
        name="epilogue",
    )(hm, mo[0], mz[0], yr, bon, rz[0], oh, hz[0], gate[0], gate[0], gate[0], xs, gx, ev, *ws,
      *(nxt[1:] if has_next else ()))
    return out[0], (out[1] if has_next else None)


def _rms_norm(x, w):
    return x * lax.rsqrt(jnp.mean(x * x, axis=-1, keepdims=True) + NORM_EPS) * w


def _bi_shift_seq(u):
    half = u.shape[-1] // 2
    g = jnp.pad(u, ((0, 0), (1, 1), (0, 0)))
    return jnp.concatenate((g[:, :-2, :half], g[:, 2:, half:]), axis=-1)


def _in_layout(d):
    rd = d // 16
    return (('m_q', d), ('m_k', d), ('m_v', d), ('m_o', d), ('m_z', d), ('m_if', 4 * ML_HEADS),
            ('r_r', d), ('r_k', d), ('r_v', d), ('r_z', d), ('r_wa', 4 * rd),
            ('h_q', d), ('h_f', 2 * d), ('h_i', d), ('h_z', d), ('gate', 3 * d))


_COL_GROUPS = ((False, ('m_q', 'm_k', 'r_r', 'r_k', 'r_v', 'h_q', 'h_f')),
               (False, ('r_wa', 'm_if')),
               (True, ('m_v', 'h_i', 'm_o', 'm_z', 'r_z', 'h_z', 'gate')))
_COL_ALIGN = 512


def _pack_w_in(w_in):
    d = w_in.shape[0]
    src, off = {}, 0
    for name, width in _in_layout(d):
        src[name] = (off, width)
        off += width

    def pack(names):
        cols, first, o = [], {}, 0
        for n in names:
            cols.append(w_in[:, src[n][0]:src[n][0] + src[n][1]])
            first[n] = o
            o += src[n][1]
        if o % _COL_ALIGN:
            cols.append(jnp.zeros((d, -o % _COL_ALIGN), w_in.dtype))
        return jnp.concatenate(cols, axis=1).astype(MXU_DT), first

    return [pack(names) for _, names in _COL_GROUPS]


def _take(col, width):
    return col[0][..., col[1]:col[1] + width]


def _zero_states(b, d):
    ml_hd = d // ML_HEADS
    ml = (jnp.zeros((b, ML_HEADS, ml_hd, ml_hd), F32), jnp.zeros((b, ML_HEADS, 1, ml_hd), F32),
          jnp.zeros((b, ML_HEADS, 8, 128), F32))
    rk = jnp.zeros((b, d // RK_GROUP, RK_GROUP, RK_GROUP), F32)
    hg_hd = d // HG_HEADS
    hg = jnp.zeros((b, HG_HEADS, hg_hd, hg_hd), F32)
    return ((ml, ml), (rk, rk), (hg, hg))


def _mlstm_branch(proj, p, init, b, l):
    gl = _take(proj('m_if'), 4 * ML_HEADS) + p['ml_if_b'].reshape(-1)
    gl = gl.transpose(0, 2, 1).reshape(b, 2, 2, ML_HEADS, l)
    acc, qk, states = None, None, []
    for dr in (0, 1):
        log_i = gl[:, dr, 0]
        log_f = jax.nn.log_sigmoid(gl[:, dr, 1])
        acc, st, qk = _mlstm_scan(proj('m_q'), proj('m_k'), proj('m_v'), p['ml_conv'], log_i, log_f,
                                  init[dr], rev=bool(dr), acc=acc, qk=qk)
        states.append(st)
    return acc, tuple(states)


def _rwkv7_seq_branch(proj, h2, p, init, v_first, b, l, d):
    nh, n = d // RK_HD, RK_HD
    rd = d // 16
    mu = p['rk_mu']

    def lerp_shift(u, m):
        return u + m * (_bi_shift_seq(u) - u)

    r = lerp_shift(_take(proj('r_r'), d), mu[0:d])
    k = lerp_shift(_take(proj('r_k'), d), mu[d:2 * d])
    v = lerp_shift(_take(proj('r_v'), d), mu[2 * d:3 * d])
    pwa = _take(proj('r_wa'), 4 * rd)
    xwa = jnp.concatenate([lerp_shift(pwa[..., i * rd:(i + 1) * rd], mu[3 * d + i * rd:3 * d + (i + 1) * rd])
                           for i in range(4)], axis=-1)
    if p['rk_v0'] is None:
        v_first = v
    else:
        lora = _mm(_mm(h2, p['rk_v1']), p['rk_v2']).reshape(b, l, d)
        v = v + (v_first - v) * jax.nn.sigmoid(p['rk_v0'] + lora)
    kk = (k * p['rk_kk']).reshape(b, l, nh, n)
    kk = kk / jnp.maximum(jnp.sqrt(jnp.sum(kk * kk, axis=-1, keepdims=True)), 1e-12)
    kk = kk.reshape(b, l, d)
    acc, states, kd_sum = None, [], 0.0
    for dr in (0, 1):
        xw = xwa[..., dr * rd:(dr + 1) * rd]
        xa = xwa[..., (2 + dr) * rd:(3 + dr) * rd]
        log_w = -jax.nn.softplus(-(p['rk_w0'][dr] + _mm(jnp.tanh(xw).reshape(b * l, rd), p['rk_w2'][dr]).reshape(b, l, d))) - 0.5
        lw = -jnp.exp(log_w)
        a = jax.nn.sigmoid(p['rk_a0'][dr] + _mm(xa.reshape(b * l, rd), p['rk_a2'][dr]).reshape(b, l, d))
        kd = k * (1.0 + (a - 1.0) * p['rk_ka'])
        acc, s = _rwkv_scan_seq(r, lw, kd, v, kk, a, init[dr], rev=bool(dr), acc=acc)
        states.append(s)
        kd_sum = kd_sum + kd
    bonus = jnp.sum((r * kd_sum * p['rk_rk']).reshape(b, l, nh, n), axis=-1, keepdims=True) * v.reshape(b, l, nh, n)
    return acc, bonus.reshape(b, l, d), tuple(states), v_first


def _rwkv7_grid_branch(proj, h2, p, init, v_first, b, l, d):
    mu = p['rk_mu']
    zeros = jnp.zeros((d,), F32)
    has_vf = p['rk_v0'] is not None
    lora = _mm(_mm(h2, p['rk_v1']), p['rk_v2']).reshape(b, l, d) if has_vf else None
    acc, states, vf_out = None, [], v_first
    for dr in (0, 1):
        rows = {'mu_r': mu[0:d], 'mu_k': mu[d:2 * d], 'mu_v': mu[2 * d:3 * d], 'w0': p['rk_w0'][dr],
                'a0': p['rk_a0'][dr], 'kk': p['rk_kk'], 'ka': p['rk_ka'], 'rk': p['rk_rk'],
                'v0': p['rk_v0'] if has_vf else zeros}
        pvec = jnp.stack([rows[n] for n in _PV_ROWS], axis=0)
        emit_vf = (not has_vf) and dr == 0
        out = _rwkv_scan_grid(proj('r_r'), proj('r_k'), proj('r_v'), proj('r_wa'), pvec, mu[3 * d:][None],
                              _mx(p['rk_w2'][dr]), _mx(p['rk_a2'][dr]), init[dr], rev=bool(dr),
                              vf=v_first if has_vf else None, lora=lora, acc=acc, emit_vf=emit_vf)
        acc = (out[0], out[1])
        states.append(out[2])
        if emit_vf:
            vf_out = out[3]
    return acc[0], acc[1], tuple(states), vf_out


def _hgrn2_branch(proj, p, init):
    acc, states = None, []
    for dr in (0, 1):
        acc, s = _hgrn_scan(proj('h_q'), proj('h_f'), proj('h_i'), p['hg_f_b'][dr][None], p['hg_lb'][dr][None],
                            init[dr], rev=bool(dr), acc=acc)
        states.append(s)
    return acc, tuple(states)


def _mixer(h, p, init, on_grid, v_first, need_out, xs, gx, final_w, nxt):
    b, l, d = h.shape
    h2 = h.reshape(b * l, d)
    cols = {}
    for (w, first), (store_mx, _) in zip(p['w_in'], _COL_GROUPS):
        out = _mm(h2, w, MXU_DT if store_mx else F32).reshape(b, l, w.shape[1])
        cols.update({name: (out, off) for name, off in first.items()})

    def proj(name):
        return cols[name]

    hm, st_m = _mlstm_branch(proj, p, init[0], b, l)
    if on_grid:
        yr, bon, st_r, v_first = _rwkv7_grid_branch(proj, h2, p, init[1], v_first, b, l, d)
    else:
        yr, bon, st_r, v_first = _rwkv7_seq_branch(proj, h2, p, init[1], v_first, b, l, d)
    oh, st_h = _hgrn2_branch(proj, p, init[2])
    states = (st_m, st_r, st_h)
    if not need_out:
        return None, None, states, v_first
    out, h_next = _epilogue(hm, proj('m_o'), proj('m_z'), yr, bon, proj('r_z'), oh, proj('h_z'), proj('gate'),
                            xs, gx, p, final_w, nxt)
    return out, h_next, states, v_first


def kernel(x, c, ctx, c_ctx, norm_w, ada_w, ada_b, w_in, gate_b, ml_conv, ml_if_b, ml_norm_w, rk_mu, rk_w0, rk_w2, rk_a0, rk_a2, rk_kk, rk_ka, rk_rk, rk_v0, rk_v1, rk_v2, rk_ln_w, rk_ln_b, hg_f_b, hg_lb, hg_norm_w, w_pm, w_pr, w_ph, w_out, final_norm_w):
    batch, _, d = x.shape
    depth = w_in.shape[0]
    lb_p = jax.nn.softmax(hg_lb.astype(F32), axis=1)
    lower_bounds = jnp.cumsum(lb_p, axis=1) - lb_p[:, :1]
    xs, cs = x, ctx
    vf_x, vf_c = None, None
    cond = jnp.concatenate([jax.nn.silu(c), jax.nn.silu(c_ctx)[None]], axis=0)
    mods = []
    for l in range(depth):
        mod = _mm(cond, ada_w[l]) + ada_b[l]
        sh_x, sc_x, g_x = jnp.split(mod[:batch, None, :], 3, axis=-1)
        sh_c, sc_c, g_c = (jnp.broadcast_to(t[None, None, :], (batch, 1, d)) for t in jnp.split(mod[batch], 3, axis=-1))
        mods.append(((sh_x, 1.0 + sc_x, g_x), (sh_c, 1.0 + sc_c, g_c)))
    hx = _mx(_rms_norm(xs, norm_w[0]) * mods[0][0][1] + mods[0][0][0])
    hc = _mx(_rms_norm(cs, norm_w[0]) * mods[0][1][1] + mods[0][1][0])
    for l in range(depth):
        last = l == depth - 1
        p = {'w_in': _pack_w_in(w_in[l]), 'gate_b': gate_b[l], 'ml_conv': ml_conv[l], 'ml_if_b': ml_if_b[l],
             'ml_norm_w': ml_norm_w[l], 'rk_mu': rk_mu[l], 'rk_w0': rk_w0[l], 'rk_w2': rk_w2[l],
             'rk_a0': rk_a0[l], 'rk_a2': rk_a2[l], 'rk_kk': rk_kk[l], 'rk_ka': rk_ka[l], 'rk_rk': rk_rk[l],
             'rk_v0': rk_v0[l - 1] if l > 0 else None, 'rk_v1': rk_v1[l - 1] if l > 0 else None,
             'rk_v2': rk_v2[l - 1] if l > 0 else None, 'rk_ln_w': rk_ln_w[l], 'rk_ln_b': rk_ln_b[l],
             'hg_f_b': hg_f_b[l], 'hg_lb': lower_bounds[:, l], 'hg_norm_w': hg_norm_w[l],
             'w_pm': w_pm[l], 'w_pr': w_pr[l], 'w_ph': w_ph[l], 'w_out': w_out[l]}
        (_, _, gate_x), (_, _, gate_c) = mods[l]
        nxt_x = None if last else (norm_w[l + 1], mods[l + 1][0][1], mods[l + 1][0][0])
        nxt_c = None if last else (norm_w[l + 1], mods[l + 1][1][1], mods[l + 1][1][0])
        cs, hc, st_c, vf_c = _mixer(hc, p, _zero_states(batch, d), False, vf_c, not last, cs, gate_c, None, nxt_c)
        xs, hx, _, vf_x = _mixer(hx, p, st_c, True, vf_x, True, xs, gate_x, final_norm_w if last else None, nxt_x)
    return xs
```
